```python
import math
import jax, jax.numpy as jnp
from jax import lax
import numpy as np

D_MODEL = 1024
BATCH = 4
SEQ = 4096
DEPTH = 1
DEC_BATCH = 128
DEC_SEQ = 1
PAST_LEN = 2048
PAGE_SIZE = 128

SSM_WIDTH = D_MODEL // 2
SSM_GROUP = 16
SSM_GROUPS = SSM_WIDTH // SSM_GROUP
SSM_STATE = 64
DT_MIN = 0.001
DT_MAX = 0.1
EIG_CLIP = -1e-4
SB_HEAD_DIM = 64
SB_HEADS = (D_MODEL // 2) // SB_HEAD_DIM
SB_WIDTH = SB_HEADS * SB_HEAD_DIM
Q_BLOCK = 128
SB_BIAS_INIT = -7.0
SPLITS = (SSM_WIDTH, SSM_WIDTH + SB_WIDTH, SSM_WIDTH + 2 * SB_WIDTH, SSM_WIDTH + 3 * SB_WIDTH,
          SSM_WIDTH + 3 * SB_WIDTH + D_MODEL)
IN_COLS = SSM_WIDTH + 3 * SB_WIDTH + 2 * D_MODEL
N_EXPERTS = 64
TOP_K = 8
N_GROUPS = 8
TOPK_GROUPS = 4
EXPERT_HIDDEN = D_MODEL // 4
SHARED_HIDDEN = EXPERT_HIDDEN
ROUTED_SCALE = 2.5
RMS_EPS = 1e-6

kernel_name = 'hybrid_s5_stickbreaking_moe_step'


def rmsnorm(x, g):
    xf = x.astype(jnp.float32)
    r = lax.rsqrt(jnp.mean(xf * xf, axis=-1, keepdims=True) + RMS_EPS)
    return (xf * r).astype(x.dtype) * g


def _complex_affine_combine(e1, e2):
    ar1, ai1, br1, bi1 = e1
    ar2, ai2, br2, bi2 = e2
    return (ar2 * ar1 - ai2 * ai1,
            ar2 * ai1 + ai2 * ar1,
            ar2 * br1 - ai2 * bi1 + br2,
            ar2 * bi1 + ai2 * br1 + bi2)


def s5_mixer(u, x0_re, x0_im, lam_re, lam_im, log_dt, b_re, b_im, c_re, c_im, d_skip):
    f32 = jnp.float32
    bsz, t_len, _ = u.shape
    ug = u.astype(f32).reshape(bsz, t_len, SSM_GROUPS, SSM_GROUP)
    dt = jnp.exp(log_dt.astype(f32))[:, None]
    lre = jnp.minimum(lam_re.astype(f32), EIG_CLIP)
    lim = lam_im.astype(f32)
    mag = jnp.exp(lre * dt)
    a_re = mag * jnp.cos(lim * dt)
    a_im = mag * jnp.sin(lim * dt)
    den = lre * lre + lim * lim
    f_re = ((a_re - 1.0) * lre + a_im * lim) / den
    f_im = (a_im * lre - (a_re - 1.0) * lim) / den
    br = b_re.astype(f32)
    bi = b_im.astype(f32)
    bb_re = f_re[..., None] * br - f_im[..., None] * bi
    bb_im = f_re[..., None] * bi + f_im[..., None] * br
    bu_re = jnp.einsum('btgi,gpi->btgp', ug, bb_re)
    bu_im = jnp.einsum('btgi,gpi->btgp', ug, bb_im)
    s_re = x0_re.astype(f32)
    s_im = x0_im.astype(f32)
    bu_re = bu_re.at[:, 0].add(a_re * s_re - a_im * s_im)
    bu_im = bu_im.at[:, 0].add(a_re * s_im + a_im * s_re)
    ar = jnp.broadcast_to(a_re, bu_re.shape)
    ai = jnp.broadcast_to(a_im, bu_re.shape)
    _, _, xs_re, xs_im = lax.associative_scan(_complex_affine_combine, (ar, ai, bu_re, bu_im), axis=1)
    y = (jnp.einsum('btgp,gip->btgi', xs_re, c_re.astype(f32))
         - jnp.einsum('btgp,gip->btgi', xs_im, c_im.astype(f32)))
    y = y.reshape(bsz, t_len, SSM_WIDTH) + d_skip.astype(f32) * u.astype(f32)
    return y.astype(u.dtype), xs_re[:, -1].astype(x0_re.dtype), xs_im[:, -1].astype(x0_im.dtype)


def stick_breaking(q, k, v, sb_bias, q_pos, k_pos):
    z = (jnp.einsum('bqhd,bkhd->bhqk', q, k).astype(jnp.float32) * (SB_HEAD_DIM ** -0.5)
         + sb_bias.astype(jnp.float32)[None, :, None, None])
    mask = k_pos[None, :] < q_pos[:, None]
    log_stay = jnp.where(mask, jax.nn.log_sigmoid(-z), 0.0)
    log_after = lax.cumsum(log_stay, axis=3, reverse=True) - log_stay
    w = jnp.where(mask, jnp.exp(jax.nn.log_sigmoid(z) + log_after), 0.0)
    return jnp.einsum('bhqk,bkhd->bqhd', w.astype(v.dtype), v)


def sb_prompt(q, k, v, sb_bias):
    bsz, t_len, h, dh = q.shape
    nb = t_len // Q_BLOCK
    qb = q.reshape(bsz, nb, Q_BLOCK, h, dh).transpose(1, 0, 2, 3, 4)
    pos = jnp.arange(t_len, dtype=jnp.int32).reshape(nb, Q_BLOCK)
    k_pos = jnp.arange(t_len, dtype=jnp.int32)
    o = lax.map(lambda a: stick_breaking(a[0], k, v, sb_bias, a[1], k_pos), (qb, pos))
    return o.transpose(1, 0, 2, 3, 4).reshape(bsz, t_len, h, dh)


def sb_sample(q, k, v, sb_bias, k_past, v_past):
    past = k_past.shape[1]
    t_len = q.shape[1]
    k_all = jnp.concatenate([k_past, k], axis=1)
    v_all = jnp.concatenate([v_past, v], axis=1)
    q_pos = past + jnp.arange(t_len, dtype=jnp.int32)
    k_pos = jnp.arange(past + t_len, dtype=jnp.int32)
    return stick_breaking(q, k_all, v_all, sb_bias, q_pos, k_pos)


def moe(h, w_router, router_bias, w_gate_e, w_up_e, w_down_e, w_gate_s, w_up_s, w_down_s):
    shape = h.shape
    t = h.reshape(-1, D_MODEL)
    n_tok = t.shape[0]
    scores = jax.nn.sigmoid((t @ w_router).astype(jnp.float32))
    biased = scores + router_bias.astype(jnp.float32)
    grp = biased.reshape(n_tok, N_GROUPS, N_EXPERTS // N_GROUPS)
    grp_score = jnp.sum(lax.top_k(grp, 2)[0], axis=-1)
    _, grp_idx = lax.top_k(grp_score, TOPK_GROUPS)
    grp_mask = jnp.sum(jax.nn.one_hot(grp_idx, N_GROUPS, dtype=jnp.float32), axis=1)
    expert_mask = jnp.repeat(grp_mask, N_EXPERTS // N_GROUPS, axis=1)
    masked = jnp.where(expert_mask > 0, biased, -jnp.inf)
    _, idx = lax.top_k(masked, TOP_K)
    sel = jnp.take_along_axis(scores, idx, axis=-1)
    wts = sel / jnp.sum(sel, axis=-1, keepdims=True) * ROUTED_SCALE
    gates = jnp.sum(jax.nn.one_hot(idx, N_EXPERTS, dtype=jnp.float32) * wts[..., None], axis=1).astype(h.dtype)
    shared = (jax.nn.silu(t @ w_gate_s) * (t @ w_up_s)) @ w_down_s

    def expert_step(acc, ew):
        wg, wu, wd, g = ew
        y = (jax.nn.silu(t @ wg) * (t @ wu)) @ wd
        return acc + g[:, None] * y, None

    out, _ = lax.scan(expert_step, shared, (w_gate_e, w_up_e, w_down_e, gates.T))
    return out.reshape(shape)


def decoder_layer(x, c, ssm_re0, ssm_im0, k_past, v_past, p):
    bsz, t_len, _ = x.shape
    mod = jax.nn.silu(c) @ p['w_ada'] + p['b_ada']
    shift1, scale1, gate1, shift2, scale2, gate2 = jnp.split(mod[:, None, :], 6, axis=-1)
    h = rmsnorm(x, p['norm1_g']) * (1.0 + scale1) + shift1
    proj = h @ p['w_in']
    u, q, k, v, ga, gb = jnp.split(proj, SPLITS, axis=-1)
    q = rmsnorm(q.reshape(bsz, t_len, SB_HEADS, SB_HEAD_DIM), p['q_norm_g'])
    k = rmsnorm(k.reshape(bsz, t_len, SB_HEADS, SB_HEAD_DIM), p['k_norm_g'])
    v = v.reshape(bsz, t_len, SB_HEADS, SB_HEAD_DIM)
    y_a, st_re, st_im = s5_mixer(u, ssm_re0, ssm_im0, p['lam_re'], p['lam_im'], p['log_dt'],
                                 p['b_re'], p['b_im'], p['c_re'], p['c_im'], p['d_skip'])
    g = jax.nn.gelu(y_a)
    branch_a = (g @ p['w_glu_a']) * jax.nn.sigmoid(g @ p['w_glu_b'])
    if k_past is None:
        o = sb_prompt(q, k, v, p['sb_bias'])
    else:
        o = sb_sample(q, k, v, p['sb_bias'], k_past, v_past)
    branch_b = o.reshape(bsz, t_len, SB_WIDTH) @ p['w_sb_out']
    merged = jax.nn.sigmoid(ga) * branch_a + jax.nn.sigmoid(gb) * branch_b
    x = x + gate1 * (merged @ p['w_out'])
    h2 = rmsnorm(x, p['norm2_g']) * (1.0 + scale2) + shift2
    x = x + gate2 * moe(h2, p['w_router'], p['router_bias'], p['w_gate_e'], p['w_up_e'], p['w_down_e'],
                        p['w_gate_s'], p['w_up_s'], p['w_down_s'])
    return x, k, v, st_re, st_im


def setup_inputs(seed: int = 0) -> dict:
    key = jax.random.key(seed)
    ks = jax.random.split(key, 40)
    f32 = jnp.float32
    n_pages = PAST_LEN // PAGE_SIZE
    n_used = DEC_BATCH * n_pages
    n_pool = n_used + n_used // 4
    nrm = lambda i, shape, s: jax.random.normal(ks[i], shape, f32) * s
    perm = jax.random.permutation(ks[0], n_pool)
    page_table = perm[:n_used].reshape(DEC_BATCH, n_pages).astype(jnp.int32)
    lam_im0 = math.pi * jnp.arange(SSM_STATE, dtype=f32)[None, :]
    return {
        'x_prompt': nrm(1, (BATCH, SEQ, D_MODEL), 1.0),
        'x_sample': nrm(2, (DEC_BATCH, DEC_SEQ, D_MODEL), 1.0),
        'cache_k': nrm(3, (DEPTH, n_pool, PAGE_SIZE, SB_HEADS, SB_HEAD_DIM), 1.0),
        'cache_v': nrm(4, (DEPTH, n_pool, PAGE_SIZE, SB_HEADS, SB_HEAD_DIM), 1.0),
        'state_ssm_re': nrm(5, (DEPTH, DEC_BATCH, SSM_GROUPS, SSM_STATE), 1.0),
        'state_ssm_im': nrm(6, (DEPTH, DEC_BATCH, SSM_GROUPS, SSM_STATE), 1.0),
        'page_table': page_table,
        'c_prompt': nrm(7, (BATCH, D_MODEL), 1.0),
        'c_sample': nrm(8, (DEC_BATCH, D_MODEL), 1.0),
        'w_ada': nrm(9, (DEPTH, D_MODEL, 6 * D_MODEL), 0.2 * D_MODEL ** -0.5),
        'b_ada': nrm(10, (DEPTH, 6 * D_MODEL), 0.01),
        'norm1_g': 1.0 + nrm(11, (DEPTH, D_MODEL), 0.02),
        'norm2_g': 1.0 + nrm(12, (DEPTH, D_MODEL), 0.02),
        'w_in': nrm(13, (DEPTH, D_MODEL, IN_COLS), D_MODEL ** -0.5),
        'q_norm_g': 1.0 + nrm(14, (DEPTH, SB_HEAD_DIM), 0.02),
        'k_norm_g': 1.0 + nrm(15, (DEPTH, SB_HEAD_DIM), 0.02),
        'sb_bias': SB_BIAS_INIT + nrm(37, (DEPTH, SB_HEADS), 0.1),
        'ssm_lambda_re': -0.5 + nrm(16, (DEPTH, SSM_GROUPS, SSM_STATE), 0.01),
        'ssm_lambda_im': lam_im0 + nrm(17, (DEPTH, SSM_GROUPS, SSM_STATE), 0.01),
        'ssm_log_dt': jax.random.uniform(ks[18], (DEPTH, SSM_GROUPS), f32, math.log(DT_MIN), math.log(DT_MAX)),
        'ssm_b_re': nrm(19, (DEPTH, SSM_GROUPS, SSM_STATE, SSM_GROUP), (2 * SSM_GROUP) ** -0.5),
        'ssm_b_im': nrm(20, (DEPTH, SSM_GROUPS, SSM_STATE, SSM_GROUP), (2 * SSM_GROUP) ** -0.5),
        'ssm_c_re': nrm(21, (DEPTH, SSM_GROUPS, SSM_GROUP, SSM_STATE), (2 * SSM_STATE) ** -0.5),
        'ssm_c_im': nrm(22, (DEPTH, SSM_GROUPS, SSM_GROUP, SSM_STATE), (2 * SSM_STATE) ** -0.5),
        'ssm_d': nrm(23, (DEPTH, SSM_WIDTH), 1.0),
        'w_glu_a': nrm(24, (DEPTH, SSM_WIDTH, D_MODEL), SSM_WIDTH ** -0.5),
        'w_glu_b': nrm(25, (DEPTH, SSM_WIDTH, D_MODEL), SSM_WIDTH ** -0.5),
        'w_sb_out': nrm(26, (DEPTH, SB_WIDTH, D_MODEL), SB_WIDTH ** -0.5),
        'w_out': nrm(27, (DEPTH, D_MODEL, D_MODEL), D_MODEL ** -0.5),
        'w_router': nrm(28, (DEPTH, D_MODEL, N_EXPERTS), D_MODEL ** -0.5),
        'router_bias': nrm(29, (DEPTH, N_EXPERTS), 0.01),
        'w_gate_e': nrm(30, (DEPTH, N_EXPERTS, D_MODEL, EXPERT_HIDDEN), D_MODEL ** -0.5),
        'w_up_e': nrm(31, (DEPTH, N_EXPERTS, D_MODEL, EXPERT_HIDDEN), D_MODEL ** -0.5),
        'w_down_e': nrm(32, (DEPTH, N_EXPERTS, EXPERT_HIDDEN, D_MODEL), EXPERT_HIDDEN ** -0.5),
        'w_gate_s': nrm(33, (DEPTH, D_MODEL, SHARED_HIDDEN), D_MODEL ** -0.5),
        'w_up_s': nrm(34, (DEPTH, D_MODEL, SHARED_HIDDEN), D_MODEL ** -0.5),
        'w_down_s': nrm(35, (DEPTH, SHARED_HIDDEN, D_MODEL), SHARED_HIDDEN ** -0.5),
    }


def reference(x_prompt, x_sample, cache_k, cache_v, state_ssm_re, state_ssm_im, page_table, c_prompt, c_sample,
              w_ada, b_ada, norm1_g, norm2_g, w_in, q_norm_g, k_norm_g, sb_bias, ssm_lambda_re, ssm_lambda_im,
              ssm_log_dt, ssm_b_re, ssm_b_im, ssm_c_re, ssm_c_im, ssm_d, w_glu_a, w_glu_b, w_sb_out, w_out,
              w_router, router_bias, w_gate_e, w_up_e, w_down_e, w_gate_s, w_up_s, w_down_s):
    dec_batch, n_pages = page_table.shape
    y_p = x_prompt
    y_s = x_sample
    kp_l, vp_l, srp_l, sip_l, ksm_l, vsm_l, srs_l, sis_l = [], [], [], [], [], [], [], []
    for l in range(DEPTH):
        p = dict(w_ada=w_ada[l], b_ada=b_ada[l], norm1_g=norm1_g[l], norm2_g=norm2_g[l], w_in=w_in[l],
                 q_norm_g=q_norm_g[l], k_norm_g=k_norm_g[l], sb_bias=sb_bias[l], lam_re=ssm_lambda_re[l],
                 lam_im=ssm_lambda_im[l], log_dt=ssm_log_dt[l], b_re=ssm_b_re[l], b_im=ssm_b_im[l],
                 c_re=ssm_c_re[l], c_im=ssm_c_im[l], d_skip=ssm_d[l], w_glu_a=w_glu_a[l], w_glu_b=w_glu_b[l],
                 w_sb_out=w_sb_out[l], w_out=w_out[l], w_router=w_router[l], router_bias=router_bias[l],
                 w_gate_e=w_gate_e[l], w_up_e=w_up_e[l], w_down_e=w_down_e[l], w_gate_s=w_gate_s[l],
                 w_up_s=w_up_s[l], w_down_s=w_down_s[l])
        zeros = jnp.zeros((x_prompt.shape[0], SSM_GROUPS, SSM_STATE), x_prompt.dtype)
        y_p, kp, vp, srp, sip = decoder_layer(y_p, c_prompt, zeros, zeros, None, None, p)
        k_past = cache_k[l][page_table].reshape(dec_batch, n_pages * PAGE_SIZE, SB_HEADS, SB_HEAD_DIM)
        v_past = cache_v[l][page_table].reshape(dec_batch, n_pages * PAGE_SIZE, SB_HEADS, SB_HEAD_DIM)
        y_s, ksm, vsm, srs, sis = decoder_layer(y_s, c_sample, state_ssm_re[l], state_ssm_im[l], k_past, v_past, p)
        kp_l.append(kp); vp_l.append(vp); srp_l.append(srp); sip_l.append(sip)
        ksm_l.append(ksm); vsm_l.append(vsm); srs_l.append(srs); sis_l.append(sis)
    return (y_p, y_s, jnp.stack(kp_l), jnp.stack(vp_l), jnp.stack(srp_l), jnp.stack(sip_l),
            jnp.stack(ksm_l), jnp.stack(vsm_l), jnp.stack(srs_l), jnp.stack(sis_l))
```

```python
import functools
import math

import jax
import jax.numpy as jnp
from jax import lax
from jax.experimental import pallas as pl
from jax.experimental.pallas import tpu as pltpu

F32 = jnp.float32
BF16 = jnp.bfloat16

D_MODEL = 1024
SSM_WIDTH = 512
SSM_GROUP = 16
SSM_GROUPS = 32
SSM_STATE = 64
SSM_FLAT = SSM_GROUPS * SSM_STATE
EIG_CLIP = -1e-4
SB_HEADS = 8
SB_HEAD_DIM = 64
SB_WIDTH = 512
PAGE_SIZE = 128
N_EXPERTS = 64
TOP_K = 8
N_GROUPS = 8
TOPK_GROUPS = 4
GROUP_SIZE = N_EXPERTS // N_GROUPS
EXPERT_HIDDEN = 256
ROUTED_SCALE = 2.5
RMS_EPS = 1e-6

HEADS_PER_STEP = 4
HEAD_LANES = HEADS_PER_STEP * SB_HEAD_DIM
VMEM_LIMIT = 48 * 1024 * 1024


def _cparams(sem):
    return pltpu.CompilerParams(dimension_semantics=sem, vmem_limit_bytes=VMEM_LIMIT)


def _dot(a, b):
    return jnp.dot(a, b, preferred_element_type=F32)


def _dot_t(a, b):
    return lax.dot_general(a, b, (((1,), (1,)), ((), ())), preferred_element_type=F32)


def _split_bf16(x):
    hi = x.astype(BF16)
    lo = (x - hi.astype(F32)).astype(BF16)
    return hi, lo


def _silu(x):
    return x * jax.nn.sigmoid(x)


def _gelu_tanh(x):
    c = math.sqrt(2.0 / math.pi)
    return 0.5 * x * (1.0 + jnp.tanh(c * (x + 0.044715 * (x * x * x))))


def _ada_kernel(c_ref, w_ref, b_ref, o_ref):
    s = _silu(c_ref[...])
    s_hi, s_lo = _split_bf16(s)
    w_hi, w_lo = _split_bf16(w_ref[...])
    o_ref[...] = _dot(s_hi, w_hi) + _dot(s_hi, w_lo) + _dot(s_lo, w_hi) + b_ref[...]


def _ada_mod(c, w_ada, b_ada):
    rows, d = c.shape
    cols = w_ada.shape[1]
    tn = 1024
    return pl.pallas_call(
        _ada_kernel,
        out_shape=jax.ShapeDtypeStruct((rows, cols), F32),
        grid=(cols // tn,),
        in_specs=[pl.BlockSpec((rows, d), lambda j: (0, 0)),
                  pl.BlockSpec((d, tn), lambda j: (0, j)),
                  pl.BlockSpec((1, tn), lambda j: (0, j))],
        out_specs=pl.BlockSpec((rows, tn), lambda j: (0, j)),
        compiler_params=_cparams(("parallel",)),
        name="ada_mod",
    )(c, w_ada, b_ada.reshape(1, cols))


def _ssm_prep_kernel(lre_ref, lim_ref, ldt_ref, bre_ref, bim_ref,
                     are_ref, aim_ref, bbre_ref, bbim_ref):
    dt = jnp.exp(ldt_ref[...])
    lre = jnp.minimum(lre_ref[...], EIG_CLIP)
    lim = lim_ref[...]
    mag = jnp.exp(lre * dt)
    a_re = mag * jnp.cos(lim * dt)
    a_im = mag * jnp.sin(lim * dt)
    den = lre * lre + lim * lim
    f_re = ((a_re - 1.0) * lre + a_im * lim) / den
    f_im = (a_im * lre - (a_re - 1.0) * lim) / den
    br = bre_ref[...]
    bi = bim_ref[...]
    are_ref[...] = a_re
    aim_ref[...] = a_im
    bbre_ref[...] = f_re * br - f_im * bi
    bbim_ref[...] = f_re * bi + f_im * br


def _ssm_prep(lam_re, lam_im, log_dt, b_re, b_im):
    rep = lambda a: jnp.repeat(a.astype(F32), SSM_GROUP, axis=0)
    lre, lim = rep(lam_re), rep(lam_im)
    ldt = rep(jnp.broadcast_to(log_dt.astype(F32)[:, None], (SSM_GROUPS, SSM_STATE)))
    tr = lambda b: b.astype(F32).transpose(0, 2, 1).reshape(SSM_WIDTH, SSM_STATE)
    shp = jax.ShapeDtypeStruct((SSM_WIDTH, SSM_STATE), F32)
    a_re, a_im, bb_re, bb_im = pl.pallas_call(
        _ssm_prep_kernel, out_shape=(shp, shp, shp, shp), name="ssm_prep",
    )(lre, lim, ldt, tr(b_re), tr(b_im))
    eye = jnp.eye(SSM_GROUPS, dtype=F32)

    def blockdiag_in(bb):
        bb4 = bb.reshape(SSM_GROUPS, SSM_GROUP, 1, SSM_STATE) * eye.reshape(SSM_GROUPS, 1, SSM_GROUPS, 1)
        return bb4.reshape(SSM_WIDTH, SSM_FLAT).astype(BF16)

    a_re = a_re[::SSM_GROUP].reshape(1, SSM_FLAT)
    a_im = a_im[::SSM_GROUP].reshape(1, SSM_FLAT)
    return a_re, a_im, blockdiag_in(bb_re), blockdiag_in(bb_im)


def _blockdiag_out(c):
    eye = jnp.eye(SSM_GROUPS, dtype=F32)
    c4 = c.astype(F32).transpose(0, 2, 1).reshape(SSM_GROUPS, SSM_STATE, 1, SSM_GROUP)
    return (c4 * eye.reshape(SSM_GROUPS, 1, SSM_GROUPS, 1)).reshape(SSM_FLAT, SSM_WIDTH).astype(BF16)


def _inproj_kernel(x_ref, shift_ref, scale_ref, g1_ref, w_ref, qg_ref, kg_ref, hm_ref,
                   u_ref, q_ref, k_ref, kb_ref, v_ref, vb_ref, sga_ref, sgb_ref):
    x = x_ref[...]
    r = lax.rsqrt(jnp.mean(x * x, axis=-1, keepdims=True) + RMS_EPS)
    h = (x * r) * g1_ref[...]
    h = h * (1.0 + scale_ref[...]) + shift_ref[...]
    hb = h.astype(BF16)
    hm = hm_ref[...]

    def head_norm(t, g):
        ms = _dot((t * t).astype(BF16), hm)
        return (t * lax.rsqrt(ms + RMS_EPS)) * g

    u_ref[...] = _dot(hb, w_ref[:, 0:512])
    q = head_norm(_dot(hb, w_ref[:, 512:1024]), qg_ref[...])
    q_ref[...] = (q * (SB_HEAD_DIM ** -0.5)).astype(BF16)
    k = head_norm(_dot(hb, w_ref[:, 1024:1536]), kg_ref[...])
    k_ref[...] = k
    kb_ref[...] = k.astype(BF16)
    v = _dot(hb, w_ref[:, 1536:2048])
    v_ref[...] = v
    vb_ref[...] = v.astype(BF16)
    sga_ref[...] = jax.nn.sigmoid(_dot(hb, w_ref[:, 2048:3072])).astype(BF16)
    sgb_ref[...] = jax.nn.sigmoid(_dot(hb, w_ref[:, 3072:4096])).astype(BF16)


def _head_mean_matrix():
    idx = jnp.arange(SB_WIDTH) // SB_HEAD_DIM
    return ((idx[:, None] == idx[None, :]).astype(F32) / SB_HEAD_DIM).astype(BF16)


def _mod_spec(per_token, tm, rows_per_mod, chunk):
    if per_token:
        return pl.BlockSpec((tm, D_MODEL), lambda i: (i, chunk))
    return pl.BlockSpec((None, 1, D_MODEL), lambda i: (i // rows_per_mod, 0, chunk))


def _in_proj(x, mod, per_token, tm, norm1_g, w_in_b, q_g, k_g):
    n = x.shape[0]
    tiles_per_mod = (n // mod.shape[0]) // tm if not per_token else 1
    row = lambda i: (i, 0)
    const = lambda i: (0, 0)
    f32o = lambda w: jax.ShapeDtypeStruct((n, w), F32)
    b16o = lambda w: jax.ShapeDtypeStruct((n, w), BF16)
    blk = lambda w: pl.BlockSpec((tm, w), row)
    return pl.pallas_call(
        _inproj_kernel,
        out_shape=(f32o(512), b16o(512), f32o(512), b16o(512), f32o(512), b16o(512),
                   b16o(1024), b16o(1024)),
        grid=(n // tm,),
        in_specs=[blk(D_MODEL),
                  _mod_spec(per_token, tm, tiles_per_mod, 0),
                  _mod_spec(per_token, tm, tiles_per_mod, 1),
                  pl.BlockSpec((1, D_MODEL), const),
                  pl.BlockSpec(w_in_b.shape, const),
                  pl.BlockSpec((1, SB_WIDTH), const),
                  pl.BlockSpec((1, SB_WIDTH), const),
                  pl.BlockSpec((SB_WIDTH, SB_WIDTH), const)],
        out_specs=(blk(512), blk(512), blk(512), blk(512), blk(512), blk(512), blk(1024), blk(1024)),
        compiler_params=_cparams(("parallel",)),
        name="in_proj",
    )(x, mod, mod, norm1_g.reshape(1, D_MODEL), w_in_b,
      jnp.tile(q_g, SB_HEADS).reshape(1, SB_WIDTH), jnp.tile(k_g, SB_HEADS).reshape(1, SB_WIDTH),
      _head_mean_matrix())


def _ssm_drive(ub, bb_ref):
    half_c, half_s = SSM_WIDTH // 2, SSM_FLAT // 2
    lo = _dot(ub[:, :half_c], bb_ref[:half_c, :half_s])
    hi = _dot(ub[:, half_c:], bb_ref[half_c:, half_s:])
    return lo, hi


def _ssm_readout(xre, xim, ccre_ref, ccim_ref, d_ref, u):
    half_c, half_s = SSM_WIDTH // 2, SSM_FLAT // 2
    xr = xre.astype(BF16)
    xi = xim.astype(BF16)
    y_lo = _dot(xr[:, :half_s], ccre_ref[:half_s, :half_c]) + _dot(xi[:, :half_s], ccim_ref[:half_s, :half_c])
    y_hi = _dot(xr[:, half_s:], ccre_ref[half_s:, half_c:]) + _dot(xi[:, half_s:], ccim_ref[half_s:, half_c:])
    y = jnp.concatenate([y_lo, y_hi], axis=1) + d_ref[...] * u
    return _gelu_tanh(y)


def _ssm_prompt_kernel(u_ref, bbre_ref, bbim_ref, are_ref, aim_ref, ccre_ref, ccim_ref, d_ref,
                       g_ref, sre_ref, sim_ref, xre, xim, cre, cim, *, batch, lane_chunk):
    rows = u_ref.shape[0]
    half_s = SSM_FLAT // 2

    @pl.when(pl.program_id(0) == 0)
    def _():
        cre[...] = jnp.zeros_like(cre)
        cim[...] = jnp.zeros_like(cim)

    u = u_ref[...]
    ub = u.astype(BF16)
    lo, hi = _ssm_drive(ub, bbre_ref)
    xre[:, :half_s] = lo
    xre[:, half_s:] = hi
    lo, hi = _ssm_drive(ub, bbim_ref)
    xim[:, :half_s] = lo
    xim[:, half_s:] = hi

    first_step = lax.broadcasted_iota(jnp.int32, (8, lane_chunk), 0) < batch
    for c0 in range(0, SSM_FLAT, lane_chunk):
        cols = pl.ds(c0, lane_chunk)
        ar = jnp.broadcast_to(are_ref[:, cols], (8, lane_chunk))
        ai = jnp.broadcast_to(aim_ref[:, cols], (8, lane_chunk))

        def step(i, carry, cols=cols, ar=ar, ai=ai):
            c_r, c_i = carry
            r0 = pl.multiple_of(i * 8, 8)
            br = xre[pl.ds(r0, 8), cols]
            bi = xim[pl.ds(r0, 8), cols]
            p_r = pltpu.roll(c_r, batch, 0)
            p_i = pltpu.roll(c_i, batch, 0)
            y_r = ar * p_r - ai * p_i + br
            y_i = ar * p_i + ai * p_r + bi
            q_r = pltpu.roll(y_r, batch, 0)
            q_i = pltpu.roll(y_i, batch, 0)
            z_r = ar * q_r - ai * q_i + br
            z_i = ar * q_i + ai * q_r + bi
            xre[pl.ds(r0, 8), cols] = jnp.where(first_step, y_r, z_r)
            xim[pl.ds(r0, 8), cols] = jnp.where(first_step, y_i, z_i)
            return z_r, z_i

        c_r, c_i = lax.fori_loop(0, rows // 8, step, (cre[:, cols], cim[:, cols]))
        cre[:, cols] = c_r
        cim[:, cols] = c_i

    g_ref[...] = _ssm_readout(xre[...], xim[...], ccre_ref, ccim_ref, d_ref, u).astype(BF16)
    sre_ref[...] = cre[...]
    sim_ref[...] = cim[...]


def _ssm_prompt(u_tb, batch, ssm, tt=128, lane_chunk=512):
    a_re, a_im, bb_re, bb_im, cc_re, cc_im, d_skip = ssm
    assert 2 * batch == 8, "one sublane tile must hold exactly two time steps"
    n = u_tb.shape[0]
    rows = tt * batch
    const = lambda j: (0, 0)
    full = lambda a: pl.BlockSpec(a.shape, const)
    g, s_re, s_im = pl.pallas_call(
        functools.partial(_ssm_prompt_kernel, batch=batch, lane_chunk=lane_chunk),
        out_shape=(jax.ShapeDtypeStruct((n, SSM_WIDTH), BF16),
                   jax.ShapeDtypeStruct((8, SSM_FLAT), F32),
                   jax.ShapeDtypeStruct((8, SSM_FLAT), F32)),
        grid=(n // rows,),
        in_specs=[pl.BlockSpec((rows, SSM_WIDTH), lambda j: (j, 0)),
                  full(bb_re), full(bb_im), full(a_re), full(a_im), full(cc_re), full(cc_im), full(d_skip)],
        out_specs=(pl.BlockSpec((rows, SSM_WIDTH), lambda j: (j, 0)),
                   pl.BlockSpec((8, SSM_FLAT), const), pl.BlockSpec((8, SSM_FLAT), const)),
        scratch_shapes=[pltpu.VMEM((rows, SSM_FLAT), F32), pltpu.VMEM((rows, SSM_FLAT), F32),
                        pltpu.VMEM((8, SSM_FLAT), F32), pltpu.VMEM((8, SSM_FLAT), F32)],
        compiler_params=_cparams(("arbitrary",)),
        name="ssm_prompt",
    )(u_tb, bb_re, bb_im, a_re, a_im, cc_re, cc_im, d_skip)
    return g, s_re[batch:], s_im[batch:]


def _ssm_sample_kernel(u_ref, s0re_ref, s0im_ref, bbre_ref, bbim_ref, are_ref, aim_ref,
                       ccre_ref, ccim_ref, d_ref, g_ref, sre_ref, sim_ref):
    u = u_ref[...]
    ub = u.astype(BF16)
    ar, ai = are_ref[...], aim_ref[...]
    s_r, s_i = s0re_ref[...], s0im_ref[...]
    lo, hi = _ssm_drive(ub, bbre_ref)
    x_r = ar * s_r - ai * s_i + jnp.concatenate([lo, hi], axis=1)
    lo, hi = _ssm_drive(ub, bbim_ref)
    x_i = ar * s_i + ai * s_r + jnp.concatenate([lo, hi], axis=1)
    sre_ref[...] = x_r
    sim_ref[...] = x_i
    g_ref[...] = _ssm_readout(x_r, x_i, ccre_ref, ccim_ref, d_ref, u).astype(BF16)


def _ssm_sample(u, s_re, s_im, ssm):
    a_re, a_im, bb_re, bb_im, cc_re, cc_im, d_skip = ssm
    n = u.shape[0]
    return pl.pallas_call(
        _ssm_sample_kernel,
        out_shape=(jax.ShapeDtypeStruct((n, SSM_WIDTH), BF16),
                   jax.ShapeDtypeStruct((n, SSM_FLAT), F32),
                   jax.ShapeDtypeStruct((n, SSM_FLAT), F32)),
        compiler_params=pltpu.CompilerParams(vmem_limit_bytes=VMEM_LIMIT),
        name="ssm_sample",
    )(u, s_re, s_im, bb_re, bb_im, a_re, a_im, cc_re, cc_im, d_skip)


def _log_sigmoid_pair(z):
    soft = jnp.log1p(jnp.exp(-jnp.abs(z)))
    lsig = jnp.minimum(z, 0.0) - soft
    return lsig, lsig - z


def _attn_prompt_kernel(bias_ref, q_ref, k_ref, v_ref, tri_ref, o_ref, acc_ref, run_ref, *, tq, tk):
    hg = pl.program_id(1)
    qi = pl.program_id(2)
    nh = HEADS_PER_STEP
    q = q_ref[...]
    lane_head = lax.broadcasted_iota(jnp.int32, (1, HEAD_LANES), 1) // SB_HEAD_DIM
    qs = jnp.concatenate([q * (lane_head == h).astype(F32).astype(BF16) for h in range(nh)], axis=0)
    bias = [bias_ref[hg * nh + h] for h in range(nh)]
    tri = tri_ref[...]
    acc_ref[...] = jnp.zeros_like(acc_ref)
    run_ref[...] = jnp.zeros_like(run_ref)

    def block(j, masked):
        k0 = pl.multiple_of(j * tk, tk)
        kblk = k_ref[pl.ds(k0, tk), :]
        vblk = v_ref[pl.ds(k0, tk), :]
        zs = _dot_t(qs, kblk)
        lsig_parts, stay_parts = [], []
        for h in range(nh):
            lsig, stay = _log_sigmoid_pair(zs[h * tq:(h + 1) * tq] + bias[h])
            lsig_parts.append(lsig)
            stay_parts.append(stay)
        lsig = jnp.concatenate(lsig_parts, axis=0)
        stay = jnp.concatenate(stay_parts, axis=0)
        if masked:
            kpos = k0 + lax.broadcasted_iota(jnp.int32, (nh * tq, tk), 1)
            qpos = qi * tq + (lax.broadcasted_iota(jnp.int32, (nh * tq, tk), 0) & (tq - 1))
            valid = kpos < qpos
            stay = jnp.where(valid, stay, 0.0)
        s_hi, s_lo = _split_bf16(stay)
        after = _dot(s_hi, tri) + _dot(s_lo, tri)
        run = run_ref[...]
        w = jnp.exp(lsig + after + run)
        if masked:
            w = jnp.where(valid, w, 0.0)
        acc_ref[...] += _dot(w.astype(BF16), vblk)
        run_ref[...] = run + jnp.sum(stay, axis=1, keepdims=True)

    j_diag = (qi * tq) // tk
    block(j_diag, True)

    def body(jj, carry):
        block(j_diag - 1 - jj, False)
        return carry

    lax.fori_loop(0, j_diag, body, 0)
    acc = acc_ref[...]
    out = jnp.zeros((tq, HEAD_LANES), F32)
    for h in range(nh):
        out = jnp.where(lane_head == h, acc[h * tq:(h + 1) * tq], out)
    o_ref[...] = out.astype(BF16)


def _strict_upper(tk):
    idx = jnp.arange(tk)
    return (idx[:, None] > idx[None, :]).astype(BF16)


def _attn_prompt(q, kb, vb, sb_bias, batch, t_len, tq=128, tk=256):
    n = q.shape[0]
    nq = t_len // tq
    ngrp = SB_HEADS // HEADS_PER_STEP
    return pl.pallas_call(
        functools.partial(_attn_prompt_kernel, tq=tq, tk=tk),
        out_shape=jax.ShapeDtypeStruct((n, SB_WIDTH), BF16),
        grid=(batch, ngrp, nq),
        in_specs=[pl.BlockSpec(memory_space=pltpu.SMEM),
                  pl.BlockSpec((tq, HEAD_LANES), lambda b, g, i: (b * nq + i, g)),
                  pl.BlockSpec((t_len, HEAD_LANES), lambda b, g, i: (b, g)),
                  pl.BlockSpec((t_len, HEAD_LANES), lambda b, g, i: (b, g)),
                  pl.BlockSpec((tk, tk), lambda b, g, i: (0, 0))],
        out_specs=pl.BlockSpec((tq, HEAD_LANES), lambda b, g, i: (b * nq + i, g)),
        scratch_shapes=[pltpu.VMEM((HEADS_PER_STEP * tq, HEAD_LANES), F32),
                        pltpu.VMEM((HEADS_PER_STEP * tq, 1), F32)],
        compiler_params=_cparams(("parallel", "parallel", "arbitrary")),
        name="attn_prompt",
    )(sb_bias.astype(F32), q, kb, vb, _strict_upper(tk))


def _attn_sample_kernel(pt_ref, q_ref, bias_ref, tri_ref, *refs, n_pages):
    del pt_ref
    k_refs = refs[:n_pages]
    v_refs = refs[n_pages:2 * n_pages]
    o_ref = refs[2 * n_pages]
    q = q_ref[0]
    head_of_lane = lax.broadcasted_iota(jnp.int32, (SB_HEADS, SB_WIDTH), 1) // SB_HEAD_DIM
    own = head_of_lane == lax.broadcasted_iota(jnp.int32, (SB_HEADS, SB_WIDTH), 0)
    qrows = jnp.where(own, jnp.broadcast_to(q.astype(F32), (SB_HEADS, SB_WIDTH)), 0.0).astype(BF16)
    bias = bias_ref[...]
    tri = tri_ref[...]
    run = jnp.zeros((SB_HEADS, 1), F32)
    acc = jnp.zeros((SB_HEADS, SB_WIDTH), F32)
    for p in reversed(range(n_pages)):
        kp = k_refs[p][0].astype(BF16)
        vp = v_refs[p][0].astype(BF16)
        z = _dot_t(qrows, kp) + bias
        lsig, stay = _log_sigmoid_pair(z)
        s_hi, s_lo = _split_bf16(stay)
        after = _dot(s_hi, tri) + _dot(s_lo, tri)
        w = jnp.exp(lsig + after + run)
        acc = acc + _dot(w.astype(BF16), vp)
        run = run + jnp.sum(stay, axis=1, keepdims=True)
    o_ref[0] = jnp.sum(jnp.where(own, acc, 0.0), axis=0, keepdims=True).astype(BF16)


def _attn_sample(q, cache_k, cache_v, page_table, sb_bias):
    n, n_pages = page_table.shape
    n_pool = cache_k.shape[0]
    ck = cache_k.reshape(n_pool, PAGE_SIZE, SB_WIDTH)
    cv = cache_v.reshape(n_pool, PAGE_SIZE, SB_WIDTH)
    page_spec = lambda p: pl.BlockSpec((1, PAGE_SIZE, SB_WIDTH), lambda i, pt, p=p: (pt[i * n_pages + p], 0, 0))
    tok_spec = pl.BlockSpec((1, 1, SB_WIDTH), lambda i, pt: (i, 0, 0))
    out = pl.pallas_call(
        functools.partial(_attn_sample_kernel, n_pages=n_pages),
        out_shape=jax.ShapeDtypeStruct((n, 1, SB_WIDTH), BF16),
        grid_spec=pltpu.PrefetchScalarGridSpec(
            num_scalar_prefetch=1,
            grid=(n,),
            in_specs=[tok_spec,
                      pl.BlockSpec((SB_HEADS, 1), lambda i, pt: (0, 0)),
                      pl.BlockSpec((PAGE_SIZE, PAGE_SIZE), lambda i, pt: (0, 0))]
                     + [page_spec(p) for p in range(n_pages)] * 2,
            out_specs=tok_spec),
        compiler_params=_cparams(("parallel",)),
        name="attn_sample",
    )(page_table.reshape(-1).astype(jnp.int32), q.reshape(n, 1, SB_WIDTH),
      sb_bias.astype(F32).reshape(SB_HEADS, 1), _strict_upper(PAGE_SIZE),
      *([ck] * n_pages), *([cv] * n_pages))
    return out.reshape(n, SB_WIDTH)


def _first_index_of_max(vals, row_id, n_rows):
    m = jnp.max(vals, axis=0, keepdims=True)
    first = jnp.min(jnp.where(vals == m, row_id, n_rows), axis=0, keepdims=True)
    return m, first


def _route(logits_t, rbias):
    tm = logits_t.shape[1]
    scores = jax.nn.sigmoid(logits_t)
    biased = scores + rbias
    row8 = lax.broadcasted_iota(jnp.int32, (GROUP_SIZE, tm), 0)
    neg = jnp.float32(-jnp.inf)
    grp_score = []
    for g in range(N_GROUPS):
        blk = biased[g * GROUP_SIZE:(g + 1) * GROUP_SIZE]
        m1, first = _first_index_of_max(blk, row8, GROUP_SIZE)
        m2 = jnp.max(jnp.where(row8 == first, neg, blk), axis=0, keepdims=True)
        grp_score.append(m1 + m2)
    masked = []
    for g in range(N_GROUPS):
        beaten_by = jnp.zeros((1, tm), jnp.int32)
        for o in range(N_GROUPS):
            if o == g:
                continue
            wins = (grp_score[o] > grp_score[g]) if o > g else (grp_score[o] >= grp_score[g])
            beaten_by = beaten_by + wins.astype(jnp.int32)
        keep = beaten_by < TOPK_GROUPS
        masked.append(jnp.where(keep, biased[g * GROUP_SIZE:(g + 1) * GROUP_SIZE], neg))
    cur = jnp.concatenate(masked, axis=0)
    row = lax.broadcasted_iota(jnp.int32, (N_EXPERTS, tm), 0)
    chosen = jnp.zeros((N_EXPERTS, tm), jnp.bool_)
    for _ in range(TOP_K):
        _, first = _first_index_of_max(cur, row, N_EXPERTS)
        pick = row == first
        chosen = jnp.logical_or(chosen, pick)
        cur = jnp.where(pick, neg, cur)
    sel = jnp.where(chosen, scores, 0.0)
    return sel / jnp.sum(sel, axis=0, keepdims=True) * ROUTED_SCALE


def _post_kernel(x_ref, g_ref, o_ref, sga_ref, sgb_ref, gate1_ref, shift2_ref, scale2_ref, g2_ref,
                 wga_ref, wgb_ref, wsb_ref, wout_ref, wr_ref, rb_ref,
                 x1_ref, h2_ref, gates_ref):
    g = g_ref[...]
    branch_a = _dot(g, wga_ref[...]) * jax.nn.sigmoid(_dot(g, wgb_ref[...]))
    branch_b = _dot(o_ref[...], wsb_ref[...])
    merged = sga_ref[...].astype(F32) * branch_a + sgb_ref[...].astype(F32) * branch_b
    x1 = x_ref[...] + gate1_ref[...] * _dot(merged.astype(BF16), wout_ref[...])
    x1_ref[...] = x1
    r = lax.rsqrt(jnp.mean(x1 * x1, axis=-1, keepdims=True) + RMS_EPS)
    h2 = (x1 * r) * g2_ref[...]
    h2 = h2 * (1.0 + scale2_ref[...]) + shift2_ref[...]
    h2_ref[...] = h2.astype(BF16)
    h_hi, h_lo = _split_bf16(h2)
    w_hi, w_lo = _split_bf16(wr_ref[...])
    logits_t = _dot_t(w_hi, h_hi) + _dot_t(w_hi, h_lo) + _dot_t(w_lo, h_hi)
    gates_ref[...] = _route(logits_t, rb_ref[...])


def _post_mixer(x, g, o, sga, sgb, mod, per_token, tm, p):
    n = x.shape[0]
    tiles_per_mod = (n // mod.shape[0]) // tm if not per_token else 1
    row = lambda i: (i, 0)
    const = lambda i: (0, 0)
    blk = lambda w: pl.BlockSpec((tm, w), row)
    full = lambda a: pl.BlockSpec(a.shape, const)
    ms = lambda c: _mod_spec(per_token, tm, tiles_per_mod, c)
    weights = (p["w_glu_a"], p["w_glu_b"], p["w_sb_out"], p["w_out"], p["w_router_t"], p["router_bias"])
    return pl.pallas_call(
        _post_kernel,
        out_shape=(jax.ShapeDtypeStruct((n, D_MODEL), F32),
                   jax.ShapeDtypeStruct((n, D_MODEL), BF16),
                   jax.ShapeDtypeStruct((N_EXPERTS, n), F32)),
        grid=(n // tm,),
        in_specs=[blk(D_MODEL), blk(512), blk(512), blk(1024), blk(1024), ms(2), ms(3), ms(4),
                  pl.BlockSpec((1, D_MODEL), const)] + [full(w) for w in weights],
        out_specs=(blk(D_MODEL), blk(D_MODEL), pl.BlockSpec((N_EXPERTS, tm), lambda i: (0, i))),
        compiler_params=_cparams(("parallel",)),
        name="post_mixer",
    )(x, g, o, sga, sgb, mod, mod, mod, p["norm2_g"], *weights)


def _moe_kernel(h_ref, gates_ref, x1_ref, gate2_ref, wgs_ref, wus_ref, wds_ref,
                wg_ref, wu_ref, wd_ref, y_ref, acc_ref):
    e = pl.program_id(1)
    h = h_ref[...]

    @pl.when(e == 0)
    def _():
        act = _silu(_dot(h, wgs_ref[...])) * _dot(h, wus_ref[...])
        acc_ref[...] = _dot(act.astype(BF16), wds_ref[...])

    act = _silu(_dot(h, wg_ref[...])) * _dot(h, wu_ref[...])
    pick = (lax.broadcasted_iota(jnp.int32, (N_EXPERTS, 128), 0) == e).astype(BF16)
    g_hi, g_lo = _split_bf16(gates_ref[...])
    gcol = _dot(g_hi, pick) + _dot(g_lo, pick)
    act = act * jnp.concatenate([gcol, gcol], axis=1)
    acc_ref[...] += _dot(act.astype(BF16), wd_ref[...])

    @pl.when(e == pl.num_programs(1) - 1)
    def _():
        y_ref[...] = x1_ref[...] + gate2_ref[...] * acc_ref[...]


def _moe(h2, gates, x1, mod, per_token, tm, p):
    n = h2.shape[0]
    tiles_per_mod = (n // mod.shape[0]) // tm if not per_token else 1
    row = lambda i, e: (i, 0)
    const = lambda i, e: (0, 0)
    exp = lambda i, e: (e, 0, 0)
    if per_token:
        gate2_spec = pl.BlockSpec((tm, D_MODEL), lambda i, e: (i, 5))
    else:
        gate2_spec = pl.BlockSpec((None, 1, D_MODEL), lambda i, e: (i // tiles_per_mod, 0, 5))
    return pl.pallas_call(
        _moe_kernel,
        out_shape=jax.ShapeDtypeStruct((n, D_MODEL), F32),
        grid=(n // tm, N_EXPERTS),
        in_specs=[pl.BlockSpec((tm, D_MODEL), row),
                  pl.BlockSpec((tm, N_EXPERTS), row),
                  pl.BlockSpec((tm, D_MODEL), row),
                  gate2_spec,
                  pl.BlockSpec((D_MODEL, EXPERT_HIDDEN), const),
                  pl.BlockSpec((D_MODEL, EXPERT_HIDDEN), const),
                  pl.BlockSpec((EXPERT_HIDDEN, D_MODEL), const),
                  pl.BlockSpec((None, D_MODEL, EXPERT_HIDDEN), exp),
                  pl.BlockSpec((None, D_MODEL, EXPERT_HIDDEN), exp),
                  pl.BlockSpec((None, EXPERT_HIDDEN, D_MODEL), exp)],
        out_specs=pl.BlockSpec((tm, D_MODEL), row),
        scratch_shapes=[pltpu.VMEM((tm, D_MODEL), F32)],
        compiler_params=_cparams(("parallel", "arbitrary")),
        name="moe",
    )(h2, gates, x1, mod, p["w_gate_s"], p["w_up_s"], p["w_down_s"], p["w_gate_e"], p["w_up_e"], p["w_down_e"])


def _layer(x_prompt, x_sample, cache_k, cache_v, s0_re, s0_im, page_table, c_prompt, c_sample, p):
    batch, t_len, d = x_prompt.shape
    n_s = x_sample.shape[0]
    n_p = batch * t_len

    c_all = jnp.concatenate([c_prompt, c_sample], axis=0)
    pad = (-c_all.shape[0]) % 8
    c_all = jnp.pad(c_all, ((0, pad), (0, 0)))
    mod = _ada_mod(c_all, p["w_ada"], p["b_ada"])
    mod_p = mod[:batch].reshape(batch, 1, 6 * d)
    mod_s = mod[batch:batch + n_s]

    a_re, a_im, bb_re, bb_im = _ssm_prep(p["lam_re"], p["lam_im"], p["log_dt"], p["b_re"], p["b_im"])
    ssm = (a_re, a_im, bb_re, bb_im, _blockdiag_out(p["c_re"]), _blockdiag_out(-p["c_im"]),
           p["d_skip"].astype(F32).reshape(1, SSM_WIDTH))

    xp = x_prompt.reshape(n_p, d)
    xs = x_sample.reshape(n_s, d)
    proj_p = _in_proj(xp, mod_p, False, 512, p["norm1_g"], p["w_in"], p["q_norm_g"], p["k_norm_g"])
    proj_s = _in_proj(xs, mod_s, True, n_s, p["norm1_g"], p["w_in"], p["q_norm_g"], p["k_norm_g"])
    u_p, q_p, k_p, kb_p, v_p, vb_p, sga_p, sgb_p = proj_p
    u_s, q_s, k_s, _, v_s, _, sga_s, sgb_s = proj_s

    u_tb = u_p.reshape(batch, t_len, SSM_WIDTH).transpose(1, 0, 2).reshape(n_p, SSM_WIDTH)
    g_tb, st_re_p, st_im_p = _ssm_prompt(u_tb, batch, ssm)
    g_p = g_tb.reshape(t_len, batch, SSM_WIDTH).transpose(1, 0, 2).reshape(n_p, SSM_WIDTH)
    g_s, st_re_s, st_im_s = _ssm_sample(u_s, s0_re.reshape(n_s, SSM_FLAT), s0_im.reshape(n_s, SSM_FLAT), ssm)

    o_p = _attn_prompt(q_p, kb_p, vb_p, p["sb_bias"], batch, t_len)
    o_s = _attn_sample(q_s, cache_k, cache_v, page_table, p["sb_bias"])

    x1_p, h2_p, gates_p = _post_mixer(xp, g_p, o_p, sga_p, sgb_p, mod_p, False, 512, p)
    x1_s, h2_s, gates_s = _post_mixer(xs, g_s, o_s, sga_s, sgb_s, mod_s, True, n_s, p)

    y_p = _moe(h2_p, gates_p.T, x1_p, mod_p, False, 1024, p)
    y_s = _moe(h2_s, gates_s.T, x1_s, mod_s, True, n_s, p)

    hd = (SB_HEADS, SB_HEAD_DIM)
    gs = (SSM_GROUPS, SSM_STATE)
    return (y_p.reshape(batch, t_len, d), y_s.reshape(n_s, 1, d),
            k_p.reshape(batch, t_len, *hd), v_p.reshape(batch, t_len, *hd),
            st_re_p.reshape(batch, *gs), st_im_p.reshape(batch, *gs),
            k_s.reshape(n_s, 1, *hd), v_s.reshape(n_s, 1, *hd),
            st_re_s.reshape(n_s, *gs), st_im_s.reshape(n_s, *gs))


def kernel(x_prompt, x_sample, cache_k, cache_v, state_ssm_re, state_ssm_im, page_table, c_prompt, c_sample,
           w_ada, b_ada, norm1_g, norm2_g, w_in, q_norm_g, k_norm_g, sb_bias, ssm_lambda_re, ssm_lambda_im,
           ssm_log_dt, ssm_b_re, ssm_b_im, ssm_c_re, ssm_c_im, ssm_d, w_glu_a, w_glu_b, w_sb_out, w_out,
           w_router, router_bias, w_gate_e, w_up_e, w_down_e, w_gate_s, w_up_s, w_down_s):
    depth = w_in.shape[0]
    assert depth == 1, "single-layer step"
    l = 0
    p = dict(
        w_ada=w_ada[l], b_ada=b_ada[l],
        norm1_g=norm1_g[l], norm2_g=norm2_g[l].reshape(1, D_MODEL),
        w_in=w_in[l].astype(BF16), q_norm_g=q_norm_g[l], k_norm_g=k_norm_g[l], sb_bias=sb_bias[l],
        lam_re=ssm_lambda_re[l], lam_im=ssm_lambda_im[l], log_dt=ssm_log_dt[l],
        b_re=ssm_b_re[l], b_im=ssm_b_im[l], c_re=ssm_c_re[l], c_im=ssm_c_im[l], d_skip=ssm_d[l],
        w_glu_a=w_glu_a[l].astype(BF16), w_glu_b=w_glu_b[l].astype(BF16),
        w_sb_out=w_sb_out[l].astype(BF16), w_out=w_out[l].astype(BF16),
        w_router_t=w_router[l].T, router_bias=router_bias[l].reshape(N_EXPERTS, 1),
        w_gate_e=w_gate_e[l].astype(BF16), w_up_e=w_up_e[l].astype(BF16), w_down_e=w_down_e[l].astype(BF16),
        w_gate_s=w_gate_s[l].astype(BF16), w_up_s=w_up_s[l].astype(BF16), w_down_s=w_down_s[l].astype(BF16),
    )
    outs = _layer(x_prompt, x_sample, cache_k[l], cache_v[l], state_ssm_re[l], state_ssm_im[l],
                  page_table, c_prompt, c_sample, p)
    y_p, y_s = outs[0], outs[1]
    return (y_p, y_s) + tuple(o[None] for o in outs[2:])
```

```python
import functools
import math

import jax
import jax.numpy as jnp
from jax import lax
from jax.experimental import pallas as pl
from jax.experimental.pallas import tpu as pltpu

F32 = jnp.float32
BF16 = jnp.bfloat16

D_MODEL = 1024
SSM_WIDTH = 512
SSM_GROUP = 16
SSM_GROUPS = 32
SSM_STATE = 64
SSM_FLAT = SSM_GROUPS * SSM_STATE
EIG_CLIP = -1e-4
SB_HEADS = 8
SB_HEAD_DIM = 64
SB_WIDTH = 512
PAGE_SIZE = 128
N_EXPERTS = 64
TOP_K = 8
N_GROUPS = 8
TOPK_GROUPS = 4
GROUP_SIZE = N_EXPERTS // N_GROUPS
EXPERT_HIDDEN = 256
ROUTED_SCALE = 2.5
RMS_EPS = 1e-6

HEADS_PER_STEP = 4
HEAD_LANES = HEADS_PER_STEP * SB_HEAD_DIM
VMEM_LIMIT = 48 * 1024 * 1024


def _cparams(sem):
    return pltpu.CompilerParams(dimension_semantics=sem, vmem_limit_bytes=VMEM_LIMIT)


def _dot(a, b):
    return jnp.dot(a, b, preferred_element_type=F32)


def _dot_t(a, b):
    return lax.dot_general(a, b, (((1,), (1,)), ((), ())), preferred_element_type=F32)


def _split_bf16(x):
    hi = x.astype(BF16)
    lo = (x - hi.astype(F32)).astype(BF16)
    return hi, lo


def _silu(x):
    return x * jax.nn.sigmoid(x)


def _gelu_tanh(x):
    c = math.sqrt(2.0 / math.pi)
    return 0.5 * x * (1.0 + jnp.tanh(c * (x + 0.044715 * (x * x * x))))


def _ada_kernel(c_ref, w_ref, b_ref, o_ref):
    s = _silu(c_ref[...])
    s_hi, s_lo = _split_bf16(s)
    w_hi, w_lo = _split_bf16(w_ref[...])
    o_ref[...] = _dot(s_hi, w_hi) + _dot(s_hi, w_lo) + _dot(s_lo, w_hi) + b_ref[...]


def _ada_mod(c, w_ada, b_ada):
    rows, d = c.shape
    cols = w_ada.shape[1]
    tn = 1024
    return pl.pallas_call(
        _ada_kernel,
        out_shape=jax.ShapeDtypeStruct((rows, cols), F32),
        grid=(cols // tn,),
        in_specs=[pl.BlockSpec((rows, d), lambda j: (0, 0)),
                  pl.BlockSpec((d, tn), lambda j: (0, j)),
                  pl.BlockSpec((1, tn), lambda j: (0, j))],
        out_specs=pl.BlockSpec((rows, tn), lambda j: (0, j)),
        compiler_params=_cparams(("parallel",)),
        name="ada_mod",
    )(c, w_ada, b_ada.reshape(1, cols))


def _ssm_prep_kernel(lre_ref, lim_ref, ldt_ref, bre_ref, bim_ref,
                     are_ref, aim_ref, bbre_ref, bbim_ref):
    dt = jnp.exp(ldt_ref[...])
    lre = jnp.minimum(lre_ref[...], EIG_CLIP)
    lim = lim_ref[...]
    mag = jnp.exp(lre * dt)
    a_re = mag * jnp.cos(lim * dt)
    a_im = mag * jnp.sin(lim * dt)
    den = lre * lre + lim * lim
    f_re = ((a_re - 1.0) * lre + a_im * lim) / den
    f_im = (a_im * lre - (a_re - 1.0) * lim) / den
    br = bre_ref[...]
    bi = bim_ref[...]
    are_ref[...] = a_re
    aim_ref[...] = a_im
    bbre_ref[...] = f_re * br - f_im * bi
    bbim_ref[...] = f_re * bi + f_im * br


def _ssm_prep(lam_re, lam_im, log_dt, b_re, b_im):
    rep = lambda a: jnp.repeat(a.astype(F32), SSM_GROUP, axis=0)
    lre, lim = rep(lam_re), rep(lam_im)
    ldt = rep(jnp.broadcast_to(log_dt.astype(F32)[:, None], (SSM_GROUPS, SSM_STATE)))
    tr = lambda b: b.astype(F32).transpose(0, 2, 1).reshape(SSM_WIDTH, SSM_STATE)
    shp = jax.ShapeDtypeStruct((SSM_WIDTH, SSM_STATE), F32)
    a_re, a_im, bb_re, bb_im = pl.pallas_call(
        _ssm_prep_kernel, out_shape=(shp, shp, shp, shp), name="ssm_prep",
    )(lre, lim, ldt, tr(b_re), tr(b_im))
    eye = jnp.eye(SSM_GROUPS, dtype=F32)

    def blockdiag_in(bb):
        bb4 = bb.reshape(SSM_GROUPS, SSM_GROUP, 1, SSM_STATE) * eye.reshape(SSM_GROUPS, 1, SSM_GROUPS, 1)
        return bb4.reshape(SSM_WIDTH, SSM_FLAT).astype(BF16)

    a_re = a_re[::SSM_GROUP].reshape(1, SSM_FLAT)
    a_im = a_im[::SSM_GROUP].reshape(1, SSM_FLAT)
    return a_re, a_im, blockdiag_in(bb_re), blockdiag_in(bb_im)


def _blockdiag_out(c):
    eye = jnp.eye(SSM_GROUPS, dtype=F32)
    c4 = c.astype(F32).transpose(0, 2, 1).reshape(SSM_GROUPS, SSM_STATE, 1, SSM_GROUP)
    return (c4 * eye.reshape(SSM_GROUPS, 1, SSM_GROUPS, 1)).reshape(SSM_FLAT, SSM_WIDTH).astype(BF16)


def _inproj_kernel(x_ref, shift_ref, scale_ref, g1_ref, w_ref, qg_ref, kg_ref, hm_ref,
                   u_ref, q_ref, k_ref, kb_ref, v_ref, vb_ref, sga_ref, sgb_ref):
    x = x_ref[...]
    r = lax.rsqrt(jnp.mean(x * x, axis=-1, keepdims=True) + RMS_EPS)
    h = (x * r) * g1_ref[...]
    h = h * (1.0 + scale_ref[...]) + shift_ref[...]
    hb = h.astype(BF16)
    hm = hm_ref[...]

    def head_norm(t, g):
        ms = _dot((t * t).astype(BF16), hm)
        return (t * lax.rsqrt(ms + RMS_EPS)) * g

    u_ref[...] = _dot(hb, w_ref[:, 0:512])
    q = head_norm(_dot(hb, w_ref[:, 512:1024]), qg_ref[...])
    q_ref[...] = (q * (SB_HEAD_DIM ** -0.5)).astype(BF16)
    k = head_norm(_dot(hb, w_ref[:, 1024:1536]), kg_ref[...])
    k_ref[...] = k
    kb_ref[...] = k.astype(BF16)
    v = _dot(hb, w_ref[:, 1536:2048])
    v_ref[...] = v
    vb_ref[...] = v.astype(BF16)
    sga_ref[...] = jax.nn.sigmoid(_dot(hb, w_ref[:, 2048:3072])).astype(BF16)
    sgb_ref[...] = jax.nn.sigmoid(_dot(hb, w_ref[:, 3072:4096])).astype(BF16)


def _head_mean_matrix():
    idx = jnp.arange(SB_WIDTH) // SB_HEAD_DIM
    return ((idx[:, None] == idx[None, :]).astype(F32) / SB_HEAD_DIM).astype(BF16)


def _mod_spec(per_token, tm, rows_per_mod, chunk):
    if per_token:
        return pl.BlockSpec((tm, D_MODEL), lambda i: (i, chunk))
    return pl.BlockSpec((None, 1, D_MODEL), lambda i: (i // rows_per_mod, 0, chunk))


def _in_proj(x, mod, per_token, tm, norm1_g, w_in_b, q_g, k_g):
    n = x.shape[0]
    tiles_per_mod = (n // mod.shape[0]) // tm if not per_token else 1
    row = lambda i: (i, 0)
    const = lambda i: (0, 0)
    f32o = lambda w: jax.ShapeDtypeStruct((n, w), F32)
    b16o = lambda w: jax.ShapeDtypeStruct((n, w), BF16)
    blk = lambda w: pl.BlockSpec((tm, w), row)
    return pl.pallas_call(
        _inproj_kernel,
        out_shape=(f32o(512), b16o(512), f32o(512), b16o(512), f32o(512), b16o(512),
                   b16o(1024), b16o(1024)),
        grid=(n // tm,),
        in_specs=[blk(D_MODEL),
                  _mod_spec(per_token, tm, tiles_per_mod, 0),
                  _mod_spec(per_token, tm, tiles_per_mod, 1),
                  pl.BlockSpec((1, D_MODEL), const),
                  pl.BlockSpec(w_in_b.shape, const),
                  pl.BlockSpec((1, SB_WIDTH), const),
                  pl.BlockSpec((1, SB_WIDTH), const),
                  pl.BlockSpec((SB_WIDTH, SB_WIDTH), const)],
        out_specs=(blk(512), blk(512), blk(512), blk(512), blk(512), blk(512), blk(1024), blk(1024)),
        compiler_params=_cparams(("parallel",)),
        name="in_proj",
    )(x, mod, mod, norm1_g.reshape(1, D_MODEL), w_in_b,
      jnp.tile(q_g, SB_HEADS).reshape(1, SB_WIDTH), jnp.tile(k_g, SB_HEADS).reshape(1, SB_WIDTH),
      _head_mean_matrix())


def _ssm_drive(ub, bb_ref):
    half_c, half_s = SSM_WIDTH // 2, SSM_FLAT // 2
    lo = _dot(ub[:, :half_c], bb_ref[:half_c, :half_s])
    hi = _dot(ub[:, half_c:], bb_ref[half_c:, half_s:])
    return lo, hi


def _ssm_readout(xre, xim, ccre_ref, ccim_ref, d_ref, u):
    half_c, half_s = SSM_WIDTH // 2, SSM_FLAT // 2
    xr = xre.astype(BF16)
    xi = xim.astype(BF16)
    y_lo = _dot(xr[:, :half_s], ccre_ref[:half_s, :half_c]) + _dot(xi[:, :half_s], ccim_ref[:half_s, :half_c])
    y_hi = _dot(xr[:, half_s:], ccre_ref[half_s:, half_c:]) + _dot(xi[:, half_s:], ccim_ref[half_s:, half_c:])
    y = jnp.concatenate([y_lo, y_hi], axis=1) + d_ref[...] * u
    return _gelu_tanh(y)


def _ssm_prompt_kernel(u_ref, bbre_ref, bbim_ref, are_ref, aim_ref, ccre_ref, ccim_ref, d_ref,
                       g_ref, sre_ref, sim_ref, xre, xim, cre, cim, *, batch, lane_chunk):
    rows = u_ref.shape[0]
    half_s = SSM_FLAT // 2

    @pl.when(pl.program_id(0) == 0)
    def _():
        cre[...] = jnp.zeros_like(cre)
        cim[...] = jnp.zeros_like(cim)

    u = u_ref[...]
    ub = u.astype(BF16)
    lo, hi = _ssm_drive(ub, bbre_ref)
    xre[:, :half_s] = lo
    xre[:, half_s:] = hi
    lo, hi = _ssm_drive(ub, bbim_ref)
    xim[:, :half_s] = lo
    xim[:, half_s:] = hi

    first_step = lax.broadcasted_iota(jnp.int32, (8, lane_chunk), 0) < batch
    for c0 in range(0, SSM_FLAT, lane_chunk):
        cols = pl.ds(c0, lane_chunk)
        ar = jnp.broadcast_to(are_ref[:, cols], (8, lane_chunk))
        ai = jnp.broadcast_to(aim_ref[:, cols], (8, lane_chunk))

        def step(i, carry, cols=cols, ar=ar, ai=ai):
            c_r, c_i = carry
            r0 = pl.multiple_of(i * 8, 8)
            br = xre[pl.ds(r0, 8), cols]
            bi = xim[pl.ds(r0, 8), cols]
            p_r = pltpu.roll(c_r, batch, 0)
            p_i = pltpu.roll(c_i, batch, 0)
            y_r = ar * p_r - ai * p_i + br
            y_i = ar * p_i + ai * p_r + bi
            q_r = pltpu.roll(y_r, batch, 0)
            q_i = pltpu.roll(y_i, batch, 0)
            z_r = ar * q_r - ai * q_i + br
            z_i = ar * q_i + ai * q_r + bi
            xre[pl.ds(r0, 8), cols] = jnp.where(first_step, y_r, z_r)
            xim[pl.ds(r0, 8), cols] = jnp.where(first_step, y_i, z_i)
            return z_r, z_i

        c_r, c_i = lax.fori_loop(0, rows // 8, step, (cre[:, cols], cim[:, cols]))
        cre[:, cols] = c_r
        cim[:, cols] = c_i

    g_ref[...] = _ssm_readout(xre[...], xim[...], ccre_ref, ccim_ref, d_ref, u).astype(BF16)
    sre_ref[...] = cre[...]
    sim_ref[...] = cim[...]


def _ssm_prompt(u_tb, batch, ssm, tt=128, lane_chunk=512):
    a_re, a_im, bb_re, bb_im, cc_re, cc_im, d_skip = ssm
    assert 2 * batch == 8, "one sublane tile must hold exactly two time steps"
    n = u_tb.shape[0]
    rows = tt * batch
    const = lambda j: (0, 0)
    full = lambda a: pl.BlockSpec(a.shape, const)
    g, s_re, s_im = pl.pallas_call(
        functools.partial(_ssm_prompt_kernel, batch=batch, lane_chunk=lane_chunk),
        out_shape=(jax.ShapeDtypeStruct((n, SSM_WIDTH), BF16),
                   jax.ShapeDtypeStruct((8, SSM_FLAT), F32),
                   jax.ShapeDtypeStruct((8, SSM_FLAT), F32)),
        grid=(n // rows,),
        in_specs=[pl.BlockSpec((rows, SSM_WIDTH), lambda j: (j, 0)),
                  full(bb_re), full(bb_im), full(a_re), full(a_im), full(cc_re), full(cc_im), full(d_skip)],
        out_specs=(pl.BlockSpec((rows, SSM_WIDTH), lambda j: (j, 0)),
                   pl.BlockSpec((8, SSM_FLAT), const), pl.BlockSpec((8, SSM_FLAT), const)),
        scratch_shapes=[pltpu.VMEM((rows, SSM_FLAT), F32), pltpu.VMEM((rows, SSM_FLAT), F32),
                        pltpu.VMEM((8, SSM_FLAT), F32), pltpu.VMEM((8, SSM_FLAT), F32)],
        compiler_params=_cparams(("arbitrary",)),
        name="ssm_prompt",
    )(u_tb, bb_re, bb_im, a_re, a_im, cc_re, cc_im, d_skip)
    return g, s_re[batch:], s_im[batch:]


def _ssm_sample_kernel(u_ref, s0re_ref, s0im_ref, bbre_ref, bbim_ref, are_ref, aim_ref,
                       ccre_ref, ccim_ref, d_ref, g_ref, sre_ref, sim_ref):
    u = u_ref[...]
    ub = u.astype(BF16)
    ar, ai = are_ref[...], aim_ref[...]
    s_r, s_i = s0re_ref[...], s0im_ref[...]
    lo, hi = _ssm_drive(ub, bbre_ref)
    x_r = ar * s_r - ai * s_i + jnp.concatenate([lo, hi], axis=1)
    lo, hi = _ssm_drive(ub, bbim_ref)
    x_i = ar * s_i + ai * s_r + jnp.concatenate([lo, hi], axis=1)
    sre_ref[...] = x_r
    sim_ref[...] = x_i
    g_ref[...] = _ssm_readout(x_r, x_i, ccre_ref, ccim_ref, d_ref, u).astype(BF16)


def _ssm_sample(u, s_re, s_im, ssm):
    a_re, a_im, bb_re, bb_im, cc_re, cc_im, d_skip = ssm
    n = u.shape[0]
    return pl.pallas_call(
        _ssm_sample_kernel,
        out_shape=(jax.ShapeDtypeStruct((n, SSM_WIDTH), BF16),
                   jax.ShapeDtypeStruct((n, SSM_FLAT), F32),
                   jax.ShapeDtypeStruct((n, SSM_FLAT), F32)),
        compiler_params=pltpu.CompilerParams(vmem_limit_bytes=VMEM_LIMIT),
        name="ssm_sample",
    )(u, s_re, s_im, bb_re, bb_im, a_re, a_im, cc_re, cc_im, d_skip)


def _log_sigmoid_pair(z):
    soft = jnp.log(1.0 + jnp.exp(-jnp.abs(z)))
    lsig = jnp.minimum(z, 0.0) - soft
    return lsig, lsig - z


def _attn_prompt_kernel(bias_ref, q_ref, k_ref, v_ref, tri_ref, o_ref, acc_ref, run_ref, *, tq, tk):
    qi = pl.program_id(1)
    nh = HEADS_PER_STEP
    ngrp = SB_HEADS // nh
    lane_head = lax.broadcasted_iota(jnp.int32, (1, HEAD_LANES), 1) // SB_HEAD_DIM
    head_mask = [(lane_head == h).astype(F32).astype(BF16) for h in range(nh)]
    qs, bias = [], []
    for g in range(ngrp):
        q = q_ref[:, g * HEAD_LANES:(g + 1) * HEAD_LANES]
        qs.append(jnp.concatenate([q * head_mask[h] for h in range(nh)], axis=0))
        bias.append([bias_ref[g * nh + h] for h in range(nh)])
    tri = tri_ref[...]
    acc_ref[...] = jnp.zeros_like(acc_ref)
    run_ref[...] = jnp.zeros_like(run_ref)

    def block(j, masked):
        k0 = pl.multiple_of(j * tk, tk)
        if masked:
            kpos = k0 + lax.broadcasted_iota(jnp.int32, (nh * tq, tk), 1)
            qpos = qi * tq + (lax.broadcasted_iota(jnp.int32, (nh * tq, tk), 0) & (tq - 1))
            valid = kpos < qpos
        for g in range(ngrp):
            lanes = slice(g * HEAD_LANES, (g + 1) * HEAD_LANES)
            zs = _dot_t(qs[g], k_ref[pl.ds(k0, tk), lanes])
            lsig_parts, stay_parts = [], []
            for h in range(nh):
                lsig, stay = _log_sigmoid_pair(zs[h * tq:(h + 1) * tq] + bias[g][h])
                lsig_parts.append(lsig)
                stay_parts.append(stay)
            lsig = jnp.concatenate(lsig_parts, axis=0)
            stay = jnp.concatenate(stay_parts, axis=0)
            if masked:
                stay = jnp.where(valid, stay, 0.0)
            after = _dot(stay.astype(BF16), tri)
            run = run_ref[g]
            w = jnp.exp(lsig + after + run)
            if masked:
                w = jnp.where(valid, w, 0.0)
            acc_ref[g] += _dot(w.astype(BF16), v_ref[pl.ds(k0, tk), lanes])
            run_ref[g] = run + (after[:, 0:1] + stay[:, 0:1])

    j_diag = (qi * tq) // tk
    block(j_diag, True)

    def body(jj, carry):
        block(j_diag - 1 - jj, False)
        return carry

    lax.fori_loop(0, j_diag, body, 0)
    for g in range(ngrp):
        acc = acc_ref[g]
        out = jnp.zeros((tq, HEAD_LANES), F32)
        for h in range(nh):
            out = jnp.where(lane_head == h, acc[h * tq:(h + 1) * tq], out)
        o_ref[:, g * HEAD_LANES:(g + 1) * HEAD_LANES] = out.astype(BF16)


def _strict_upper(tk):
    idx = jnp.arange(tk)
    return (idx[:, None] > idx[None, :]).astype(BF16)


def _attn_prompt(q, kb, vb, sb_bias, batch, t_len, tq=128, tk=256):
    assert tq & (tq - 1) == 0
    n = q.shape[0]
    nq = t_len // tq
    ngrp = SB_HEADS // HEADS_PER_STEP
    return pl.pallas_call(
        functools.partial(_attn_prompt_kernel, tq=tq, tk=tk),
        out_shape=jax.ShapeDtypeStruct((n, SB_WIDTH), BF16),
        grid=(batch, nq),
        in_specs=[pl.BlockSpec(memory_space=pltpu.SMEM),
                  pl.BlockSpec((tq, SB_WIDTH), lambda b, i: (b * nq + i, 0)),
                  pl.BlockSpec((t_len, SB_WIDTH), lambda b, i: (b, 0)),
                  pl.BlockSpec((t_len, SB_WIDTH), lambda b, i: (b, 0)),
                  pl.BlockSpec((tk, tk), lambda b, i: (0, 0))],
        out_specs=pl.BlockSpec((tq, SB_WIDTH), lambda b, i: (b * nq + i, 0)),
        scratch_shapes=[pltpu.VMEM((ngrp, HEADS_PER_STEP * tq, HEAD_LANES), F32),
                        pltpu.VMEM((ngrp, HEADS_PER_STEP * tq, 1), F32)],
        compiler_params=_cparams(("parallel", "arbitrary")),
        name="attn_prompt",
    )(sb_bias.astype(F32), q, kb, vb, _strict_upper(tk))


def _attn_sample_kernel(pt_ref, q_ref, bias_ref, tri_ref, later_ref, *refs, n_pages):
    del pt_ref
    k_refs = refs[:n_pages]
    v_refs = refs[n_pages:2 * n_pages]
    o_ref = refs[2 * n_pages]
    hd = (SB_HEADS, SB_HEAD_DIM, PAGE_SIZE)
    q = q_ref[0].astype(F32)
    q_col = jnp.broadcast_to(q, (PAGE_SIZE, SB_WIDTH)).T
    q3 = q_col.reshape(hd)
    z = jnp.concatenate([jnp.sum(k_refs[p][0].reshape(hd) * q3, axis=1) for p in range(n_pages)], axis=0)
    lsig, stay = _log_sigmoid_pair(z + bias_ref[...])
    after = _dot(stay.astype(BF16), tri_ref[...])
    total = jnp.broadcast_to(after[:, 0:1] + stay[:, 0:1], stay.shape)
    t_hi, t_lo = _split_bf16(total)
    later = later_ref[...]
    run = _dot(later, t_hi) + _dot(later, t_lo)
    w = jnp.exp(lsig + after + run)
    acc = [jnp.zeros((SB_HEAD_DIM, PAGE_SIZE), F32) for _ in range(SB_HEADS)]
    for p in range(n_pages):
        for h in range(SB_HEADS):
            r = p * SB_HEADS + h
            acc[h] = acc[h] + v_refs[p][0, h * SB_HEAD_DIM:(h + 1) * SB_HEAD_DIM, :] * w[r:r + 1, :]
    a_hi, a_lo = _split_bf16(jnp.concatenate(acc, axis=0))
    ones = jnp.ones((8, PAGE_SIZE), BF16)
    o_ref[0] = (_dot_t(ones, a_hi) + _dot_t(ones, a_lo))[0:1].astype(BF16)


def _attn_sample(q, cache_k, cache_v, page_table, sb_bias):
    n, n_pages = page_table.shape
    n_pool = cache_k.shape[0]
    ck = cache_k.transpose(0, 2, 3, 1).reshape(n_pool, SB_WIDTH, PAGE_SIZE)
    cv = cache_v.transpose(0, 2, 3, 1).reshape(n_pool, SB_WIDTH, PAGE_SIZE)
    page_spec = lambda p: pl.BlockSpec((1, SB_WIDTH, PAGE_SIZE), lambda i, pt, p=p: (pt[i * n_pages + p], 0, 0))
    tok_spec = pl.BlockSpec((1, 1, SB_WIDTH), lambda i, pt: (i, 0, 0))
    rows = n_pages * SB_HEADS
    const = lambda shape: pl.BlockSpec(shape, lambda i, pt: (0, 0))
    r = jnp.arange(rows)
    later = ((r[:, None] % SB_HEADS == r[None, :] % SB_HEADS)
             & (r[None, :] // SB_HEADS > r[:, None] // SB_HEADS)).astype(BF16)
    out = pl.pallas_call(
        functools.partial(_attn_sample_kernel, n_pages=n_pages),
        out_shape=jax.ShapeDtypeStruct((n, 1, SB_WIDTH), BF16),
        grid_spec=pltpu.PrefetchScalarGridSpec(
            num_scalar_prefetch=1,
            grid=(n,),
            in_specs=[tok_spec, const((rows, 1)), const((PAGE_SIZE, PAGE_SIZE)), const((rows, rows))]
                     + [page_spec(p) for p in range(n_pages)] * 2,
            out_specs=tok_spec),
        compiler_params=_cparams(("parallel",)),
        name="attn_sample",
    )(page_table.reshape(-1).astype(jnp.int32), q.reshape(n, 1, SB_WIDTH),
      jnp.tile(sb_bias.astype(F32), n_pages).reshape(rows, 1), _strict_upper(PAGE_SIZE), later,
      *([ck] * n_pages), *([cv] * n_pages))
    return out.reshape(n, SB_WIDTH)


def _first_index_of_max(vals, row_id, n_rows):
    m = jnp.max(vals, axis=0, keepdims=True)
    first = jnp.min(jnp.where(vals == m, row_id, n_rows), axis=0, keepdims=True)
    return m, first


def _route(logits_t, rbias):
    tm = logits_t.shape[1]
    scores = jax.nn.sigmoid(logits_t)
    biased = scores + rbias
    row8 = lax.broadcasted_iota(jnp.int32, (GROUP_SIZE, tm), 0)
    neg = jnp.float32(-jnp.inf)
    grp_score = []
    for g in range(N_GROUPS):
        blk = biased[g * GROUP_SIZE:(g + 1) * GROUP_SIZE]
        m1, first = _first_index_of_max(blk, row8, GROUP_SIZE)
        m2 = jnp.max(jnp.where(row8 == first, neg, blk), axis=0, keepdims=True)
        grp_score.append(m1 + m2)
    masked = []
    for g in range(N_GROUPS):
        beaten_by = jnp.zeros((1, tm), jnp.int32)
        for o in range(N_GROUPS):
            if o == g:
                continue
            wins = (grp_score[o] > grp_score[g]) if o > g else (grp_score[o] >= grp_score[g])
            beaten_by = beaten_by + wins.astype(jnp.int32)
        keep = beaten_by < TOPK_GROUPS
        masked.append(jnp.where(keep, biased[g * GROUP_SIZE:(g + 1) * GROUP_SIZE], neg))
    cur = jnp.concatenate(masked, axis=0)
    row = lax.broadcasted_iota(jnp.int32, (N_EXPERTS, tm), 0)
    chosen = jnp.zeros((N_EXPERTS, tm), jnp.bool_)
    for _ in range(TOP_K):
        _, first = _first_index_of_max(cur, row, N_EXPERTS)
        pick = row == first
        chosen = jnp.logical_or(chosen, pick)
        cur = jnp.where(pick, neg, cur)
    sel = jnp.where(chosen, scores, 0.0)
    return sel / jnp.sum(sel, axis=0, keepdims=True) * ROUTED_SCALE


def _post_kernel(x_ref, g_ref, o_ref, sga_ref, sgb_ref, gate1_ref, shift2_ref, scale2_ref, g2_ref,
                 wga_ref, wgb_ref, wsb_ref, wout_ref, wr_ref, rb_ref,
                 x1_ref, h2_ref, gates_ref):
    g = g_ref[...]
    branch_a = _dot(g, wga_ref[...]) * jax.nn.sigmoid(_dot(g, wgb_ref[...]))
    branch_b = _dot(o_ref[...], wsb_ref[...])
    merged = sga_ref[...].astype(F32) * branch_a + sgb_ref[...].astype(F32) * branch_b
    x1 = x_ref[...] + gate1_ref[...] * _dot(merged.astype(BF16), wout_ref[...])
    x1_ref[...] = x1
    r = lax.rsqrt(jnp.mean(x1 * x1, axis=-1, keepdims=True) + RMS_EPS)
    h2 = (x1 * r) * g2_ref[...]
    h2 = h2 * (1.0 + scale2_ref[...]) + shift2_ref[...]
    h2_ref[...] = h2.astype(BF16)
    h_hi, h_lo = _split_bf16(h2)
    w_hi, w_lo = _split_bf16(wr_ref[...])
    logits_t = _dot_t(w_hi, h_hi) + _dot_t(w_hi, h_lo) + _dot_t(w_lo, h_hi)
    gates_ref[...] = _route(logits_t, rb_ref[...])


def _post_mixer(x, g, o, sga, sgb, mod, per_token, tm, p):
    n = x.shape[0]
    tiles_per_mod = (n // mod.shape[0]) // tm if not per_token else 1
    row = lambda i: (i, 0)
    const = lambda i: (0, 0)
    blk = lambda w: pl.BlockSpec((tm, w), row)
    full = lambda a: pl.BlockSpec(a.shape, const)
    ms = lambda c: _mod_spec(per_token, tm, tiles_per_mod, c)
    weights = (p["w_glu_a"], p["w_glu_b"], p["w_sb_out"], p["w_out"], p["w_router_t"], p["router_bias"])
    return pl.pallas_call(
        _post_kernel,
        out_shape=(jax.ShapeDtypeStruct((n, D_MODEL), F32),
                   jax.ShapeDtypeStruct((n, D_MODEL), BF16),
                   jax.ShapeDtypeStruct((N_EXPERTS, n), F32)),
        grid=(n // tm,),
        in_specs=[blk(D_MODEL), blk(512), blk(512), blk(1024), blk(1024), ms(2), ms(3), ms(4),
                  pl.BlockSpec((1, D_MODEL), const)] + [full(w) for w in weights],
        out_specs=(blk(D_MODEL), blk(D_MODEL), pl.BlockSpec((N_EXPERTS, tm), lambda i: (0, i))),
        compiler_params=_cparams(("parallel",)),
        name="post_mixer",
    )(x, g, o, sga, sgb, mod, mod, mod, p["norm2_g"], *weights)


def _moe_kernel(h_ref, gates_ref, x1_ref, gate2_ref, wgs_ref, wus_ref, wds_ref,
                wg_ref, wu_ref, wd_ref, y_ref, acc_ref):
    e = pl.program_id(1)
    h = h_ref[...]

    @pl.when(e == 0)
    def _():
        act = _silu(_dot(h, wgs_ref[...])) * _dot(h, wus_ref[...])
        acc_ref[...] = _dot(act.astype(BF16), wds_ref[...])

    act = _silu(_dot(h, wg_ref[...])) * _dot(h, wu_ref[...])
    pick = (lax.broadcasted_iota(jnp.int32, (N_EXPERTS, 128), 0) == e).astype(BF16)
    g_hi, g_lo = _split_bf16(gates_ref[...])
    gcol = _dot(g_hi, pick) + _dot(g_lo, pick)
    act = act * jnp.concatenate([gcol, gcol], axis=1)
    acc_ref[...] += _dot(act.astype(BF16), wd_ref[...])

    @pl.when(e == pl.num_programs(1) - 1)
    def _():
        y_ref[...] = x1_ref[...] + gate2_ref[...] * acc_ref[...]


def _moe(h2, gates, x1, mod, per_token, tm, p):
    n = h2.shape[0]
    tiles_per_mod = (n // mod.shape[0]) // tm if not per_token else 1
    row = lambda i, e: (i, 0)
    const = lambda i, e: (0, 0)
    exp = lambda i, e: (e, 0, 0)
    if per_token:
        gate2_spec = pl.BlockSpec((tm, D_MODEL), lambda i, e: (i, 5))
    else:
        gate2_spec = pl.BlockSpec((None, 1, D_MODEL), lambda i, e: (i // tiles_per_mod, 0, 5))
    return pl.pallas_call(
        _moe_kernel,
        out_shape=jax.ShapeDtypeStruct((n, D_MODEL), F32),
        grid=(n // tm, N_EXPERTS),
        in_specs=[pl.BlockSpec((tm, D_MODEL), row),
                  pl.BlockSpec((tm, N_EXPERTS), row),
                  pl.BlockSpec((tm, D_MODEL), row),
                  gate2_spec,
                  pl.BlockSpec((D_MODEL, EXPERT_HIDDEN), const),
                  pl.BlockSpec((D_MODEL, EXPERT_HIDDEN), const),
                  pl.BlockSpec((EXPERT_HIDDEN, D_MODEL), const),
                  pl.BlockSpec((None, D_MODEL, EXPERT_HIDDEN), exp),
                  pl.BlockSpec((None, D_MODEL, EXPERT_HIDDEN), exp),
                  pl.BlockSpec((None, EXPERT_HIDDEN, D_MODEL), exp)],
        out_specs=pl.BlockSpec((tm, D_MODEL), row),
        scratch_shapes=[pltpu.VMEM((tm, D_MODEL), F32)],
        compiler_params=_cparams(("parallel", "arbitrary")),
        name="moe",
    )(h2, gates, x1, mod, p["w_gate_s"], p["w_up_s"], p["w_down_s"], p["w_gate_e"], p["w_up_e"], p["w_down_e"])


def _layer(x_prompt, x_sample, cache_k, cache_v, s0_re, s0_im, page_table, c_prompt, c_sample, p):
    batch, t_len, d = x_prompt.shape
    n_s = x_sample.shape[0]
    n_p = batch * t_len

    c_all = jnp.concatenate([c_prompt, c_sample], axis=0)
    pad = (-c_all.shape[0]) % 8
    c_all = jnp.pad(c_all, ((0, pad), (0, 0)))
    mod = _ada_mod(c_all, p["w_ada"], p["b_ada"])
    mod_p = mod[:batch].reshape(batch, 1, 6 * d)
    mod_s = mod[batch:batch + n_s]

    a_re, a_im, bb_re, bb_im = _ssm_prep(p["lam_re"], p["lam_im"], p["log_dt"], p["b_re"], p["b_im"])
    ssm = (a_re, a_im, bb_re, bb_im, _blockdiag_out(p["c_re"]), _blockdiag_out(-p["c_im"]),
           p["d_skip"].astype(F32).reshape(1, SSM_WIDTH))

    xp = x_prompt.reshape(n_p, d)
    xs = x_sample.reshape(n_s, d)
    proj_p = _in_proj(xp, mod_p, False, 512, p["norm1_g"], p["w_in"], p["q_norm_g"], p["k_norm_g"])
    proj_s = _in_proj(xs, mod_s, True, n_s, p["norm1_g"], p["w_in"], p["q_norm_g"], p["k_norm_g"])
    u_p, q_p, k_p, kb_p, v_p, vb_p, sga_p, sgb_p = proj_p
    u_s, q_s, k_s, _, v_s, _, sga_s, sgb_s = proj_s

    u_tb = u_p.reshape(batch, t_len, SSM_WIDTH).transpose(1, 0, 2).reshape(n_p, SSM_WIDTH)
    g_tb, st_re_p, st_im_p = _ssm_prompt(u_tb, batch, ssm)
    g_p = g_tb.reshape(t_len, batch, SSM_WIDTH).transpose(1, 0, 2).reshape(n_p, SSM_WIDTH)
    g_s, st_re_s, st_im_s = _ssm_sample(u_s, s0_re.reshape(n_s, SSM_FLAT), s0_im.reshape(n_s, SSM_FLAT), ssm)

    o_p = _attn_prompt(q_p, kb_p, vb_p, p["sb_bias"], batch, t_len)
    o_s = _attn_sample(q_s, cache_k, cache_v, page_table, p["sb_bias"])

    x1_p, h2_p, gates_p = _post_mixer(xp, g_p, o_p, sga_p, sgb_p, mod_p, False, 512, p)
    x1_s, h2_s, gates_s = _post_mixer(xs, g_s, o_s, sga_s, sgb_s, mod_s, True, n_s, p)

    y_p = _moe(h2_p, gates_p.T, x1_p, mod_p, False, 1024, p)
    y_s = _moe(h2_s, gates_s.T, x1_s, mod_s, True, n_s, p)

    hd = (SB_HEADS, SB_HEAD_DIM)
    gs = (SSM_GROUPS, SSM_STATE)
    return (y_p.reshape(batch, t_len, d), y_s.reshape(n_s, 1, d),
            k_p.reshape(batch, t_len, *hd), v_p.reshape(batch, t_len, *hd),
            st_re_p.reshape(batch, *gs), st_im_p.reshape(batch, *gs),
            k_s.reshape(n_s, 1, *hd), v_s.reshape(n_s, 1, *hd),
            st_re_s.reshape(n_s, *gs), st_im_s.reshape(n_s, *gs))


def kernel(x_prompt, x_sample, cache_k, cache_v, state_ssm_re, state_ssm_im, page_table, c_prompt, c_sample,
           w_ada, b_ada, norm1_g, norm2_g, w_in, q_norm_g, k_norm_g, sb_bias, ssm_lambda_re, ssm_lambda_im,
           ssm_log_dt, ssm_b_re, ssm_b_im, ssm_c_re, ssm_c_im, ssm_d, w_glu_a, w_glu_b, w_sb_out, w_out,
           w_router, router_bias, w_gate_e, w_up_e, w_down_e, w_gate_s, w_up_s, w_down_s):
    depth = w_in.shape[0]
    assert depth == 1, "single-layer step"
    l = 0
    p = dict(
        w_ada=w_ada[l], b_ada=b_ada[l],
        norm1_g=norm1_g[l], norm2_g=norm2_g[l].reshape(1, D_MODEL),
        w_in=w_in[l].astype(BF16), q_norm_g=q_norm_g[l], k_norm_g=k_norm_g[l], sb_bias=sb_bias[l],
        lam_re=ssm_lambda_re[l], lam_im=ssm_lambda_im[l], log_dt=ssm_log_dt[l],
        b_re=ssm_b_re[l], b_im=ssm_b_im[l], c_re=ssm_c_re[l], c_im=ssm_c_im[l], d_skip=ssm_d[l],
        w_glu_a=w_glu_a[l].astype(BF16), w_glu_b=w_glu_b[l].astype(BF16),
        w_sb_out=w_sb_out[l].astype(BF16), w_out=w_out[l].astype(BF16),
        w_router_t=w_router[l].T, router_bias=router_bias[l].reshape(N_EXPERTS, 1),
        w_gate_e=w_gate_e[l].astype(BF16), w_up_e=w_up_e[l].astype(BF16), w_down_e=w_down_e[l].astype(BF16),
        w_gate_s=w_gate_s[l].astype(BF16), w_up_s=w_up_s[l].astype(BF16), w_down_s=w_down_s[l].astype(BF16),
    )
    outs = _layer(x_prompt, x_sample, cache_k[l], cache_v[l], state_ssm_re[l], state_ssm_im[l],
                  page_table, c_prompt, c_sample, p)
    y_p, y_s = outs[0], outs[1]
    return (y_p, y_s) + tuple(o[None] for o in outs[2:])
```

```python
import functools
import math

import jax
import jax.numpy as jnp
from jax import lax
from jax.experimental import pallas as pl
from jax.experimental.pallas import tpu as pltpu

F32 = jnp.float32
BF16 = jnp.bfloat16

D_MODEL = 1024
SSM_WIDTH = 512
SSM_GROUP = 16
SSM_GROUPS = 32
SSM_STATE = 64
SSM_FLAT = SSM_GROUPS * SSM_STATE
EIG_CLIP = -1e-4
SB_HEADS = 8
SB_HEAD_DIM = 64
SB_WIDTH = 512
PAGE_SIZE = 128
N_EXPERTS = 64
TOP_K = 8
N_GROUPS = 8
TOPK_GROUPS = 4
GROUP_SIZE = N_EXPERTS // N_GROUPS
EXPERT_HIDDEN = 256
ROUTED_SCALE = 2.5
RMS_EPS = 1e-6

LOG2E = math.log2(math.e)
LOGIT_SCALE_LOG2 = SB_HEAD_DIM ** -0.5 * LOG2E
HEADS_PER_STEP = 4
HEAD_LANES = HEADS_PER_STEP * SB_HEAD_DIM
VMEM_LIMIT = 48 * 1024 * 1024


def _cparams(sem):
    return pltpu.CompilerParams(dimension_semantics=sem, vmem_limit_bytes=VMEM_LIMIT)


def _dot(a, b):
    return jnp.dot(a, b, preferred_element_type=F32)


def _dot_t(a, b):
    return lax.dot_general(a, b, (((1,), (1,)), ((), ())), preferred_element_type=F32)


def _split_bf16(x):
    hi = x.astype(BF16)
    lo = (x - hi.astype(F32)).astype(BF16)
    return hi, lo


def _silu(x):
    return x * jax.nn.sigmoid(x)


def _gelu_tanh(x):
    c = math.sqrt(2.0 / math.pi)
    return 0.5 * x * (1.0 + jnp.tanh(c * (x + 0.044715 * (x * x * x))))


def _ada_kernel(c_ref, w_ref, b_ref, o_ref):
    s = _silu(c_ref[...])
    s_hi, s_lo = _split_bf16(s)
    w_hi, w_lo = _split_bf16(w_ref[...])
    o_ref[...] = _dot(s_hi, w_hi) + _dot(s_hi, w_lo) + _dot(s_lo, w_hi) + b_ref[...]


def _ada_mod(c, w_ada, b_ada):
    rows, d = c.shape
    cols = w_ada.shape[1]
    tn = 1024
    return pl.pallas_call(
        _ada_kernel,
        out_shape=jax.ShapeDtypeStruct((rows, cols), F32),
        grid=(cols // tn,),
        in_specs=[pl.BlockSpec((rows, d), lambda j: (0, 0)),
                  pl.BlockSpec((d, tn), lambda j: (0, j)),
                  pl.BlockSpec((1, tn), lambda j: (0, j))],
        out_specs=pl.BlockSpec((rows, tn), lambda j: (0, j)),
        compiler_params=_cparams(("parallel",)),
        name="ada_mod",
    )(c, w_ada, b_ada.reshape(1, cols))


def _ssm_prep_kernel(lre_ref, lim_ref, ldt_ref, bre_ref, bim_ref,
                     are_ref, aim_ref, bbre_ref, bbim_ref):
    dt = jnp.exp(ldt_ref[...])
    lre = jnp.minimum(lre_ref[...], EIG_CLIP)
    lim = lim_ref[...]
    mag = jnp.exp(lre * dt)
    a_re = mag * jnp.cos(lim * dt)
    a_im = mag * jnp.sin(lim * dt)
    den = lre * lre + lim * lim
    f_re = ((a_re - 1.0) * lre + a_im * lim) / den
    f_im = (a_im * lre - (a_re - 1.0) * lim) / den
    br = bre_ref[...]
    bi = bim_ref[...]
    are_ref[...] = a_re
    aim_ref[...] = a_im
    bbre_ref[...] = f_re * br - f_im * bi
    bbim_ref[...] = f_re * bi + f_im * br


def _ssm_prep(lam_re, lam_im, log_dt, b_re, b_im):
    rep = lambda a: jnp.repeat(a.astype(F32), SSM_GROUP, axis=0)
    lre, lim = rep(lam_re), rep(lam_im)
    ldt = rep(jnp.broadcast_to(log_dt.astype(F32)[:, None], (SSM_GROUPS, SSM_STATE)))
    tr = lambda b: b.astype(F32).transpose(0, 2, 1).reshape(SSM_WIDTH, SSM_STATE)
    shp = jax.ShapeDtypeStruct((SSM_WIDTH, SSM_STATE), F32)
    a_re, a_im, bb_re, bb_im = pl.pallas_call(
        _ssm_prep_kernel, out_shape=(shp, shp, shp, shp), name="ssm_prep",
    )(lre, lim, ldt, tr(b_re), tr(b_im))
    eye = jnp.eye(SSM_GROUPS, dtype=F32)

    def blockdiag_in(bb):
        bb4 = bb.reshape(SSM_GROUPS, SSM_GROUP, 1, SSM_STATE) * eye.reshape(SSM_GROUPS, 1, SSM_GROUPS, 1)
        return bb4.reshape(SSM_WIDTH, SSM_FLAT).astype(BF16)

    a_re = a_re[::SSM_GROUP].reshape(1, SSM_FLAT)
    a_im = a_im[::SSM_GROUP].reshape(1, SSM_FLAT)
    return a_re, a_im, blockdiag_in(bb_re), blockdiag_in(bb_im)


def _blockdiag_out(c):
    eye = jnp.eye(SSM_GROUPS, dtype=F32)
    c4 = c.astype(F32).transpose(0, 2, 1).reshape(SSM_GROUPS, SSM_STATE, 1, SSM_GROUP)
    return (c4 * eye.reshape(SSM_GROUPS, 1, SSM_GROUPS, 1)).reshape(SSM_FLAT, SSM_WIDTH).astype(BF16)


def _inproj_kernel(x_ref, shift_ref, scale_ref, g1_ref, w_ref, qg_ref, kg_ref, hm_ref,
                   u_ref, q_ref, k_ref, kb_ref, v_ref, vb_ref, sga_ref, sgb_ref):
    x = x_ref[...]
    r = lax.rsqrt(jnp.mean(x * x, axis=-1, keepdims=True) + RMS_EPS)
    h = (x * r) * g1_ref[...]
    h = h * (1.0 + scale_ref[...]) + shift_ref[...]
    hb = h.astype(BF16)
    hm = hm_ref[...]

    def head_norm(t, g):
        ms = _dot((t * t).astype(BF16), hm)
        return (t * lax.rsqrt(ms + RMS_EPS)) * g

    u_ref[...] = _dot(hb, w_ref[:, 0:512])
    q = head_norm(_dot(hb, w_ref[:, 512:1024]), qg_ref[...])
    q_ref[...] = (q * LOGIT_SCALE_LOG2).astype(BF16)
    k = head_norm(_dot(hb, w_ref[:, 1024:1536]), kg_ref[...])
    k_ref[...] = k.T
    kb_ref[...] = k.astype(BF16)
    v = _dot(hb, w_ref[:, 1536:2048])
    v_ref[...] = v.T
    vb_ref[...] = v.astype(BF16)
    sga_ref[...] = jax.nn.sigmoid(_dot(hb, w_ref[:, 2048:3072])).astype(BF16)
    sgb_ref[...] = jax.nn.sigmoid(_dot(hb, w_ref[:, 3072:4096])).astype(BF16)


def _head_mean_matrix():
    idx = jnp.arange(SB_WIDTH) // SB_HEAD_DIM
    return ((idx[:, None] == idx[None, :]).astype(F32) / SB_HEAD_DIM).astype(BF16)


def _mod_spec(per_token, tm, rows_per_mod, chunk):
    if per_token:
        return pl.BlockSpec((tm, D_MODEL), lambda i: (i, chunk))
    return pl.BlockSpec((None, 1, D_MODEL), lambda i: (i // rows_per_mod, 0, chunk))


def _in_proj(x, mod, per_token, tm, seq_len, norm1_g, w_in_b, q_g, k_g):
    n = x.shape[0]
    tiles_per_mod = (n // mod.shape[0]) // tm if not per_token else 1
    tiles_per_seq = seq_len // tm
    row = lambda i: (i, 0)
    const = lambda i: (0, 0)
    f32o = lambda w: jax.ShapeDtypeStruct((n, w), F32)
    b16o = lambda w: jax.ShapeDtypeStruct((n, w), BF16)
    kvo = jax.ShapeDtypeStruct((n // seq_len, SB_WIDTH, seq_len), F32)
    kv_blk = pl.BlockSpec((None, SB_WIDTH, tm), lambda i: (i // tiles_per_seq, 0, i % tiles_per_seq))
    blk = lambda w: pl.BlockSpec((tm, w), row)
    return pl.pallas_call(
        _inproj_kernel,
        out_shape=(f32o(512), b16o(512), kvo, b16o(512), kvo, b16o(512), b16o(1024), b16o(1024)),
        grid=(n // tm,),
        in_specs=[blk(D_MODEL),
                  _mod_spec(per_token, tm, tiles_per_mod, 0),
                  _mod_spec(per_token, tm, tiles_per_mod, 1),
                  pl.BlockSpec((1, D_MODEL), const),
                  pl.BlockSpec(w_in_b.shape, const),
                  pl.BlockSpec((1, SB_WIDTH), const),
                  pl.BlockSpec((1, SB_WIDTH), const),
                  pl.BlockSpec((SB_WIDTH, SB_WIDTH), const)],
        out_specs=(blk(512), blk(512), kv_blk, blk(512), kv_blk, blk(512), blk(1024), blk(1024)),
        compiler_params=_cparams(("parallel",)),
        name="in_proj",
    )(x, mod, mod, norm1_g.reshape(1, D_MODEL), w_in_b,
      jnp.tile(q_g, SB_HEADS).reshape(1, SB_WIDTH), jnp.tile(k_g, SB_HEADS).reshape(1, SB_WIDTH),
      _head_mean_matrix())


def _ssm_drive(ub, bb_ref):
    half_c, half_s = SSM_WIDTH // 2, SSM_FLAT // 2
    lo = _dot(ub[:, :half_c], bb_ref[:half_c, :half_s])
    hi = _dot(ub[:, half_c:], bb_ref[half_c:, half_s:])
    return lo, hi


def _ssm_readout(xre, xim, ccre_ref, ccim_ref, d_ref, u):
    half_c, half_s = SSM_WIDTH // 2, SSM_FLAT // 2
    xr = xre.astype(BF16)
    xi = xim.astype(BF16)
    y_lo = _dot(xr[:, :half_s], ccre_ref[:half_s, :half_c]) + _dot(xi[:, :half_s], ccim_ref[:half_s, :half_c])
    y_hi = _dot(xr[:, half_s:], ccre_ref[half_s:, half_c:]) + _dot(xi[:, half_s:], ccim_ref[half_s:, half_c:])
    y = jnp.concatenate([y_lo, y_hi], axis=1) + d_ref[...] * u
    return _gelu_tanh(y)


def _ssm_prompt_kernel(u_ref, bbre_ref, bbim_ref, are_ref, aim_ref, ccre_ref, ccim_ref, d_ref,
                       g_ref, sre_ref, sim_ref, ubuf, gbuf, xre, xim, cre, cim, *, lane_chunk):
    batch, tt, _ = u_ref.shape
    rows = tt * batch
    half_s = SSM_FLAT // 2

    @pl.when(pl.program_id(0) == 0)
    def _():
        cre[...] = jnp.zeros_like(cre)
        cim[...] = jnp.zeros_like(cim)

    n_lane_tiles = SSM_WIDTH // 128
    for b in range(batch):
        for c in range(n_lane_tiles):
            ubuf[c, pl.ds(b, tt, stride=batch), :] = u_ref[b, :, c * 128:(c + 1) * 128]
    u = jnp.concatenate([ubuf[c] for c in range(n_lane_tiles)], axis=1)
    ub = u.astype(BF16)
    lo, hi = _ssm_drive(ub, bbre_ref)
    xre[:, :half_s] = lo
    xre[:, half_s:] = hi
    lo, hi = _ssm_drive(ub, bbim_ref)
    xim[:, :half_s] = lo
    xim[:, half_s:] = hi

    first_step = lax.broadcasted_iota(jnp.int32, (8, lane_chunk), 0) < batch
    for c0 in range(0, SSM_FLAT, lane_chunk):
        cols = pl.ds(c0, lane_chunk)
        ar = jnp.broadcast_to(are_ref[:, cols], (8, lane_chunk))
        ai = jnp.broadcast_to(aim_ref[:, cols], (8, lane_chunk))

        def step(i, carry, cols=cols, ar=ar, ai=ai):
            c_r, c_i = carry
            r0 = pl.multiple_of(i * 8, 8)
            br = xre[pl.ds(r0, 8), cols]
            bi = xim[pl.ds(r0, 8), cols]
            p_r = pltpu.roll(c_r, batch, 0)
            p_i = pltpu.roll(c_i, batch, 0)
            y_r = ar * p_r - ai * p_i + br
            y_i = ar * p_i + ai * p_r + bi
            q_r = pltpu.roll(y_r, batch, 0)
            q_i = pltpu.roll(y_i, batch, 0)
            z_r = ar * q_r - ai * q_i + br
            z_i = ar * q_i + ai * q_r + bi
            xre[pl.ds(r0, 8), cols] = jnp.where(first_step, y_r, z_r)
            xim[pl.ds(r0, 8), cols] = jnp.where(first_step, y_i, z_i)
            return z_r, z_i

        c_r, c_i = lax.fori_loop(0, rows // 8, step, (cre[:, cols], cim[:, cols]))
        cre[:, cols] = c_r
        cim[:, cols] = c_i

    g = _ssm_readout(xre[...], xim[...], ccre_ref, ccim_ref, d_ref, u)
    for c in range(n_lane_tiles):
        gbuf[c] = g[:, c * 128:(c + 1) * 128]
    for b in range(batch):
        g_ref[b] = jnp.concatenate([gbuf[c, pl.ds(b, tt, stride=batch), :] for c in range(n_lane_tiles)],
                                   axis=1).astype(BF16)
    sre_ref[...] = cre[...]
    sim_ref[...] = cim[...]


def _ssm_prompt(u, ssm, tt=128, lane_chunk=512):
    a_re, a_im, bb_re, bb_im, cc_re, cc_im, d_skip = ssm
    batch, t_len, _ = u.shape
    assert 2 * batch == 8, "one sublane tile must hold exactly two time steps"
    rows = tt * batch
    const = lambda j: (0, 0)
    full = lambda a: pl.BlockSpec(a.shape, const)
    seq_blk = pl.BlockSpec((batch, tt, SSM_WIDTH), lambda j: (0, j, 0))
    g, s_re, s_im = pl.pallas_call(
        functools.partial(_ssm_prompt_kernel, lane_chunk=lane_chunk),
        out_shape=(jax.ShapeDtypeStruct((batch, t_len, SSM_WIDTH), BF16),
                   jax.ShapeDtypeStruct((8, SSM_FLAT), F32),
                   jax.ShapeDtypeStruct((8, SSM_FLAT), F32)),
        grid=(t_len // tt,),
        in_specs=[seq_blk,
                  full(bb_re), full(bb_im), full(a_re), full(a_im), full(cc_re), full(cc_im), full(d_skip)],
        out_specs=(seq_blk, pl.BlockSpec((8, SSM_FLAT), const), pl.BlockSpec((8, SSM_FLAT), const)),
        scratch_shapes=[pltpu.VMEM((SSM_WIDTH // 128, rows, 128), F32),
                        pltpu.VMEM((SSM_WIDTH // 128, rows, 128), F32),
                        pltpu.VMEM((rows, SSM_FLAT), F32), pltpu.VMEM((rows, SSM_FLAT), F32),
                        pltpu.VMEM((8, SSM_FLAT), F32), pltpu.VMEM((8, SSM_FLAT), F32)],
        compiler_params=_cparams(("arbitrary",)),
        name="ssm_prompt",
    )(u, bb_re, bb_im, a_re, a_im, cc_re, cc_im, d_skip)
    return g, s_re[batch:], s_im[batch:]


def _ssm_sample_kernel(u_ref, s0re_ref, s0im_ref, bbre_ref, bbim_ref, are_ref, aim_ref,
                       ccre_ref, ccim_ref, d_ref, g_ref, sre_ref, sim_ref):
    u = u_ref[...]
    ub = u.astype(BF16)
    ar, ai = are_ref[...], aim_ref[...]
    s_r, s_i = s0re_ref[...], s0im_ref[...]
    lo, hi = _ssm_drive(ub, bbre_ref)
    x_r = ar * s_r - ai * s_i + jnp.concatenate([lo, hi], axis=1)
    lo, hi = _ssm_drive(ub, bbim_ref)
    x_i = ar * s_i + ai * s_r + jnp.concatenate([lo, hi], axis=1)
    sre_ref[...] = x_r
    sim_ref[...] = x_i
    g_ref[...] = _ssm_readout(x_r, x_i, ccre_ref, ccim_ref, d_ref, u).astype(BF16)


def _ssm_sample(u, s_re, s_im, ssm):
    a_re, a_im, bb_re, bb_im, cc_re, cc_im, d_skip = ssm
    n = u.shape[0]
    return pl.pallas_call(
        _ssm_sample_kernel,
        out_shape=(jax.ShapeDtypeStruct((n, SSM_WIDTH), BF16),
                   jax.ShapeDtypeStruct((n, SSM_FLAT), F32),
                   jax.ShapeDtypeStruct((n, SSM_FLAT), F32)),
        compiler_params=pltpu.CompilerParams(vmem_limit_bytes=VMEM_LIMIT),
        name="ssm_sample",
    )(u, s_re, s_im, bb_re, bb_im, a_re, a_im, cc_re, cc_im, d_skip)


def _log_sigmoid_pair(z2):
    sign = jnp.uint32(0x80000000)
    neg_abs = lax.bitcast_convert_type(lax.bitcast_convert_type(z2, jnp.uint32) | sign, F32)
    soft = jnp.log2(1.0 + jnp.exp2(neg_abs))
    lsig = jnp.minimum(z2, 0.0) - soft
    return lsig, lsig - z2


def _attn_prompt_kernel(bias_ref, q_ref, k_ref, v_ref, tri_ref, o_ref, acc_ref, run_ref, *, tq, tk):
    qi = pl.program_id(1)
    nh = HEADS_PER_STEP
    ngrp = SB_HEADS // nh
    lane_head = lax.broadcasted_iota(jnp.int32, (1, HEAD_LANES), 1) // SB_HEAD_DIM
    head_mask = [(lane_head == h).astype(F32).astype(BF16) for h in range(nh)]
    qs, bias = [], []
    for g in range(ngrp):
        q = q_ref[:, g * HEAD_LANES:(g + 1) * HEAD_LANES]
        qs.append(jnp.concatenate([q * head_mask[h] for h in range(nh)], axis=0))
        bias.append([bias_ref[g * nh + h] for h in range(nh)])
    tri = tri_ref[...]
    acc_ref[...] = jnp.zeros_like(acc_ref)
    run_ref[...] = jnp.zeros_like(run_ref)

    def block(js, masked):
        if masked:
            k0 = pl.multiple_of(js[0] * tk, tk)
            kpos = k0 + lax.broadcasted_iota(jnp.int32, (nh * tq, tk), 1)
            qpos = qi * tq + (lax.broadcasted_iota(jnp.int32, (nh * tq, tk), 0) & (tq - 1))
            valid = kpos < qpos
        for g in range(ngrp):
            lanes = slice(g * HEAD_LANES, (g + 1) * HEAD_LANES)
            run = run_ref[g]
            pv = None
            for j in js:
                k0 = pl.multiple_of(j * tk, tk)
                zs = _dot_t(qs[g], k_ref[pl.ds(k0, tk), lanes])
                lsig_parts, stay_parts = [], []
                for h in range(nh):
                    lsig, stay = _log_sigmoid_pair(zs[h * tq:(h + 1) * tq] + bias[g][h])
                    lsig_parts.append(lsig)
                    stay_parts.append(stay)
                lsig = jnp.concatenate(lsig_parts, axis=0)
                stay = jnp.concatenate(stay_parts, axis=0)
                if masked:
                    stay = jnp.where(valid, stay, 0.0)
                after = _dot(stay.astype(BF16), tri)
                w = jnp.exp2(lsig + after + run)
                if masked:
                    w = jnp.where(valid, w, 0.0)
                part = _dot(w.astype(BF16), v_ref[pl.ds(k0, tk), lanes])
                pv = part if pv is None else pv + part
                run = run + (after[:, 0:1] + stay[:, 0:1])
            acc_ref[g] += pv
            run_ref[g] = run

    j_diag = (qi * tq) // tk
    block((j_diag,), True)

    @pl.when(j_diag % 2 == 1)
    def _():
        block((j_diag - 1,), False)

    def body(jj, carry):
        j = 2 * (j_diag // 2 - jj) - 1
        block((j, j - 1), False)
        return carry

    lax.fori_loop(0, j_diag // 2, body, 0)
    for g in range(ngrp):
        acc = acc_ref[g]
        out = jnp.zeros((tq, HEAD_LANES), F32)
        for h in range(nh):
            out = jnp.where(lane_head == h, acc[h * tq:(h + 1) * tq], out)
        o_ref[:, g * HEAD_LANES:(g + 1) * HEAD_LANES] = out.astype(BF16)


def _strict_upper(tk):
    idx = jnp.arange(tk)
    return (idx[:, None] > idx[None, :]).astype(BF16)


def _attn_prompt(q, kb, vb, sb_bias, batch, t_len, tq=128, tk=256):
    assert tq & (tq - 1) == 0
    n = q.shape[0]
    nq = t_len // tq
    ngrp = SB_HEADS // HEADS_PER_STEP
    return pl.pallas_call(
        functools.partial(_attn_prompt_kernel, tq=tq, tk=tk),
        out_shape=jax.ShapeDtypeStruct((n, SB_WIDTH), BF16),
        grid=(batch, nq),
        in_specs=[pl.BlockSpec(memory_space=pltpu.SMEM),
                  pl.BlockSpec((tq, SB_WIDTH), lambda b, i: (b * nq + i, 0)),
                  pl.BlockSpec((t_len, SB_WIDTH), lambda b, i: (b, 0)),
                  pl.BlockSpec((t_len, SB_WIDTH), lambda b, i: (b, 0)),
                  pl.BlockSpec((tk, tk), lambda b, i: (0, 0))],
        out_specs=pl.BlockSpec((tq, SB_WIDTH), lambda b, i: (b * nq + i, 0)),
        scratch_shapes=[pltpu.VMEM((ngrp, HEADS_PER_STEP * tq, HEAD_LANES), F32),
                        pltpu.VMEM((ngrp, HEADS_PER_STEP * tq, 1), F32)],
        compiler_params=_cparams(("parallel", "arbitrary")),
        name="attn_prompt",
    )(sb_bias.astype(F32) * LOG2E, q, kb, vb, _strict_upper(tk))


def _attn_sample_kernel(pt_ref, q_ref, bias_ref, tri_ref, later_ref, *refs, n_pages):
    del pt_ref
    k_refs = refs[:n_pages]
    v_refs = refs[n_pages:2 * n_pages]
    o_ref = refs[2 * n_pages]
    hd = (SB_HEADS, SB_HEAD_DIM, PAGE_SIZE)
    q = q_ref[0].astype(F32)
    q_col = jnp.broadcast_to(q, (PAGE_SIZE, SB_WIDTH)).T
    q3 = q_col.reshape(hd)
    z = jnp.concatenate([jnp.sum(k_refs[p][0].reshape(hd) * q3, axis=1) for p in range(n_pages)], axis=0)
    lsig, stay = _log_sigmoid_pair(z + bias_ref[...])
    after = _dot(stay.astype(BF16), tri_ref[...])
    total = jnp.broadcast_to(after[:, 0:1] + stay[:, 0:1], stay.shape)
    t_hi, t_lo = _split_bf16(total)
    later = later_ref[...]
    run = _dot(later, t_hi) + _dot(later, t_lo)
    w = jnp.exp2(lsig + after + run)
    acc = [jnp.zeros((SB_HEAD_DIM, PAGE_SIZE), F32) for _ in range(SB_HEADS)]
    for p in range(n_pages):
        for h in range(SB_HEADS):
            r = p * SB_HEADS + h
            acc[h] = acc[h] + v_refs[p][0, h * SB_HEAD_DIM:(h + 1) * SB_HEAD_DIM, :] * w[r:r + 1, :]
    a_hi, a_lo = _split_bf16(jnp.concatenate(acc, axis=0))
    ones = jnp.ones((8, PAGE_SIZE), BF16)
    o_ref[0] = (_dot_t(ones, a_hi) + _dot_t(ones, a_lo))[0:1].astype(BF16)


def _attn_sample(q, cache_k, cache_v, page_table, sb_bias):
    n, n_pages = page_table.shape
    n_pool = cache_k.shape[0]
    ck = cache_k.transpose(0, 2, 3, 1).reshape(n_pool, SB_WIDTH, PAGE_SIZE)
    cv = cache_v.transpose(0, 2, 3, 1).reshape(n_pool, SB_WIDTH, PAGE_SIZE)
    page_spec = lambda p: pl.BlockSpec((1, SB_WIDTH, PAGE_SIZE), lambda i, pt, p=p: (pt[i * n_pages + p], 0, 0))
    tok_spec = pl.BlockSpec((1, 1, SB_WIDTH), lambda i, pt: (i, 0, 0))
    rows = n_pages * SB_HEADS
    const = lambda shape: pl.BlockSpec(shape, lambda i, pt: (0, 0))
    r = jnp.arange(rows)
    later = ((r[:, None] % SB_HEADS == r[None, :] % SB_HEADS)
             & (r[None, :] // SB_HEADS > r[:, None] // SB_HEADS)).astype(BF16)
    out = pl.pallas_call(
        functools.partial(_attn_sample_kernel, n_pages=n_pages),
        out_shape=jax.ShapeDtypeStruct((n, 1, SB_WIDTH), BF16),
        grid_spec=pltpu.PrefetchScalarGridSpec(
            num_scalar_prefetch=1,
            grid=(n,),
            in_specs=[tok_spec, const((rows, 1)), const((PAGE_SIZE, PAGE_SIZE)), const((rows, rows))]
                     + [page_spec(p) for p in range(n_pages)] * 2,
            out_specs=tok_spec),
        compiler_params=_cparams(("parallel",)),
        name="attn_sample",
    )(page_table.reshape(-1).astype(jnp.int32), q.reshape(n, 1, SB_WIDTH),
      jnp.tile(sb_bias.astype(F32) * LOG2E, n_pages).reshape(rows, 1), _strict_upper(PAGE_SIZE), later,
      *([ck] * n_pages), *([cv] * n_pages))
    return out.reshape(n, SB_WIDTH)


def _first_index_of_max(vals, row_id, n_rows):
    m = jnp.max(vals, axis=0, keepdims=True)
    first = jnp.min(jnp.where(vals == m, row_id, n_rows), axis=0, keepdims=True)
    return m, first


def _route(logits_t, rbias):
    tm = logits_t.shape[1]
    scores = jax.nn.sigmoid(logits_t)
    biased = scores + rbias
    row8 = lax.broadcasted_iota(jnp.int32, (GROUP_SIZE, tm), 0)
    neg = jnp.float32(-jnp.inf)
    grp_score = []
    for g in range(N_GROUPS):
        blk = biased[g * GROUP_SIZE:(g + 1) * GROUP_SIZE]
        m1, first = _first_index_of_max(blk, row8, GROUP_SIZE)
        m2 = jnp.max(jnp.where(row8 == first, neg, blk), axis=0, keepdims=True)
        grp_score.append(m1 + m2)
    masked = []
    for g in range(N_GROUPS):
        beaten_by = jnp.zeros((1, tm), jnp.int32)
        for o in range(N_GROUPS):
            if o == g:
                continue
            wins = (grp_score[o] > grp_score[g]) if o > g else (grp_score[o] >= grp_score[g])
            beaten_by = beaten_by + wins.astype(jnp.int32)
        keep = beaten_by < TOPK_GROUPS
        masked.append(jnp.where(keep, biased[g * GROUP_SIZE:(g + 1) * GROUP_SIZE], neg))
    cur = jnp.concatenate(masked, axis=0)
    row = lax.broadcasted_iota(jnp.int32, (N_EXPERTS, tm), 0)
    chosen = jnp.zeros((N_EXPERTS, tm), jnp.bool_)
    for _ in range(TOP_K):
        _, first = _first_index_of_max(cur, row, N_EXPERTS)
        pick = row == first
        chosen = jnp.logical_or(chosen, pick)
        cur = jnp.where(pick, neg, cur)
    sel = jnp.where(chosen, scores, 0.0)
    return sel / jnp.sum(sel, axis=0, keepdims=True) * ROUTED_SCALE


def _post_kernel(x_ref, g_ref, o_ref, sga_ref, sgb_ref, gate1_ref, shift2_ref, scale2_ref, g2_ref,
                 wga_ref, wgb_ref, wsb_ref, wout_ref, wr_ref, rb_ref,
                 x1_ref, h2_ref, gates_ref):
    g = g_ref[...]
    branch_a = _dot(g, wga_ref[...]) * jax.nn.sigmoid(_dot(g, wgb_ref[...]))
    branch_b = _dot(o_ref[...], wsb_ref[...])
    merged = sga_ref[...].astype(F32) * branch_a + sgb_ref[...].astype(F32) * branch_b
    x1 = x_ref[...] + gate1_ref[...] * _dot(merged.astype(BF16), wout_ref[...])
    x1_ref[...] = x1
    r = lax.rsqrt(jnp.mean(x1 * x1, axis=-1, keepdims=True) + RMS_EPS)
    h2 = (x1 * r) * g2_ref[...]
    h2 = h2 * (1.0 + scale2_ref[...]) + shift2_ref[...]
    h2_ref[...] = h2.astype(BF16)
    h_hi, h_lo = _split_bf16(h2)
    w_hi, w_lo = _split_bf16(wr_ref[...])
    logits_t = _dot_t(w_hi, h_hi) + _dot_t(w_hi, h_lo) + _dot_t(w_lo, h_hi)
    gates_ref[...] = _route(logits_t, rb_ref[...])


def _post_mixer(x, g, o, sga, sgb, mod, per_token, tm, p):
    n = x.shape[0]
    tiles_per_mod = (n // mod.shape[0]) // tm if not per_token else 1
    row = lambda i: (i, 0)
    const = lambda i: (0, 0)
    blk = lambda w: pl.BlockSpec((tm, w), row)
    full = lambda a: pl.BlockSpec(a.shape, const)
    ms = lambda c: _mod_spec(per_token, tm, tiles_per_mod, c)
    weights = (p["w_glu_a"], p["w_glu_b"], p["w_sb_out"], p["w_out"], p["w_router_t"], p["router_bias"])
    return pl.pallas_call(
        _post_kernel,
        out_shape=(jax.ShapeDtypeStruct((n, D_MODEL), F32),
                   jax.ShapeDtypeStruct((n, D_MODEL), BF16),
                   jax.ShapeDtypeStruct((N_EXPERTS, n), F32)),
        grid=(n // tm,),
        in_specs=[blk(D_MODEL), blk(512), blk(512), blk(1024), blk(1024), ms(2), ms(3), ms(4),
                  pl.BlockSpec((1, D_MODEL), const)] + [full(w) for w in weights],
        out_specs=(blk(D_MODEL), blk(D_MODEL), pl.BlockSpec((N_EXPERTS, tm), lambda i: (0, i))),
        compiler_params=_cparams(("parallel",)),
        name="post_mixer",
    )(x, g, o, sga, sgb, mod, mod, mod, p["norm2_g"], *weights)


def _moe_kernel(h_ref, gates_ref, x1_ref, gate2_ref, wgs_ref, wus_ref, wds_ref,
                wg_ref, wu_ref, wd_ref, y_ref, acc_ref):
    e = pl.program_id(1)
    h = h_ref[...]

    @pl.when(e == 0)
    def _():
        act = _silu(_dot(h, wgs_ref[...])) * _dot(h, wus_ref[...])
        acc_ref[...] = _dot(act.astype(BF16), wds_ref[...])

    act = _silu(_dot(h, wg_ref[...])) * _dot(h, wu_ref[...])
    pick = (lax.broadcasted_iota(jnp.int32, (N_EXPERTS, 128), 0) == e).astype(BF16)
    g_hi, g_lo = _split_bf16(gates_ref[...])
    gcol = _dot(g_hi, pick) + _dot(g_lo, pick)
    act = act * jnp.concatenate([gcol, gcol], axis=1)
    acc_ref[...] += _dot(act.astype(BF16), wd_ref[...])

    @pl.when(e == pl.num_programs(1) - 1)
    def _():
        y_ref[...] = x1_ref[...] + gate2_ref[...] * acc_ref[...]


def _moe(h2, gates, x1, mod, per_token, tm, p):
    n = h2.shape[0]
    tiles_per_mod = (n // mod.shape[0]) // tm if not per_token else 1
    row = lambda i, e: (i, 0)
    const = lambda i, e: (0, 0)
    exp = lambda i, e: (e, 0, 0)
    if per_token:
        gate2_spec = pl.BlockSpec((tm, D_MODEL), lambda i, e: (i, 5))
    else:
        gate2_spec = pl.BlockSpec((None, 1, D_MODEL), lambda i, e: (i // tiles_per_mod, 0, 5))
    return pl.pallas_call(
        _moe_kernel,
        out_shape=jax.ShapeDtypeStruct((n, D_MODEL), F32),
        grid=(n // tm, N_EXPERTS),
        in_specs=[pl.BlockSpec((tm, D_MODEL), row),
                  pl.BlockSpec((tm, N_EXPERTS), row),
                  pl.BlockSpec((tm, D_MODEL), row),
                  gate2_spec,
                  pl.BlockSpec((D_MODEL, EXPERT_HIDDEN), const),
                  pl.BlockSpec((D_MODEL, EXPERT_HIDDEN), const),
                  pl.BlockSpec((EXPERT_HIDDEN, D_MODEL), const),
                  pl.BlockSpec((None, D_MODEL, EXPERT_HIDDEN), exp),
                  pl.BlockSpec((None, D_MODEL, EXPERT_HIDDEN), exp),
                  pl.BlockSpec((None, EXPERT_HIDDEN, D_MODEL), exp)],
        out_specs=pl.BlockSpec((tm, D_MODEL), row),
        scratch_shapes=[pltpu.VMEM((tm, D_MODEL), F32)],
        compiler_params=_cparams(("parallel", "arbitrary")),
        name="moe",
    )(h2, gates, x1, mod, p["w_gate_s"], p["w_up_s"], p["w_down_s"], p["w_gate_e"], p["w_up_e"], p["w_down_e"])


def _layer(x_prompt, x_sample, cache_k, cache_v, s0_re, s0_im, page_table, c_prompt, c_sample, p):
    batch, t_len, d = x_prompt.shape
    n_s = x_sample.shape[0]
    n_p = batch * t_len

    c_all = jnp.concatenate([c_prompt, c_sample], axis=0)
    pad = (-c_all.shape[0]) % 8
    c_all = jnp.pad(c_all, ((0, pad), (0, 0)))
    mod = _ada_mod(c_all, p["w_ada"], p["b_ada"])
    mod_p = mod[:batch].reshape(batch, 1, 6 * d)
    mod_s = mod[batch:batch + n_s]

    a_re, a_im, bb_re, bb_im = _ssm_prep(p["lam_re"], p["lam_im"], p["log_dt"], p["b_re"], p["b_im"])
    ssm = (a_re, a_im, bb_re, bb_im, _blockdiag_out(p["c_re"]), _blockdiag_out(-p["c_im"]),
           p["d_skip"].astype(F32).reshape(1, SSM_WIDTH))

    xp = x_prompt.reshape(n_p, d)
    xs = x_sample.reshape(n_s, d)
    proj_p = _in_proj(xp, mod_p, False, 512, t_len, p["norm1_g"], p["w_in"], p["q_norm_g"], p["k_norm_g"])
    proj_s = _in_proj(xs, mod_s, True, n_s, n_s, p["norm1_g"], p["w_in"], p["q_norm_g"], p["k_norm_g"])
    u_p, q_p, k_p, kb_p, v_p, vb_p, sga_p, sgb_p = proj_p
    u_s, q_s, k_s, _, v_s, _, sga_s, sgb_s = proj_s

    g_p, st_re_p, st_im_p = _ssm_prompt(u_p.reshape(batch, t_len, SSM_WIDTH), ssm)
    g_p = g_p.reshape(n_p, SSM_WIDTH)
    g_s, st_re_s, st_im_s = _ssm_sample(u_s, s0_re.reshape(n_s, SSM_FLAT), s0_im.reshape(n_s, SSM_FLAT), ssm)

    o_p = _attn_prompt(q_p, kb_p, vb_p, p["sb_bias"], batch, t_len)
    o_s = _attn_sample(q_s, cache_k, cache_v, page_table, p["sb_bias"])

    x1_p, h2_p, gates_p = _post_mixer(xp, g_p, o_p, sga_p, sgb_p, mod_p, False, 512, p)
    x1_s, h2_s, gates_s = _post_mixer(xs, g_s, o_s, sga_s, sgb_s, mod_s, True, n_s, p)

    y_p = _moe(h2_p, gates_p.T, x1_p, mod_p, False, 1024, p)
    y_s = _moe(h2_s, gates_s.T, x1_s, mod_s, True, n_s, p)

    hd = (SB_HEADS, SB_HEAD_DIM)
    gs = (SSM_GROUPS, SSM_STATE)

    def kv_out(a_t, n_seq, seq_len):
        return a_t.reshape(n_seq, *hd, seq_len).transpose(0, 3, 1, 2)

    return (y_p.reshape(batch, t_len, d), y_s.reshape(n_s, 1, d),
            kv_out(k_p, batch, t_len), kv_out(v_p, batch, t_len),
            st_re_p.reshape(batch, *gs), st_im_p.reshape(batch, *gs),
            kv_out(k_s, 1, n_s).reshape(n_s, 1, *hd), kv_out(v_s, 1, n_s).reshape(n_s, 1, *hd),
            st_re_s.reshape(n_s, *gs), st_im_s.reshape(n_s, *gs))


def kernel(x_prompt, x_sample, cache_k, cache_v, state_ssm_re, state_ssm_im, page_table, c_prompt, c_sample,
           w_ada, b_ada, norm1_g, norm2_g, w_in, q_norm_g, k_norm_g, sb_bias, ssm_lambda_re, ssm_lambda_im,
           ssm_log_dt, ssm_b_re, ssm_b_im, ssm_c_re, ssm_c_im, ssm_d, w_glu_a, w_glu_b, w_sb_out, w_out,
           w_router, router_bias, w_gate_e, w_up_e, w_down_e, w_gate_s, w_up_s, w_down_s):
    depth = w_in.shape[0]
    assert depth == 1, "single-layer step"
    l = 0
    p = dict(
        w_ada=w_ada[l], b_ada=b_ada[l],
        norm1_g=norm1_g[l], norm2_g=norm2_g[l].reshape(1, D_MODEL),
        w_in=w_in[l].astype(BF16), q_norm_g=q_norm_g[l], k_norm_g=k_norm_g[l], sb_bias=sb_bias[l],
        lam_re=ssm_lambda_re[l], lam_im=ssm_lambda_im[l], log_dt=ssm_log_dt[l],
        b_re=ssm_b_re[l], b_im=ssm_b_im[l], c_re=ssm_c_re[l], c_im=ssm_c_im[l], d_skip=ssm_d[l],
        w_glu_a=w_glu_a[l].astype(BF16), w_glu_b=w_glu_b[l].astype(BF16),
        w_sb_out=w_sb_out[l].astype(BF16), w_out=w_out[l].astype(BF16),
        w_router_t=w_router[l].T, router_bias=router_bias[l].reshape(N_EXPERTS, 1),
        w_gate_e=w_gate_e[l].astype(BF16), w_up_e=w_up_e[l].astype(BF16), w_down_e=w_down_e[l].astype(BF16),
        w_gate_s=w_gate_s[l].astype(BF16), w_up_s=w_up_s[l].astype(BF16), w_down_s=w_down_s[l].astype(BF16),
    )
    outs = _layer(x_prompt, x_sample, cache_k[l], cache_v[l], state_ssm_re[l], state_ssm_im[l],
                  page_table, c_prompt, c_sample, p)
    y_p, y_s = outs[0], outs[1]
    return (y_p, y_s) + tuple(o[None] for o in outs[2:])
```

```python
import functools
import math

import jax
import jax.numpy as jnp
from jax import lax
from jax.experimental import pallas as pl
from jax.experimental.pallas import tpu as pltpu

F32 = jnp.float32
BF16 = jnp.bfloat16

D_MODEL = 1024
SSM_WIDTH = 512
SSM_GROUP = 16
SSM_GROUPS = 32
SSM_STATE = 64
SSM_FLAT = SSM_GROUPS * SSM_STATE
EIG_CLIP = -1e-4
SB_HEADS = 8
SB_HEAD_DIM = 64
SB_WIDTH = 512
PAGE_SIZE = 128
N_EXPERTS = 64
TOP_K = 8
N_GROUPS = 8
TOPK_GROUPS = 4
GROUP_SIZE = N_EXPERTS // N_GROUPS
EXPERT_HIDDEN = 256
EXPERTS_PER_STEP = 4
ROUTED_SCALE = 2.5
RMS_EPS = 1e-6

LOG2E = math.log2(math.e)
LOGIT_SCALE_LOG2 = SB_HEAD_DIM ** -0.5 * LOG2E
HEADS_PER_STEP = 4
HEAD_LANES = HEADS_PER_STEP * SB_HEAD_DIM
VMEM_LIMIT = 48 * 1024 * 1024


def _cparams(sem):
    return pltpu.CompilerParams(dimension_semantics=sem, vmem_limit_bytes=VMEM_LIMIT)


def _dot(a, b):
    return jnp.dot(a, b, preferred_element_type=F32)


def _dot_t(a, b):
    return lax.dot_general(a, b, (((1,), (1,)), ((), ())), preferred_element_type=F32)


def _split_bf16(x):
    hi = x.astype(BF16)
    lo = (x - hi.astype(F32)).astype(BF16)
    return hi, lo


def _silu(x):
    return x * jax.nn.sigmoid(x)


def _gelu_tanh(x):
    c = math.sqrt(2.0 / math.pi)
    return 0.5 * x * (1.0 + jnp.tanh(c * (x + 0.044715 * (x * x * x))))


def _ada_kernel(c_ref, w_ref, b_ref, o_ref):
    s = _silu(c_ref[...])
    s_hi, s_lo = _split_bf16(s)
    w_hi, w_lo = _split_bf16(w_ref[...])
    o_ref[...] = _dot(s_hi, w_hi) + _dot(s_hi, w_lo) + _dot(s_lo, w_hi) + b_ref[...]


def _ada_mod(c, w_ada, b_ada):
    rows, d = c.shape
    cols = w_ada.shape[1]
    tn = 1024
    return pl.pallas_call(
        _ada_kernel,
        out_shape=jax.ShapeDtypeStruct((rows, cols), F32),
        grid=(cols // tn,),
        in_specs=[pl.BlockSpec((rows, d), lambda j: (0, 0)),
                  pl.BlockSpec((d, tn), lambda j: (0, j)),
                  pl.BlockSpec((1, tn), lambda j: (0, j))],
        out_specs=pl.BlockSpec((rows, tn), lambda j: (0, j)),
        compiler_params=_cparams(("parallel",)),
        name="ada_mod",
    )(c, w_ada, b_ada.reshape(1, cols))


def _ssm_prep_kernel(lre_ref, lim_ref, ldt_ref, bre_ref, bim_ref,
                     are_ref, aim_ref, bbre_ref, bbim_ref):
    dt = jnp.exp(ldt_ref[...])
    lre = jnp.minimum(lre_ref[...], EIG_CLIP)
    lim = lim_ref[...]
    mag = jnp.exp(lre * dt)
    a_re = mag * jnp.cos(lim * dt)
    a_im = mag * jnp.sin(lim * dt)
    den = lre * lre + lim * lim
    f_re = ((a_re - 1.0) * lre + a_im * lim) / den
    f_im = (a_im * lre - (a_re - 1.0) * lim) / den
    br = bre_ref[...]
    bi = bim_ref[...]
    are_ref[...] = a_re
    aim_ref[...] = a_im
    bbre_ref[...] = f_re * br - f_im * bi
    bbim_ref[...] = f_re * bi + f_im * br


def _ssm_prep(lam_re, lam_im, log_dt, b_re, b_im):
    rep = lambda a: jnp.repeat(a.astype(F32), SSM_GROUP, axis=0)
    lre, lim = rep(lam_re), rep(lam_im)
    ldt = rep(jnp.broadcast_to(log_dt.astype(F32)[:, None], (SSM_GROUPS, SSM_STATE)))
    tr = lambda b: b.astype(F32).transpose(0, 2, 1).reshape(SSM_WIDTH, SSM_STATE)
    shp = jax.ShapeDtypeStruct((SSM_WIDTH, SSM_STATE), F32)
    a_re, a_im, bb_re, bb_im = pl.pallas_call(
        _ssm_prep_kernel, out_shape=(shp, shp, shp, shp), name="ssm_prep",
    )(lre, lim, ldt, tr(b_re), tr(b_im))
    eye = jnp.eye(SSM_GROUPS, dtype=F32)

    def blockdiag_in(bb):
        bb4 = bb.reshape(SSM_GROUPS, SSM_GROUP, 1, SSM_STATE) * eye.reshape(SSM_GROUPS, 1, SSM_GROUPS, 1)
        return bb4.reshape(SSM_WIDTH, SSM_FLAT).astype(BF16)

    a_re = a_re[::SSM_GROUP].reshape(1, SSM_FLAT)
    a_im = a_im[::SSM_GROUP].reshape(1, SSM_FLAT)
    return a_re, a_im, blockdiag_in(bb_re), blockdiag_in(bb_im)


def _blockdiag_out(c):
    eye = jnp.eye(SSM_GROUPS, dtype=F32)
    c4 = c.astype(F32).transpose(0, 2, 1).reshape(SSM_GROUPS, SSM_STATE, 1, SSM_GROUP)
    return (c4 * eye.reshape(SSM_GROUPS, 1, SSM_GROUPS, 1)).reshape(SSM_FLAT, SSM_WIDTH).astype(BF16)


def _inproj_kernel(x_ref, shift_ref, scale_ref, g1_ref, w_ref, qg_ref, kg_ref, hm_ref,
                   u_ref, q_ref, k_ref, kb_ref, v_ref, vb_ref, sga_ref, sgb_ref):
    x = x_ref[...]
    r = lax.rsqrt(jnp.mean(x * x, axis=-1, keepdims=True) + RMS_EPS)
    h = (x * r) * g1_ref[...]
    h = h * (1.0 + scale_ref[...]) + shift_ref[...]
    hb = h.astype(BF16)
    hm = hm_ref[...]

    def head_norm(t, g):
        ms = _dot((t * t).astype(BF16), hm)
        return (t * lax.rsqrt(ms + RMS_EPS)) * g

    u_ref[...] = _dot(hb, w_ref[:, 0:512])
    q = head_norm(_dot(hb, w_ref[:, 512:1024]), qg_ref[...])
    q_ref[...] = (q * LOGIT_SCALE_LOG2).astype(BF16)
    k = head_norm(_dot(hb, w_ref[:, 1024:1536]), kg_ref[...])
    k_ref[...] = k.T
    kb_ref[...] = k.astype(BF16)
    v = _dot(hb, w_ref[:, 1536:2048])
    v_ref[...] = v.T
    vb_ref[...] = v.astype(BF16)
    sga_ref[...] = jax.nn.sigmoid(_dot(hb, w_ref[:, 2048:3072])).astype(BF16)
    sgb_ref[...] = jax.nn.sigmoid(_dot(hb, w_ref[:, 3072:4096])).astype(BF16)


def _head_mean_matrix():
    idx = jnp.arange(SB_WIDTH) // SB_HEAD_DIM
    return ((idx[:, None] == idx[None, :]).astype(F32) / SB_HEAD_DIM).astype(BF16)


def _mod_spec(per_token, tm, rows_per_mod, chunk):
    if per_token:
        return pl.BlockSpec((tm, D_MODEL), lambda i: (i, chunk))
    return pl.BlockSpec((None, 1, D_MODEL), lambda i: (i // rows_per_mod, 0, chunk))


def _in_proj(x, mod, per_token, tm, seq_len, norm1_g, w_in_b, q_g, k_g):
    n = x.shape[0]
    tiles_per_mod = (n // mod.shape[0]) // tm if not per_token else 1
    tiles_per_seq = seq_len // tm
    row = lambda i: (i, 0)
    const = lambda i: (0, 0)
    f32o = lambda w: jax.ShapeDtypeStruct((n, w), F32)
    b16o = lambda w: jax.ShapeDtypeStruct((n, w), BF16)
    kvo = jax.ShapeDtypeStruct((n // seq_len, SB_WIDTH, seq_len), F32)
    kv_blk = pl.BlockSpec((None, SB_WIDTH, tm), lambda i: (i // tiles_per_seq, 0, i % tiles_per_seq))
    blk = lambda w: pl.BlockSpec((tm, w), row)
    return pl.pallas_call(
        _inproj_kernel,
        out_shape=(f32o(512), b16o(512), kvo, b16o(512), kvo, b16o(512), b16o(1024), b16o(1024)),
        grid=(n // tm,),
        in_specs=[blk(D_MODEL),
                  _mod_spec(per_token, tm, tiles_per_mod, 0),
                  _mod_spec(per_token, tm, tiles_per_mod, 1),
                  pl.BlockSpec((1, D_MODEL), const),
                  pl.BlockSpec(w_in_b.shape, const),
                  pl.BlockSpec((1, SB_WIDTH), const),
                  pl.BlockSpec((1, SB_WIDTH), const),
                  pl.BlockSpec((SB_WIDTH, SB_WIDTH), const)],
        out_specs=(blk(512), blk(512), kv_blk, blk(512), kv_blk, blk(512), blk(1024), blk(1024)),
        compiler_params=_cparams(("parallel",)),
        name="in_proj",
    )(x, mod, mod, norm1_g.reshape(1, D_MODEL), w_in_b,
      jnp.tile(q_g, SB_HEADS).reshape(1, SB_WIDTH), jnp.tile(k_g, SB_HEADS).reshape(1, SB_WIDTH),
      _head_mean_matrix())


def _ssm_drive(ub, bb_ref):
    half_c, half_s = SSM_WIDTH // 2, SSM_FLAT // 2
    lo = _dot(ub[:, :half_c], bb_ref[:half_c, :half_s])
    hi = _dot(ub[:, half_c:], bb_ref[half_c:, half_s:])
    return lo, hi


def _ssm_readout(xre, xim, ccre_ref, ccim_ref, d_ref, u):
    half_c, half_s = SSM_WIDTH // 2, SSM_FLAT // 2
    xr = xre.astype(BF16)
    xi = xim.astype(BF16)
    y_lo = _dot(xr[:, :half_s], ccre_ref[:half_s, :half_c]) + _dot(xi[:, :half_s], ccim_ref[:half_s, :half_c])
    y_hi = _dot(xr[:, half_s:], ccre_ref[half_s:, half_c:]) + _dot(xi[:, half_s:], ccim_ref[half_s:, half_c:])
    y = jnp.concatenate([y_lo, y_hi], axis=1) + d_ref[...] * u
    return _gelu_tanh(y)


def _ssm_prompt_kernel(u_ref, bbre_ref, bbim_ref, are_ref, aim_ref, ccre_ref, ccim_ref, d_ref,
                       g_ref, sre_ref, sim_ref, ubuf, gbuf, xre, xim, cre, cim, *, lane_chunk):
    batch, tt, _ = u_ref.shape
    rows = tt * batch
    half_s = SSM_FLAT // 2

    @pl.when(pl.program_id(0) == 0)
    def _():
        cre[...] = jnp.zeros_like(cre)
        cim[...] = jnp.zeros_like(cim)

    n_lane_tiles = SSM_WIDTH // 128
    for b in range(batch):
        for c in range(n_lane_tiles):
            ubuf[c, pl.ds(b, tt, stride=batch), :] = u_ref[b, :, c * 128:(c + 1) * 128]
    u = jnp.concatenate([ubuf[c] for c in range(n_lane_tiles)], axis=1)
    ub = u.astype(BF16)
    lo, hi = _ssm_drive(ub, bbre_ref)
    xre[:, :half_s] = lo
    xre[:, half_s:] = hi
    lo, hi = _ssm_drive(ub, bbim_ref)
    xim[:, :half_s] = lo
    xim[:, half_s:] = hi

    first_step = lax.broadcasted_iota(jnp.int32, (8, lane_chunk), 0) < batch
    for c0 in range(0, SSM_FLAT, lane_chunk):
        cols = pl.ds(c0, lane_chunk)
        ar = jnp.broadcast_to(are_ref[:, cols], (8, lane_chunk))
        ai = jnp.broadcast_to(aim_ref[:, cols], (8, lane_chunk))

        def step(i, carry, cols=cols, ar=ar, ai=ai):
            c_r, c_i = carry
            r0 = pl.multiple_of(i * 8, 8)
            br = xre[pl.ds(r0, 8), cols]
            bi = xim[pl.ds(r0, 8), cols]
            p_r = pltpu.roll(c_r, batch, 0)
            p_i = pltpu.roll(c_i, batch, 0)
            y_r = ar * p_r - ai * p_i + br
            y_i = ar * p_i + ai * p_r + bi
            q_r = pltpu.roll(y_r, batch, 0)
            q_i = pltpu.roll(y_i, batch, 0)
            z_r = ar * q_r - ai * q_i + br
            z_i = ar * q_i + ai * q_r + bi
            xre[pl.ds(r0, 8), cols] = jnp.where(first_step, y_r, z_r)
            xim[pl.ds(r0, 8), cols] = jnp.where(first_step, y_i, z_i)
            return z_r, z_i

        c_r, c_i = lax.fori_loop(0, rows // 8, step, (cre[:, cols], cim[:, cols]))
        cre[:, cols] = c_r
        cim[:, cols] = c_i

    g = _ssm_readout(xre[...], xim[...], ccre_ref, ccim_ref, d_ref, u)
    for c in range(n_lane_tiles):
        gbuf[c] = g[:, c * 128:(c + 1) * 128]
    for b in range(batch):
        g_ref[b] = jnp.concatenate([gbuf[c, pl.ds(b, tt, stride=batch), :] for c in range(n_lane_tiles)],
                                   axis=1).astype(BF16)
    sre_ref[...] = cre[...]
    sim_ref[...] = cim[...]


def _ssm_prompt(u, ssm, tt=128, lane_chunk=512):
    a_re, a_im, bb_re, bb_im, cc_re, cc_im, d_skip = ssm
    batch, t_len, _ = u.shape
    assert 2 * batch == 8, "one sublane tile must hold exactly two time steps"
    rows = tt * batch
    const = lambda j: (0, 0)
    full = lambda a: pl.BlockSpec(a.shape, const)
    seq_blk = pl.BlockSpec((batch, tt, SSM_WIDTH), lambda j: (0, j, 0))
    g, s_re, s_im = pl.pallas_call(
        functools.partial(_ssm_prompt_kernel, lane_chunk=lane_chunk),
        out_shape=(jax.ShapeDtypeStruct((batch, t_len, SSM_WIDTH), BF16),
                   jax.ShapeDtypeStruct((8, SSM_FLAT), F32),
                   jax.ShapeDtypeStruct((8, SSM_FLAT), F32)),
        grid=(t_len // tt,),
        in_specs=[seq_blk,
                  full(bb_re), full(bb_im), full(a_re), full(a_im), full(cc_re), full(cc_im), full(d_skip)],
        out_specs=(seq_blk, pl.BlockSpec((8, SSM_FLAT), const), pl.BlockSpec((8, SSM_FLAT), const)),
        scratch_shapes=[pltpu.VMEM((SSM_WIDTH // 128, rows, 128), F32),
                        pltpu.VMEM((SSM_WIDTH // 128, rows, 128), F32),
                        pltpu.VMEM((rows, SSM_FLAT), F32), pltpu.VMEM((rows, SSM_FLAT), F32),
                        pltpu.VMEM((8, SSM_FLAT), F32), pltpu.VMEM((8, SSM_FLAT), F32)],
        compiler_params=_cparams(("arbitrary",)),
        name="ssm_prompt",
    )(u, bb_re, bb_im, a_re, a_im, cc_re, cc_im, d_skip)
    return g, s_re[batch:], s_im[batch:]


def _ssm_sample_kernel(u_ref, s0re_ref, s0im_ref, bbre_ref, bbim_ref, are_ref, aim_ref,
                       ccre_ref, ccim_ref, d_ref, g_ref, sre_ref, sim_ref):
    u = u_ref[...]
    ub = u.astype(BF16)
    ar, ai = are_ref[...], aim_ref[...]
    s_r, s_i = s0re_ref[...], s0im_ref[...]
    lo, hi = _ssm_drive(ub, bbre_ref)
    x_r = ar * s_r - ai * s_i + jnp.concatenate([lo, hi], axis=1)
    lo, hi = _ssm_drive(ub, bbim_ref)
    x_i = ar * s_i + ai * s_r + jnp.concatenate([lo, hi], axis=1)
    sre_ref[...] = x_r
    sim_ref[...] = x_i
    g_ref[...] = _ssm_readout(x_r, x_i, ccre_ref, ccim_ref, d_ref, u).astype(BF16)


def _ssm_sample(u, s_re, s_im, ssm):
    a_re, a_im, bb_re, bb_im, cc_re, cc_im, d_skip = ssm
    n = u.shape[0]
    return pl.pallas_call(
        _ssm_sample_kernel,
        out_shape=(jax.ShapeDtypeStruct((n, SSM_WIDTH), BF16),
                   jax.ShapeDtypeStruct((n, SSM_FLAT), F32),
                   jax.ShapeDtypeStruct((n, SSM_FLAT), F32)),
        compiler_params=pltpu.CompilerParams(vmem_limit_bytes=VMEM_LIMIT),
        name="ssm_sample",
    )(u, s_re, s_im, bb_re, bb_im, a_re, a_im, cc_re, cc_im, d_skip)


def _log_sigmoid_pair(z2):
    sign = jnp.uint32(0x80000000)
    neg_abs = lax.bitcast_convert_type(lax.bitcast_convert_type(z2, jnp.uint32) | sign, F32)
    soft = jnp.log2(1.0 + jnp.exp2(neg_abs))
    lsig = jnp.minimum(z2, 0.0) - soft
    return lsig, lsig - z2


def _attn_prompt_kernel(bias_ref, q_ref, k_ref, v_ref, tri_ref, o_ref, acc_ref, run_ref, *, tq, tk):
    qi = pl.program_id(1)
    nh = HEADS_PER_STEP
    ngrp = SB_HEADS // nh
    lane_head = lax.broadcasted_iota(jnp.int32, (1, HEAD_LANES), 1) // SB_HEAD_DIM
    head_mask = [(lane_head == h).astype(F32).astype(BF16) for h in range(nh)]
    qs, bias = [], []
    for g in range(ngrp):
        q = q_ref[:, g * HEAD_LANES:(g + 1) * HEAD_LANES]
        qs.append(jnp.concatenate([q * head_mask[h] for h in range(nh)], axis=0))
        bias.append([bias_ref[g * nh + h] for h in range(nh)])
    tri = tri_ref[...]
    acc_ref[...] = jnp.zeros_like(acc_ref)
    run_ref[...] = jnp.zeros_like(run_ref)

    def block(js, masked):
        if masked:
            k0 = pl.multiple_of(js[0] * tk, tk)
            kpos = k0 + lax.broadcasted_iota(jnp.int32, (nh * tq, tk), 1)
            qpos = qi * tq + (lax.broadcasted_iota(jnp.int32, (nh * tq, tk), 0) & (tq - 1))
            valid = kpos < qpos
        for g in range(ngrp):
            lanes = slice(g * HEAD_LANES, (g + 1) * HEAD_LANES)
            run = run_ref[g]
            pv = None
            for j in js:
                k0 = pl.multiple_of(j * tk, tk)
                zs = _dot_t(qs[g], k_ref[pl.ds(k0, tk), lanes])
                lsig_parts, stay_parts = [], []
                for h in range(nh):
                    lsig, stay = _log_sigmoid_pair(zs[h * tq:(h + 1) * tq] + bias[g][h])
                    lsig_parts.append(lsig)
                    stay_parts.append(stay)
                lsig = jnp.concatenate(lsig_parts, axis=0)
                stay = jnp.concatenate(stay_parts, axis=0)
                if masked:
                    stay = jnp.where(valid, stay, 0.0)
                after = _dot(stay.astype(BF16), tri)
                w = jnp.exp2(lsig + after + run)
                if masked:
                    w = jnp.where(valid, w, 0.0)
                part = _dot(w.astype(BF16), v_ref[pl.ds(k0, tk), lanes])
                pv = part if pv is None else pv + part
                run = run + (after[:, 0:1] + stay[:, 0:1])
            acc_ref[g] += pv
            run_ref[g] = run

    j_diag = (qi * tq) // tk
    block((j_diag,), True)

    @pl.when(j_diag % 2 == 1)
    def _():
        block((j_diag - 1,), False)

    def body(jj, carry):
        j = 2 * (j_diag // 2 - jj) - 1
        block((j, j - 1), False)
        return carry

    lax.fori_loop(0, j_diag // 2, body, 0)
    for g in range(ngrp):
        acc = acc_ref[g]
        out = jnp.zeros((tq, HEAD_LANES), F32)
        for h in range(nh):
            out = jnp.where(lane_head == h, acc[h * tq:(h + 1) * tq], out)
        o_ref[:, g * HEAD_LANES:(g + 1) * HEAD_LANES] = out.astype(BF16)


def _strict_upper(tk):
    idx = jnp.arange(tk)
    return (idx[:, None] > idx[None, :]).astype(BF16)


def _attn_prompt(q, kb, vb, sb_bias, batch, t_len, tq=256, tk=256):
    assert tq & (tq - 1) == 0 and tq <= tk, "one masked key block must cover the diagonal"
    n = q.shape[0]
    nq = t_len // tq
    ngrp = SB_HEADS // HEADS_PER_STEP
    return pl.pallas_call(
        functools.partial(_attn_prompt_kernel, tq=tq, tk=tk),
        out_shape=jax.ShapeDtypeStruct((n, SB_WIDTH), BF16),
        grid=(batch, nq),
        in_specs=[pl.BlockSpec(memory_space=pltpu.SMEM),
                  pl.BlockSpec((tq, SB_WIDTH), lambda b, i: (b * nq + i, 0)),
                  pl.BlockSpec((t_len, SB_WIDTH), lambda b, i: (b, 0)),
                  pl.BlockSpec((t_len, SB_WIDTH), lambda b, i: (b, 0)),
                  pl.BlockSpec((tk, tk), lambda b, i: (0, 0))],
        out_specs=pl.BlockSpec((tq, SB_WIDTH), lambda b, i: (b * nq + i, 0)),
        scratch_shapes=[pltpu.VMEM((ngrp, HEADS_PER_STEP * tq, HEAD_LANES), F32),
                        pltpu.VMEM((ngrp, HEADS_PER_STEP * tq, 1), F32)],
        compiler_params=_cparams(("parallel", "arbitrary")),
        name="attn_prompt",
    )(sb_bias.astype(F32) * LOG2E, q, kb, vb, _strict_upper(tk))


def _attn_sample_kernel(pt_ref, q_ref, bias_ref, tri_ref, later_ref, *refs, n_pages):
    del pt_ref
    k_refs = refs[:n_pages]
    v_refs = refs[n_pages:2 * n_pages]
    o_ref = refs[2 * n_pages]
    hd = (SB_HEADS, SB_HEAD_DIM, PAGE_SIZE)
    q = q_ref[0].astype(F32)
    q_col = jnp.broadcast_to(q, (PAGE_SIZE, SB_WIDTH)).T
    q3 = q_col.reshape(hd)
    z = jnp.concatenate([jnp.sum(k_refs[p][0].reshape(hd) * q3, axis=1) for p in range(n_pages)], axis=0)
    lsig, stay = _log_sigmoid_pair(z + bias_ref[...])
    after = _dot(stay.astype(BF16), tri_ref[...])
    total = jnp.broadcast_to(after[:, 0:1] + stay[:, 0:1], stay.shape)
    t_hi, t_lo = _split_bf16(total)
    later = later_ref[...]
    run = _dot(later, t_hi) + _dot(later, t_lo)
    w = jnp.exp2(lsig + after + run)
    acc = [jnp.zeros((SB_HEAD_DIM, PAGE_SIZE), F32) for _ in range(SB_HEADS)]
    for p in range(n_pages):
        for h in range(SB_HEADS):
            r = p * SB_HEADS + h
            acc[h] = acc[h] + v_refs[p][0, h * SB_HEAD_DIM:(h + 1) * SB_HEAD_DIM, :] * w[r:r + 1, :]
    a_hi, a_lo = _split_bf16(jnp.concatenate(acc, axis=0))
    ones = jnp.ones((8, PAGE_SIZE), BF16)
    o_ref[0] = (_dot_t(ones, a_hi) + _dot_t(ones, a_lo))[0:1].astype(BF16)


def _attn_sample(q, cache_k, cache_v, page_table, sb_bias):
    n, n_pages = page_table.shape
    n_pool = cache_k.shape[0]
    ck = cache_k.transpose(0, 2, 3, 1).reshape(n_pool, SB_WIDTH, PAGE_SIZE)
    cv = cache_v.transpose(0, 2, 3, 1).reshape(n_pool, SB_WIDTH, PAGE_SIZE)
    page_spec = lambda p: pl.BlockSpec((1, SB_WIDTH, PAGE_SIZE), lambda i, pt, p=p: (pt[i * n_pages + p], 0, 0))
    tok_spec = pl.BlockSpec((1, 1, SB_WIDTH), lambda i, pt: (i, 0, 0))
    rows = n_pages * SB_HEADS
    const = lambda shape: pl.BlockSpec(shape, lambda i, pt: (0, 0))
    r = jnp.arange(rows)
    later = ((r[:, None] % SB_HEADS == r[None, :] % SB_HEADS)
             & (r[None, :] // SB_HEADS > r[:, None] // SB_HEADS)).astype(BF16)
    out = pl.pallas_call(
        functools.partial(_attn_sample_kernel, n_pages=n_pages),
        out_shape=jax.ShapeDtypeStruct((n, 1, SB_WIDTH), BF16),
        grid_spec=pltpu.PrefetchScalarGridSpec(
            num_scalar_prefetch=1,
            grid=(n,),
            in_specs=[tok_spec, const((rows, 1)), const((PAGE_SIZE, PAGE_SIZE)), const((rows, rows))]
                     + [page_spec(p) for p in range(n_pages)] * 2,
            out_specs=tok_spec),
        compiler_params=_cparams(("parallel",)),
        name="attn_sample",
    )(page_table.reshape(-1).astype(jnp.int32), q.reshape(n, 1, SB_WIDTH),
      jnp.tile(sb_bias.astype(F32) * LOG2E, n_pages).reshape(rows, 1), _strict_upper(PAGE_SIZE), later,
      *([ck] * n_pages), *([cv] * n_pages))
    return out.reshape(n, SB_WIDTH)


def _first_index_of_max(vals, row_id, n_rows):
    m = jnp.max(vals, axis=0, keepdims=True)
    first = jnp.min(jnp.where(vals == m, row_id, n_rows), axis=0, keepdims=True)
    return m, first


def _route(logits_t, rbias):
    tm = logits_t.shape[1]
    scores = jax.nn.sigmoid(logits_t)
    biased = scores + rbias
    row8 = lax.broadcasted_iota(jnp.int32, (GROUP_SIZE, tm), 0)
    neg = jnp.float32(-jnp.inf)
    grp_score = []
    for g in range(N_GROUPS):
        blk = biased[g * GROUP_SIZE:(g + 1) * GROUP_SIZE]
        m1, first = _first_index_of_max(blk, row8, GROUP_SIZE)
        m2 = jnp.max(jnp.where(row8 == first, neg, blk), axis=0, keepdims=True)
        grp_score.append(m1 + m2)
    masked = []
    for g in range(N_GROUPS):
        beaten_by = jnp.zeros((1, tm), jnp.int32)
        for o in range(N_GROUPS):
            if o == g:
                continue
            wins = (grp_score[o] > grp_score[g]) if o > g else (grp_score[o] >= grp_score[g])
            beaten_by = beaten_by + wins.astype(jnp.int32)
        keep = beaten_by < TOPK_GROUPS
        masked.append(jnp.where(keep, biased[g * GROUP_SIZE:(g + 1) * GROUP_SIZE], neg))
    cur = jnp.concatenate(masked, axis=0)
    row = lax.broadcasted_iota(jnp.int32, (N_EXPERTS, tm), 0)
    chosen = jnp.zeros((N_EXPERTS, tm), jnp.bool_)
    for _ in range(TOP_K):
        _, first = _first_index_of_max(cur, row, N_EXPERTS)
        pick = row == first
        chosen = jnp.logical_or(chosen, pick)
        cur = jnp.where(pick, neg, cur)
    sel = jnp.where(chosen, scores, 0.0)
    return sel / jnp.sum(sel, axis=0, keepdims=True) * ROUTED_SCALE


def _post_kernel(x_ref, g_ref, o_ref, sga_ref, sgb_ref, gate1_ref, shift2_ref, scale2_ref, g2_ref,
                 wga_ref, wgb_ref, wsb_ref, wout_ref, wr_ref, rb_ref,
                 x1_ref, h2_ref, gates_ref):
    g = g_ref[...]
    branch_a = _dot(g, wga_ref[...]) * jax.nn.sigmoid(_dot(g, wgb_ref[...]))
    branch_b = _dot(o_ref[...], wsb_ref[...])
    merged = sga_ref[...].astype(F32) * branch_a + sgb_ref[...].astype(F32) * branch_b
    x1 = x_ref[...] + gate1_ref[...] * _dot(merged.astype(BF16), wout_ref[...])
    x1_ref[...] = x1
    r = lax.rsqrt(jnp.mean(x1 * x1, axis=-1, keepdims=True) + RMS_EPS)
    h2 = (x1 * r) * g2_ref[...]
    h2 = h2 * (1.0 + scale2_ref[...]) + shift2_ref[...]
    h2_ref[...] = h2.astype(BF16)
    h_hi, h_lo = _split_bf16(h2)
    w_hi, w_lo = _split_bf16(wr_ref[...])
    logits_t = _dot_t(w_hi, h_hi) + _dot_t(w_hi, h_lo) + _dot_t(w_lo, h_hi)
    gates_ref[...] = _route(logits_t, rb_ref[...])


def _post_mixer(x, g, o, sga, sgb, mod, per_token, tm, p):
    n = x.shape[0]
    tiles_per_mod = (n // mod.shape[0]) // tm if not per_token else 1
    row = lambda i: (i, 0)
    const = lambda i: (0, 0)
    blk = lambda w: pl.BlockSpec((tm, w), row)
    full = lambda a: pl.BlockSpec(a.shape, const)
    ms = lambda c: _mod_spec(per_token, tm, tiles_per_mod, c)
    weights = (p["w_glu_a"], p["w_glu_b"], p["w_sb_out"], p["w_out"], p["w_router_t"], p["router_bias"])
    return pl.pallas_call(
        _post_kernel,
        out_shape=(jax.ShapeDtypeStruct((n, D_MODEL), F32),
                   jax.ShapeDtypeStruct((n, D_MODEL), BF16),
                   jax.ShapeDtypeStruct((N_EXPERTS, n), F32)),
        grid=(n // tm,),
        in_specs=[blk(D_MODEL), blk(512), blk(512), blk(1024), blk(1024), ms(2), ms(3), ms(4),
                  pl.BlockSpec((1, D_MODEL), const)] + [full(w) for w in weights],
        out_specs=(blk(D_MODEL), blk(D_MODEL), pl.BlockSpec((N_EXPERTS, tm), lambda i: (0, i))),
        compiler_params=_cparams(("parallel",)),
        name="post_mixer",
    )(x, g, o, sga, sgb, mod, mod, mod, p["norm2_g"], *weights)


def _moe_kernel(h_ref, gates_ref, x1_ref, gate2_ref, wgs_ref, wus_ref, wds_ref,
                wg_ref, wu_ref, wd_ref, y_ref, acc_ref):
    j = pl.program_id(1)
    h = h_ref[...]

    @pl.when(j == 0)
    def _():
        act = _silu(_dot(h, wgs_ref[...])) * _dot(h, wus_ref[...])
        acc_ref[...] = _dot(act.astype(BF16), wds_ref[...])

    act = _silu(_dot(h, wg_ref[...])) * _dot(h, wu_ref[...])
    gates = gates_ref[...]
    lane = lax.broadcasted_iota(jnp.int32, gates.shape, 1)
    scaled = []
    for s in range(EXPERTS_PER_STEP):
        gcol = jnp.sum(jnp.where(lane == j * EXPERTS_PER_STEP + s, gates, 0.0), axis=1, keepdims=True)
        scaled.append((act[:, s * EXPERT_HIDDEN:(s + 1) * EXPERT_HIDDEN] * gcol).astype(BF16))
    acc_ref[...] += _dot(jnp.concatenate(scaled, axis=1), wd_ref[...])

    @pl.when(j == pl.num_programs(1) - 1)
    def _():
        y_ref[...] = x1_ref[...] + gate2_ref[...] * acc_ref[...]


def _side_by_side(w):
    groups = N_EXPERTS // EXPERTS_PER_STEP
    w = w.astype(BF16).reshape(groups, EXPERTS_PER_STEP, D_MODEL, EXPERT_HIDDEN)
    return w.transpose(0, 2, 1, 3).reshape(groups, D_MODEL, EXPERTS_PER_STEP * EXPERT_HIDDEN)


def _moe(h2, gates, x1, mod, per_token, tm, p):
    n = h2.shape[0]
    tiles_per_mod = (n // mod.shape[0]) // tm if not per_token else 1
    row = lambda i, e: (i, 0)
    const = lambda i, e: (0, 0)
    exp = lambda i, e: (e, 0, 0)
    if per_token:
        gate2_spec = pl.BlockSpec((tm, D_MODEL), lambda i, e: (i, 5))
    else:
        gate2_spec = pl.BlockSpec((None, 1, D_MODEL), lambda i, e: (i // tiles_per_mod, 0, 5))
    wide = EXPERTS_PER_STEP * EXPERT_HIDDEN
    return pl.pallas_call(
        _moe_kernel,
        out_shape=jax.ShapeDtypeStruct((n, D_MODEL), F32),
        grid=(n // tm, N_EXPERTS // EXPERTS_PER_STEP),
        in_specs=[pl.BlockSpec((tm, D_MODEL), row),
                  pl.BlockSpec((tm, N_EXPERTS), row),
                  pl.BlockSpec((tm, D_MODEL), row),
                  gate2_spec,
                  pl.BlockSpec((D_MODEL, EXPERT_HIDDEN), const),
                  pl.BlockSpec((D_MODEL, EXPERT_HIDDEN), const),
                  pl.BlockSpec((EXPERT_HIDDEN, D_MODEL), const),
                  pl.BlockSpec((None, D_MODEL, wide), exp),
                  pl.BlockSpec((None, D_MODEL, wide), exp),
                  pl.BlockSpec((None, wide, D_MODEL), exp)],
        out_specs=pl.BlockSpec((tm, D_MODEL), row),
        scratch_shapes=[pltpu.VMEM((tm, D_MODEL), F32)],
        compiler_params=_cparams(("parallel", "arbitrary")),
        name="moe",
    )(h2, gates, x1, mod, p["w_gate_s"], p["w_up_s"], p["w_down_s"], p["w_gate_e"], p["w_up_e"], p["w_down_e"])


def _layer(x_prompt, x_sample, cache_k, cache_v, s0_re, s0_im, page_table, c_prompt, c_sample, p):
    batch, t_len, d = x_prompt.shape
    n_s = x_sample.shape[0]
    n_p = batch * t_len

    c_all = jnp.concatenate([c_prompt, c_sample], axis=0)
    pad = (-c_all.shape[0]) % 8
    c_all = jnp.pad(c_all, ((0, pad), (0, 0)))
    mod = _ada_mod(c_all, p["w_ada"], p["b_ada"])
    mod_p = mod[:batch].reshape(batch, 1, 6 * d)
    mod_s = mod[batch:batch + n_s]

    a_re, a_im, bb_re, bb_im = _ssm_prep(p["lam_re"], p["lam_im"], p["log_dt"], p["b_re"], p["b_im"])
    ssm = (a_re, a_im, bb_re, bb_im, _blockdiag_out(p["c_re"]), _blockdiag_out(-p["c_im"]),
           p["d_skip"].astype(F32).reshape(1, SSM_WIDTH))

    xp = x_prompt.reshape(n_p, d)
    xs = x_sample.reshape(n_s, d)
    proj_p = _in_proj(xp, mod_p, False, 512, t_len, p["norm1_g"], p["w_in"], p["q_norm_g"], p["k_norm_g"])
    proj_s = _in_proj(xs, mod_s, True, n_s, n_s, p["norm1_g"], p["w_in"], p["q_norm_g"], p["k_norm_g"])
    u_p, q_p, k_p, kb_p, v_p, vb_p, sga_p, sgb_p = proj_p
    u_s, q_s, k_s, _, v_s, _, sga_s, sgb_s = proj_s

    g_p, st_re_p, st_im_p = _ssm_prompt(u_p.reshape(batch, t_len, SSM_WIDTH), ssm)
    g_p = g_p.reshape(n_p, SSM_WIDTH)
    g_s, st_re_s, st_im_s = _ssm_sample(u_s, s0_re.reshape(n_s, SSM_FLAT), s0_im.reshape(n_s, SSM_FLAT), ssm)

    o_p = _attn_prompt(q_p, kb_p, vb_p, p["sb_bias"], batch, t_len)
    o_s = _attn_sample(q_s, cache_k, cache_v, page_table, p["sb_bias"])

    x1_p, h2_p, gates_p = _post_mixer(xp, g_p, o_p, sga_p, sgb_p, mod_p, False, 512, p)
    x1_s, h2_s, gates_s = _post_mixer(xs, g_s, o_s, sga_s, sgb_s, mod_s, True, n_s, p)

    y_p = _moe(h2_p, gates_p.T, x1_p, mod_p, False, 1024, p)
    y_s = _moe(h2_s, gates_s.T, x1_s, mod_s, True, n_s, p)

    hd = (SB_HEADS, SB_HEAD_DIM)
    gs = (SSM_GROUPS, SSM_STATE)

    def kv_out(a_t, n_seq, seq_len):
        return a_t.reshape(n_seq, *hd, seq_len).transpose(0, 3, 1, 2)

    return (y_p.reshape(batch, t_len, d), y_s.reshape(n_s, 1, d),
            kv_out(k_p, batch, t_len), kv_out(v_p, batch, t_len),
            st_re_p.reshape(batch, *gs), st_im_p.reshape(batch, *gs),
            kv_out(k_s, 1, n_s).reshape(n_s, 1, *hd), kv_out(v_s, 1, n_s).reshape(n_s, 1, *hd),
            st_re_s.reshape(n_s, *gs), st_im_s.reshape(n_s, *gs))


def kernel(x_prompt, x_sample, cache_k, cache_v, state_ssm_re, state_ssm_im, page_table, c_prompt, c_sample,
           w_ada, b_ada, norm1_g, norm2_g, w_in, q_norm_g, k_norm_g, sb_bias, ssm_lambda_re, ssm_lambda_im,
           ssm_log_dt, ssm_b_re, ssm_b_im, ssm_c_re, ssm_c_im, ssm_d, w_glu_a, w_glu_b, w_sb_out, w_out,
           w_router, router_bias, w_gate_e, w_up_e, w_down_e, w_gate_s, w_up_s, w_down_s):
    depth = w_in.shape[0]
    assert depth == 1, "single-layer step"
    l = 0
    p = dict(
        w_ada=w_ada[l], b_ada=b_ada[l],
        norm1_g=norm1_g[l], norm2_g=norm2_g[l].reshape(1, D_MODEL),
        w_in=w_in[l].astype(BF16), q_norm_g=q_norm_g[l], k_norm_g=k_norm_g[l], sb_bias=sb_bias[l],
        lam_re=ssm_lambda_re[l], lam_im=ssm_lambda_im[l], log_dt=ssm_log_dt[l],
        b_re=ssm_b_re[l], b_im=ssm_b_im[l], c_re=ssm_c_re[l], c_im=ssm_c_im[l], d_skip=ssm_d[l],
        w_glu_a=w_glu_a[l].astype(BF16), w_glu_b=w_glu_b[l].astype(BF16),
        w_sb_out=w_sb_out[l].astype(BF16), w_out=w_out[l].astype(BF16),
        w_router_t=w_router[l].T, router_bias=router_bias[l].reshape(N_EXPERTS, 1),
        w_gate_e=_side_by_side(w_gate_e[l]), w_up_e=_side_by_side(w_up_e[l]),
        w_down_e=w_down_e[l].astype(BF16).reshape(N_EXPERTS // EXPERTS_PER_STEP,
                                                   EXPERTS_PER_STEP * EXPERT_HIDDEN, D_MODEL),
        w_gate_s=w_gate_s[l].astype(BF16), w_up_s=w_up_s[l].astype(BF16), w_down_s=w_down_s[l].astype(BF16),
    )
    outs = _layer(x_prompt, x_sample, cache_k[l], cache_v[l], state_ssm_re[l], state_ssm_im[l],
                  page_table, c_prompt, c_sample, p)
    y_p, y_s = outs[0], outs[1]
    return (y_p, y_s) + tuple(o[None] for o in outs[2:])
```

```python
import functools
import math

import jax
import jax.numpy as jnp
from jax import lax
from jax.experimental import pallas as pl
from jax.experimental.pallas import tpu as pltpu

F32 = jnp.float32
BF16 = jnp.bfloat16

D_MODEL = 1024
SSM_WIDTH = 512
SSM_GROUP = 16
SSM_GROUPS = 32
SSM_STATE = 64
SSM_FLAT = SSM_GROUPS * SSM_STATE
EIG_CLIP = -1e-4
SB_HEADS = 8
SB_HEAD_DIM = 64
SB_WIDTH = 512
PAGE_SIZE = 128
N_EXPERTS = 64
TOP_K = 8
N_GROUPS = 8
TOPK_GROUPS = 4
GROUP_SIZE = N_EXPERTS // N_GROUPS
EXPERT_HIDDEN = 256
EXPERTS_PER_STEP = 4
ROUTED_SCALE = 2.5
RMS_EPS = 1e-6

LOG2E = math.log2(math.e)
LOGIT_SCALE_LOG2 = SB_HEAD_DIM ** -0.5 * LOG2E
HEADS_PER_STEP = 4
HEAD_LANES = HEADS_PER_STEP * SB_HEAD_DIM
VMEM_LIMIT = 48 * 1024 * 1024


def _cparams(sem):
    return pltpu.CompilerParams(dimension_semantics=sem, vmem_limit_bytes=VMEM_LIMIT)


def _dot(a, b):
    return jnp.dot(a, b, preferred_element_type=F32)


def _dot_t(a, b):
    return lax.dot_general(a, b, (((1,), (1,)), ((), ())), preferred_element_type=F32)


def _split_bf16(x):
    hi = x.astype(BF16)
    lo = (x - hi.astype(F32)).astype(BF16)
    return hi, lo


def _silu(x):
    return x * jax.nn.sigmoid(x)


def _gelu_tanh(x):
    c = math.sqrt(2.0 / math.pi)
    return 0.5 * x * (1.0 + jnp.tanh(c * (x + 0.044715 * (x * x * x))))


def _ada_kernel(c_ref, w_ref, b_ref, o_ref):
    s = _silu(c_ref[...])
    s_hi, s_lo = _split_bf16(s)
    w_hi, w_lo = _split_bf16(w_ref[...])
    o_ref[...] = _dot(s_hi, w_hi) + _dot(s_hi, w_lo) + _dot(s_lo, w_hi) + b_ref[...]


def _ada_mod(c, w_ada, b_ada):
    rows, d = c.shape
    cols = w_ada.shape[1]
    tn = 1024
    return pl.pallas_call(
        _ada_kernel,
        out_shape=jax.ShapeDtypeStruct((rows, cols), F32),
        grid=(cols // tn,),
        in_specs=[pl.BlockSpec((rows, d), lambda j: (0, 0)),
                  pl.BlockSpec((d, tn), lambda j: (0, j)),
                  pl.BlockSpec((1, tn), lambda j: (0, j))],
        out_specs=pl.BlockSpec((rows, tn), lambda j: (0, j)),
        compiler_params=_cparams(("parallel",)),
        name="ada_mod",
    )(c, w_ada, b_ada.reshape(1, cols))


def _ssm_prep_kernel(lre_ref, lim_ref, ldt_ref, bre_ref, bim_ref,
                     are_ref, aim_ref, bbre_ref, bbim_ref):
    dt = jnp.exp(ldt_ref[...])
    lre = jnp.minimum(lre_ref[...], EIG_CLIP)
    lim = lim_ref[...]
    mag = jnp.exp(lre * dt)
    a_re = mag * jnp.cos(lim * dt)
    a_im = mag * jnp.sin(lim * dt)
    den = lre * lre + lim * lim
    f_re = ((a_re - 1.0) * lre + a_im * lim) / den
    f_im = (a_im * lre - (a_re - 1.0) * lim) / den
    br = bre_ref[...]
    bi = bim_ref[...]
    are_ref[...] = a_re
    aim_ref[...] = a_im
    bbre_ref[...] = f_re * br - f_im * bi
    bbim_ref[...] = f_re * bi + f_im * br


def _ssm_prep(lam_re, lam_im, log_dt, b_re, b_im):
    rep = lambda a: jnp.repeat(a.astype(F32), SSM_GROUP, axis=0)
    lre, lim = rep(lam_re), rep(lam_im)
    ldt = rep(jnp.broadcast_to(log_dt.astype(F32)[:, None], (SSM_GROUPS, SSM_STATE)))
    tr = lambda b: b.astype(F32).transpose(0, 2, 1).reshape(SSM_WIDTH, SSM_STATE)
    shp = jax.ShapeDtypeStruct((SSM_WIDTH, SSM_STATE), F32)
    a_re, a_im, bb_re, bb_im = pl.pallas_call(
        _ssm_prep_kernel, out_shape=(shp, shp, shp, shp), name="ssm_prep",
    )(lre, lim, ldt, tr(b_re), tr(b_im))
    eye = jnp.eye(SSM_GROUPS, dtype=F32)

    def blockdiag_in(bb):
        bb4 = bb.reshape(SSM_GROUPS, SSM_GROUP, 1, SSM_STATE) * eye.reshape(SSM_GROUPS, 1, SSM_GROUPS, 1)
        return bb4.reshape(SSM_WIDTH, SSM_FLAT).astype(BF16)

    a_re = a_re[::SSM_GROUP].reshape(1, SSM_FLAT)
    a_im = a_im[::SSM_GROUP].reshape(1, SSM_FLAT)
    return a_re, a_im, blockdiag_in(bb_re), blockdiag_in(bb_im)


def _blockdiag_out(c):
    eye = jnp.eye(SSM_GROUPS, dtype=F32)
    c4 = c.astype(F32).transpose(0, 2, 1).reshape(SSM_GROUPS, SSM_STATE, 1, SSM_GROUP)
    return (c4 * eye.reshape(SSM_GROUPS, 1, SSM_GROUPS, 1)).reshape(SSM_FLAT, SSM_WIDTH).astype(BF16)


def _inproj_kernel(x_ref, shift_ref, scale_ref, g1_ref, w_ref, qg_ref, kg_ref, hm_ref,
                   u_ref, q_ref, k_ref, kb_ref, v_ref, vb_ref, sga_ref, sgb_ref):
    x = x_ref[...]
    r = lax.rsqrt(jnp.mean(x * x, axis=-1, keepdims=True) + RMS_EPS)
    h = (x * r) * g1_ref[...]
    h = h * (1.0 + scale_ref[...]) + shift_ref[...]
    hb = h.astype(BF16)
    hm = hm_ref[...]

    def head_norm(t, g):
        ms = _dot((t * t).astype(BF16), hm)
        return (t * lax.rsqrt(ms + RMS_EPS)) * g

    u_ref[...] = _dot(hb, w_ref[:, 0:512])
    q = head_norm(_dot(hb, w_ref[:, 512:1024]), qg_ref[...])
    q_ref[...] = (q * LOGIT_SCALE_LOG2).astype(BF16)
    k = head_norm(_dot(hb, w_ref[:, 1024:1536]), kg_ref[...])
    k_ref[...] = k.T
    kb_ref[...] = k.astype(BF16)
    v = _dot(hb, w_ref[:, 1536:2048])
    v_ref[...] = v.T
    vb_ref[...] = v.astype(BF16)
    sga_ref[...] = jax.nn.sigmoid(_dot(hb, w_ref[:, 2048:3072])).astype(BF16)
    sgb_ref[...] = jax.nn.sigmoid(_dot(hb, w_ref[:, 3072:4096])).astype(BF16)


def _head_mean_matrix():
    idx = jnp.arange(SB_WIDTH) // SB_HEAD_DIM
    return ((idx[:, None] == idx[None, :]).astype(F32) / SB_HEAD_DIM).astype(BF16)


def _mod_spec(per_token, tm, rows_per_mod, chunk):
    if per_token:
        return pl.BlockSpec((tm, D_MODEL), lambda i: (i, chunk))
    return pl.BlockSpec((None, 1, D_MODEL), lambda i: (i // rows_per_mod, 0, chunk))


def _in_proj(x, mod, per_token, tm, seq_len, norm1_g, w_in_b, q_g, k_g):
    n = x.shape[0]
    tiles_per_mod = (n // mod.shape[0]) // tm if not per_token else 1
    tiles_per_seq = seq_len // tm
    row = lambda i: (i, 0)
    const = lambda i: (0, 0)
    f32o = lambda w: jax.ShapeDtypeStruct((n, w), F32)
    b16o = lambda w: jax.ShapeDtypeStruct((n, w), BF16)
    kvo = jax.ShapeDtypeStruct((n // seq_len, SB_WIDTH, seq_len), F32)
    kv_blk = pl.BlockSpec((None, SB_WIDTH, tm), lambda i: (i // tiles_per_seq, 0, i % tiles_per_seq))
    blk = lambda w: pl.BlockSpec((tm, w), row)
    return pl.pallas_call(
        _inproj_kernel,
        out_shape=(f32o(512), b16o(512), kvo, b16o(512), kvo, b16o(512), b16o(1024), b16o(1024)),
        grid=(n // tm,),
        in_specs=[blk(D_MODEL),
                  _mod_spec(per_token, tm, tiles_per_mod, 0),
                  _mod_spec(per_token, tm, tiles_per_mod, 1),
                  pl.BlockSpec((1, D_MODEL), const),
                  pl.BlockSpec(w_in_b.shape, const),
                  pl.BlockSpec((1, SB_WIDTH), const),
                  pl.BlockSpec((1, SB_WIDTH), const),
                  pl.BlockSpec((SB_WIDTH, SB_WIDTH), const)],
        out_specs=(blk(512), blk(512), kv_blk, blk(512), kv_blk, blk(512), blk(1024), blk(1024)),
        compiler_params=_cparams(("parallel",)),
        name="in_proj",
    )(x, mod, mod, norm1_g.reshape(1, D_MODEL), w_in_b,
      jnp.tile(q_g, SB_HEADS).reshape(1, SB_WIDTH), jnp.tile(k_g, SB_HEADS).reshape(1, SB_WIDTH),
      _head_mean_matrix())


def _ssm_drive(ub, bb_ref):
    half_c, half_s = SSM_WIDTH // 2, SSM_FLAT // 2
    lo = _dot(ub[:, :half_c], bb_ref[:half_c, :half_s])
    hi = _dot(ub[:, half_c:], bb_ref[half_c:, half_s:])
    return lo, hi


def _ssm_readout(xre, xim, ccre_ref, ccim_ref, d_ref, u):
    half_c, half_s = SSM_WIDTH // 2, SSM_FLAT // 2
    xr = xre.astype(BF16)
    xi = xim.astype(BF16)
    y_lo = _dot(xr[:, :half_s], ccre_ref[:half_s, :half_c]) + _dot(xi[:, :half_s], ccim_ref[:half_s, :half_c])
    y_hi = _dot(xr[:, half_s:], ccre_ref[half_s:, half_c:]) + _dot(xi[:, half_s:], ccim_ref[half_s:, half_c:])
    y = jnp.concatenate([y_lo, y_hi], axis=1) + d_ref[...] * u
    return _gelu_tanh(y)


def _ssm_prompt_kernel(u_ref, bbre_ref, bbim_ref, are_ref, aim_ref, ccre_ref, ccim_ref, d_ref,
                       g_ref, sre_ref, sim_ref, ubuf, gbuf, xre, xim, cre, cim, *, lane_chunk):
    batch, tt, _ = u_ref.shape
    rows = tt * batch
    half_s = SSM_FLAT // 2

    @pl.when(pl.program_id(0) == 0)
    def _():
        cre[...] = jnp.zeros_like(cre)
        cim[...] = jnp.zeros_like(cim)

    n_lane_tiles = SSM_WIDTH // 128
    for b in range(batch):
        for c in range(n_lane_tiles):
            ubuf[c, pl.ds(b, tt, stride=batch), :] = u_ref[b, :, c * 128:(c + 1) * 128]
    u = jnp.concatenate([ubuf[c] for c in range(n_lane_tiles)], axis=1)
    ub = u.astype(BF16)
    lo, hi = _ssm_drive(ub, bbre_ref)
    xre[:, :half_s] = lo
    xre[:, half_s:] = hi
    lo, hi = _ssm_drive(ub, bbim_ref)
    xim[:, :half_s] = lo
    xim[:, half_s:] = hi

    first_step = lax.broadcasted_iota(jnp.int32, (8, lane_chunk), 0) < batch
    for c0 in range(0, SSM_FLAT, lane_chunk):
        cols = pl.ds(c0, lane_chunk)
        ar = jnp.broadcast_to(are_ref[:, cols], (8, lane_chunk))
        ai = jnp.broadcast_to(aim_ref[:, cols], (8, lane_chunk))

        def step(i, carry, cols=cols, ar=ar, ai=ai):
            c_r, c_i = carry
            r0 = pl.multiple_of(i * 8, 8)
            br = xre[pl.ds(r0, 8), cols]
            bi = xim[pl.ds(r0, 8), cols]
            p_r = pltpu.roll(c_r, batch, 0)
            p_i = pltpu.roll(c_i, batch, 0)
            y_r = ar * p_r - ai * p_i + br
            y_i = ar * p_i + ai * p_r + bi
            q_r = pltpu.roll(y_r, batch, 0)
            q_i = pltpu.roll(y_i, batch, 0)
            z_r = ar * q_r - ai * q_i + br
            z_i = ar * q_i + ai * q_r + bi
            xre[pl.ds(r0, 8), cols] = jnp.where(first_step, y_r, z_r)
            xim[pl.ds(r0, 8), cols] = jnp.where(first_step, y_i, z_i)
            return z_r, z_i

        c_r, c_i = lax.fori_loop(0, rows // 8, step, (cre[:, cols], cim[:, cols]))
        cre[:, cols] = c_r
        cim[:, cols] = c_i

    g = _ssm_readout(xre[...], xim[...], ccre_ref, ccim_ref, d_ref, u)
    for c in range(n_lane_tiles):
        gbuf[c] = g[:, c * 128:(c + 1) * 128]
    for b in range(batch):
        g_ref[b] = jnp.concatenate([gbuf[c, pl.ds(b, tt, stride=batch), :] for c in range(n_lane_tiles)],
                                   axis=1).astype(BF16)
    sre_ref[...] = cre[...]
    sim_ref[...] = cim[...]


def _ssm_prompt(u, ssm, tt=128, lane_chunk=512):
    a_re, a_im, bb_re, bb_im, cc_re, cc_im, d_skip = ssm
    batch, t_len, _ = u.shape
    assert 2 * batch == 8, "one sublane tile must hold exactly two time steps"
    rows = tt * batch
    const = lambda j: (0, 0)
    full = lambda a: pl.BlockSpec(a.shape, const)
    seq_blk = pl.BlockSpec((batch, tt, SSM_WIDTH), lambda j: (0, j, 0))
    g, s_re, s_im = pl.pallas_call(
        functools.partial(_ssm_prompt_kernel, lane_chunk=lane_chunk),
        out_shape=(jax.ShapeDtypeStruct((batch, t_len, SSM_WIDTH), BF16),
                   jax.ShapeDtypeStruct((8, SSM_FLAT), F32),
                   jax.ShapeDtypeStruct((8, SSM_FLAT), F32)),
        grid=(t_len // tt,),
        in_specs=[seq_blk,
                  full(bb_re), full(bb_im), full(a_re), full(a_im), full(cc_re), full(cc_im), full(d_skip)],
        out_specs=(seq_blk, pl.BlockSpec((8, SSM_FLAT), const), pl.BlockSpec((8, SSM_FLAT), const)),
        scratch_shapes=[pltpu.VMEM((SSM_WIDTH // 128, rows, 128), F32),
                        pltpu.VMEM((SSM_WIDTH // 128, rows, 128), F32),
                        pltpu.VMEM((rows, SSM_FLAT), F32), pltpu.VMEM((rows, SSM_FLAT), F32),
                        pltpu.VMEM((8, SSM_FLAT), F32), pltpu.VMEM((8, SSM_FLAT), F32)],
        compiler_params=_cparams(("arbitrary",)),
        name="ssm_prompt",
    )(u, bb_re, bb_im, a_re, a_im, cc_re, cc_im, d_skip)
    return g, s_re[batch:], s_im[batch:]


def _ssm_sample_kernel(u_ref, s0re_ref, s0im_ref, bbre_ref, bbim_ref, are_ref, aim_ref,
                       ccre_ref, ccim_ref, d_ref, g_ref, sre_ref, sim_ref):
    u = u_ref[...]
    ub = u.astype(BF16)
    ar, ai = are_ref[...], aim_ref[...]
    s_r, s_i = s0re_ref[...], s0im_ref[...]
    lo, hi = _ssm_drive(ub, bbre_ref)
    x_r = ar * s_r - ai * s_i + jnp.concatenate([lo, hi], axis=1)
    lo, hi = _ssm_drive(ub, bbim_ref)
    x_i = ar * s_i + ai * s_r + jnp.concatenate([lo, hi], axis=1)
    sre_ref[...] = x_r
    sim_ref[...] = x_i
    g_ref[...] = _ssm_readout(x_r, x_i, ccre_ref, ccim_ref, d_ref, u).astype(BF16)


def _ssm_sample(u, s_re, s_im, ssm):
    a_re, a_im, bb_re, bb_im, cc_re, cc_im, d_skip = ssm
    n = u.shape[0]
    return pl.pallas_call(
        _ssm_sample_kernel,
        out_shape=(jax.ShapeDtypeStruct((n, SSM_WIDTH), BF16),
                   jax.ShapeDtypeStruct((n, SSM_FLAT), F32),
                   jax.ShapeDtypeStruct((n, SSM_FLAT), F32)),
        compiler_params=pltpu.CompilerParams(vmem_limit_bytes=VMEM_LIMIT),
        name="ssm_sample",
    )(u, s_re, s_im, bb_re, bb_im, a_re, a_im, cc_re, cc_im, d_skip)


def _log_sigmoid_pair(z2):
    sign = jnp.uint32(0x80000000)
    neg_abs = lax.bitcast_convert_type(lax.bitcast_convert_type(z2, jnp.uint32) | sign, F32)
    soft = jnp.log2(1.0 + jnp.exp2(neg_abs))
    lsig = jnp.minimum(z2, 0.0) - soft
    return lsig, lsig - z2


def _attn_prompt_kernel(bias_ref, q_ref, k_ref, v_ref, tri_ref, o_ref, acc_ref, run_ref, *, tq, tk):
    qi = pl.program_id(1)
    nh = HEADS_PER_STEP
    ngrp = SB_HEADS // nh
    lane_head = lax.broadcasted_iota(jnp.int32, (1, HEAD_LANES), 1) // SB_HEAD_DIM
    head_mask = [(lane_head == h).astype(F32).astype(BF16) for h in range(nh)]
    qs, bias = [], []
    for g in range(ngrp):
        q = q_ref[:, g * HEAD_LANES:(g + 1) * HEAD_LANES]
        qs.append(jnp.concatenate([q * head_mask[h] for h in range(nh)], axis=0))
        bias.append([bias_ref[g * nh + h] for h in range(nh)])
    tri = tri_ref[...]
    acc_ref[...] = jnp.zeros_like(acc_ref)
    run_ref[...] = jnp.zeros_like(run_ref)

    def block(js, masked):
        if masked:
            k0 = pl.multiple_of(js[0] * tk, tk)
            kpos = k0 + lax.broadcasted_iota(jnp.int32, (nh * tq, tk), 1)
            qpos = qi * tq + (lax.broadcasted_iota(jnp.int32, (nh * tq, tk), 0) & (tq - 1))
            valid = kpos < qpos
        for g in range(ngrp):
            lanes = slice(g * HEAD_LANES, (g + 1) * HEAD_LANES)
            run = run_ref[g]
            pv = None
            for j in js:
                k0 = pl.multiple_of(j * tk, tk)
                zs = _dot_t(qs[g], k_ref[pl.ds(k0, tk), lanes])
                lsig_parts, stay_parts = [], []
                for h in range(nh):
                    lsig, stay = _log_sigmoid_pair(zs[h * tq:(h + 1) * tq] + bias[g][h])
                    lsig_parts.append(lsig)
                    stay_parts.append(stay)
                lsig = jnp.concatenate(lsig_parts, axis=0)
                stay = jnp.concatenate(stay_parts, axis=0)
                if masked:
                    stay = jnp.where(valid, stay, 0.0)
                after = _dot(stay.astype(BF16), tri)
                w = jnp.exp2(lsig + after + run)
                if masked:
                    w = jnp.where(valid, w, 0.0)
                part = _dot(w.astype(BF16), v_ref[pl.ds(k0, tk), lanes])
                pv = part if pv is None else pv + part
                run = run + (after[:, 0:1] + stay[:, 0:1])
            acc_ref[g] += pv
            run_ref[g] = run

    j_diag = (qi * tq) // tk
    block((j_diag,), True)

    @pl.when(j_diag % 2 == 1)
    def _():
        block((j_diag - 1,), False)

    def body(jj, carry):
        j = 2 * (j_diag // 2 - jj) - 1
        block((j, j - 1), False)
        return carry

    lax.fori_loop(0, j_diag // 2, body, 0)
    for g in range(ngrp):
        acc = acc_ref[g]
        out = jnp.zeros((tq, HEAD_LANES), F32)
        for h in range(nh):
            out = jnp.where(lane_head == h, acc[h * tq:(h + 1) * tq], out)
        o_ref[:, g * HEAD_LANES:(g + 1) * HEAD_LANES] = out.astype(BF16)


def _strict_upper(tk):
    idx = jnp.arange(tk)
    return (idx[:, None] > idx[None, :]).astype(BF16)


def _attn_prompt(q, kb, vb, sb_bias, batch, t_len, tq=256, tk=256):
    assert tq & (tq - 1) == 0 and tq <= tk, "one masked key block must cover the diagonal"
    n = q.shape[0]
    nq = t_len // tq
    ngrp = SB_HEADS // HEADS_PER_STEP
    return pl.pallas_call(
        functools.partial(_attn_prompt_kernel, tq=tq, tk=tk),
        out_shape=jax.ShapeDtypeStruct((n, SB_WIDTH), BF16),
        grid=(batch, nq),
        in_specs=[pl.BlockSpec(memory_space=pltpu.SMEM),
                  pl.BlockSpec((tq, SB_WIDTH), lambda b, i: (b * nq + i, 0)),
                  pl.BlockSpec((t_len, SB_WIDTH), lambda b, i: (b, 0)),
                  pl.BlockSpec((t_len, SB_WIDTH), lambda b, i: (b, 0)),
                  pl.BlockSpec((tk, tk), lambda b, i: (0, 0))],
        out_specs=pl.BlockSpec((tq, SB_WIDTH), lambda b, i: (b * nq + i, 0)),
        scratch_shapes=[pltpu.VMEM((ngrp, HEADS_PER_STEP * tq, HEAD_LANES), F32),
                        pltpu.VMEM((ngrp, HEADS_PER_STEP * tq, 1), F32)],
        compiler_params=_cparams(("parallel", "arbitrary")),
        name="attn_prompt",
    )(sb_bias.astype(F32) * LOG2E, q, kb, vb, _strict_upper(tk))


def _attn_sample_kernel(pt_ref, q_ref, bias_ref, tri_ref, later_ref, *refs, n_pages):
    del pt_ref
    k_refs = refs[:n_pages]
    v_refs = refs[n_pages:2 * n_pages]
    o_ref = refs[2 * n_pages]
    hd = (SB_HEADS, SB_HEAD_DIM, PAGE_SIZE)
    q = q_ref[0].astype(F32)
    q_col = jnp.broadcast_to(q, (PAGE_SIZE, SB_WIDTH)).T
    q3 = q_col.reshape(hd)
    z = jnp.concatenate([jnp.sum(k_refs[p][0].reshape(hd) * q3, axis=1) for p in range(n_pages)], axis=0)
    lsig, stay = _log_sigmoid_pair(z + bias_ref[...])
    after = _dot(stay.astype(BF16), tri_ref[...])
    total = jnp.broadcast_to(after[:, 0:1] + stay[:, 0:1], stay.shape)
    t_hi, t_lo = _split_bf16(total)
    later = later_ref[...]
    run = _dot(later, t_hi) + _dot(later, t_lo)
    w = jnp.exp2(lsig + after + run)
    acc = [jnp.zeros((SB_HEAD_DIM, PAGE_SIZE), F32) for _ in range(SB_HEADS)]
    for p in range(n_pages):
        for h in range(SB_HEADS):
            r = p * SB_HEADS + h
            acc[h] = acc[h] + v_refs[p][0, h * SB_HEAD_DIM:(h + 1) * SB_HEAD_DIM, :] * w[r:r + 1, :]
    a_hi, a_lo = _split_bf16(jnp.concatenate(acc, axis=0))
    ones = jnp.ones((8, PAGE_SIZE), BF16)
    o_ref[0] = (_dot_t(ones, a_hi) + _dot_t(ones, a_lo))[0:1].astype(BF16)


def _attn_sample(q, cache_k, cache_v, page_table, sb_bias):
    n, n_pages = page_table.shape
    n_pool = cache_k.shape[0]
    ck = cache_k.transpose(0, 2, 3, 1).reshape(n_pool, SB_WIDTH, PAGE_SIZE)
    cv = cache_v.transpose(0, 2, 3, 1).reshape(n_pool, SB_WIDTH, PAGE_SIZE)
    page_spec = lambda p: pl.BlockSpec((1, SB_WIDTH, PAGE_SIZE), lambda i, pt, p=p: (pt[i * n_pages + p], 0, 0))
    tok_spec = pl.BlockSpec((1, 1, SB_WIDTH), lambda i, pt: (i, 0, 0))
    rows = n_pages * SB_HEADS
    const = lambda shape: pl.BlockSpec(shape, lambda i, pt: (0, 0))
    r = jnp.arange(rows)
    later = ((r[:, None] % SB_HEADS == r[None, :] % SB_HEADS)
             & (r[None, :] // SB_HEADS > r[:, None] // SB_HEADS)).astype(BF16)
    out = pl.pallas_call(
        functools.partial(_attn_sample_kernel, n_pages=n_pages),
        out_shape=jax.ShapeDtypeStruct((n, 1, SB_WIDTH), BF16),
        grid_spec=pltpu.PrefetchScalarGridSpec(
            num_scalar_prefetch=1,
            grid=(n,),
            in_specs=[tok_spec, const((rows, 1)), const((PAGE_SIZE, PAGE_SIZE)), const((rows, rows))]
                     + [page_spec(p) for p in range(n_pages)] * 2,
            out_specs=tok_spec),
        compiler_params=_cparams(("parallel",)),
        name="attn_sample",
    )(page_table.reshape(-1).astype(jnp.int32), q.reshape(n, 1, SB_WIDTH),
      jnp.tile(sb_bias.astype(F32) * LOG2E, n_pages).reshape(rows, 1), _strict_upper(PAGE_SIZE), later,
      *([ck] * n_pages), *([cv] * n_pages))
    return out.reshape(n, SB_WIDTH)


def _first_index_of_max(vals, row_id, n_rows):
    m = jnp.max(vals, axis=0, keepdims=True)
    first = jnp.min(jnp.where(vals == m, row_id, n_rows), axis=0, keepdims=True)
    return m, first


def _route(logits_t, rbias):
    tm = logits_t.shape[1]
    scores = jax.nn.sigmoid(logits_t)
    biased = scores + rbias
    row8 = lax.broadcasted_iota(jnp.int32, (GROUP_SIZE, tm), 0)
    neg = jnp.float32(-jnp.inf)
    grp_score = []
    for g in range(N_GROUPS):
        blk = biased[g * GROUP_SIZE:(g + 1) * GROUP_SIZE]
        m1, first = _first_index_of_max(blk, row8, GROUP_SIZE)
        m2 = jnp.max(jnp.where(row8 == first, neg, blk), axis=0, keepdims=True)
        grp_score.append(m1 + m2)
    masked = []
    for g in range(N_GROUPS):
        beaten_by = jnp.zeros((1, tm), jnp.int32)
        for o in range(N_GROUPS):
            if o == g:
                continue
            wins = (grp_score[o] > grp_score[g]) if o > g else (grp_score[o] >= grp_score[g])
            beaten_by = beaten_by + wins.astype(jnp.int32)
        keep = beaten_by < TOPK_GROUPS
        masked.append(jnp.where(keep, biased[g * GROUP_SIZE:(g + 1) * GROUP_SIZE], neg))
    cur = jnp.concatenate(masked, axis=0)
    row = lax.broadcasted_iota(jnp.int32, (N_EXPERTS, tm), 0)
    chosen = jnp.zeros((N_EXPERTS, tm), jnp.bool_)
    for _ in range(TOP_K):
        _, first = _first_index_of_max(cur, row, N_EXPERTS)
        pick = row == first
        chosen = jnp.logical_or(chosen, pick)
        cur = jnp.where(pick, neg, cur)
    sel = jnp.where(chosen, scores, 0.0)
    return sel / jnp.sum(sel, axis=0, keepdims=True) * ROUTED_SCALE


def _post_kernel(x_ref, g_ref, o_ref, sga_ref, sgb_ref, gate1_ref, shift2_ref, scale2_ref, g2_ref,
                 wga_ref, wgb_ref, wsb_ref, wout_ref, wr_ref, rb_ref,
                 x1_ref, h2_ref, gates_ref):
    g = g_ref[...]
    branch_a = _dot(g, wga_ref[...]) * jax.nn.sigmoid(_dot(g, wgb_ref[...]))
    branch_b = _dot(o_ref[...], wsb_ref[...])
    merged = sga_ref[...].astype(F32) * branch_a + sgb_ref[...].astype(F32) * branch_b
    x1 = x_ref[...] + gate1_ref[...] * _dot(merged.astype(BF16), wout_ref[...])
    x1_ref[...] = x1
    r = lax.rsqrt(jnp.mean(x1 * x1, axis=-1, keepdims=True) + RMS_EPS)
    h2 = (x1 * r) * g2_ref[...]
    h2 = h2 * (1.0 + scale2_ref[...]) + shift2_ref[...]
    h2_ref[...] = h2.astype(BF16)
    h_hi, h_lo = _split_bf16(h2)
    w_hi, w_lo = _split_bf16(wr_ref[...])
    logits_t = _dot_t(w_hi, h_hi) + _dot_t(w_hi, h_lo) + _dot_t(w_lo, h_hi)
    gates_ref[...] = _route(logits_t, rb_ref[...])


def _post_mixer(x, g, o, sga, sgb, mod, per_token, tm, p):
    n = x.shape[0]
    tiles_per_mod = (n // mod.shape[0]) // tm if not per_token else 1
    row = lambda i: (i, 0)
    const = lambda i: (0, 0)
    blk = lambda w: pl.BlockSpec((tm, w), row)
    full = lambda a: pl.BlockSpec(a.shape, const)
    ms = lambda c: _mod_spec(per_token, tm, tiles_per_mod, c)
    weights = (p["w_glu_a"], p["w_glu_b"], p["w_sb_out"], p["w_out"], p["w_router_t"], p["router_bias"])
    return pl.pallas_call(
        _post_kernel,
        out_shape=(jax.ShapeDtypeStruct((n, D_MODEL), F32),
                   jax.ShapeDtypeStruct((n, D_MODEL), BF16),
                   jax.ShapeDtypeStruct((N_EXPERTS, n), F32)),
        grid=(n // tm,),
        in_specs=[blk(D_MODEL), blk(512), blk(512), blk(1024), blk(1024), ms(2), ms(3), ms(4),
                  pl.BlockSpec((1, D_MODEL), const)] + [full(w) for w in weights],
        out_specs=(blk(D_MODEL), blk(D_MODEL), pl.BlockSpec((N_EXPERTS, tm), lambda i: (0, i))),
        compiler_params=_cparams(("parallel",)),
        name="post_mixer",
    )(x, g, o, sga, sgb, mod, mod, mod, p["norm2_g"], *weights)


def _moe_kernel(h_ref, gates_ref, x1_ref, gate2_ref, wgs_ref, wus_ref, wds_ref,
                wg_ref, wu_ref, wd_ref, y_ref, acc_ref):
    j = pl.program_id(1)
    h = h_ref[...]

    @pl.when(j == 0)
    def _():
        act = _silu(_dot(h, wgs_ref[...])) * _dot(h, wus_ref[...])
        acc_ref[...] = _dot(act.astype(BF16), wds_ref[...])

    gates = gates_ref[...]
    lane = lax.broadcasted_iota(jnp.int32, gates.shape, 1)
    scaled = []
    for s in range(EXPERTS_PER_STEP):
        act = _silu(_dot(h, wg_ref[s])) * _dot(h, wu_ref[s])
        gcol = jnp.sum(jnp.where(lane == j * EXPERTS_PER_STEP + s, gates, 0.0), axis=1, keepdims=True)
        scaled.append((act * gcol).astype(BF16))
    acc_ref[...] += _dot(jnp.concatenate(scaled, axis=1), wd_ref[...])

    @pl.when(j == pl.num_programs(1) - 1)
    def _():
        y_ref[...] = x1_ref[...] + gate2_ref[...] * acc_ref[...]


def _moe(h2, gates, x1, mod, per_token, tm, p):
    n = h2.shape[0]
    tiles_per_mod = (n // mod.shape[0]) // tm if not per_token else 1
    row = lambda i, e: (i, 0)
    const = lambda i, e: (0, 0)
    exp = lambda i, e: (e, 0, 0)
    if per_token:
        gate2_spec = pl.BlockSpec((tm, D_MODEL), lambda i, e: (i, 5))
    else:
        gate2_spec = pl.BlockSpec((None, 1, D_MODEL), lambda i, e: (i // tiles_per_mod, 0, 5))
    wide = EXPERTS_PER_STEP * EXPERT_HIDDEN
    return pl.pallas_call(
        _moe_kernel,
        out_shape=jax.ShapeDtypeStruct((n, D_MODEL), F32),
        grid=(n // tm, N_EXPERTS // EXPERTS_PER_STEP),
        in_specs=[pl.BlockSpec((tm, D_MODEL), row),
                  pl.BlockSpec((tm, N_EXPERTS), row),
                  pl.BlockSpec((tm, D_MODEL), row),
                  gate2_spec,
                  pl.BlockSpec((D_MODEL, EXPERT_HIDDEN), const),
                  pl.BlockSpec((D_MODEL, EXPERT_HIDDEN), const),
                  pl.BlockSpec((EXPERT_HIDDEN, D_MODEL), const),
                  pl.BlockSpec((EXPERTS_PER_STEP, D_MODEL, EXPERT_HIDDEN), exp),
                  pl.BlockSpec((EXPERTS_PER_STEP, D_MODEL, EXPERT_HIDDEN), exp),
                  pl.BlockSpec((None, wide, D_MODEL), exp)],
        out_specs=pl.BlockSpec((tm, D_MODEL), row),
        scratch_shapes=[pltpu.VMEM((tm, D_MODEL), F32)],
        compiler_params=_cparams(("parallel", "arbitrary")),
        name="moe",
    )(h2, gates, x1, mod, p["w_gate_s"], p["w_up_s"], p["w_down_s"], p["w_gate_e"], p["w_up_e"], p["w_down_e"])


def _layer(x_prompt, x_sample, cache_k, cache_v, s0_re, s0_im, page_table, c_prompt, c_sample, p):
    batch, t_len, d = x_prompt.shape
    n_s = x_sample.shape[0]
    n_p = batch * t_len

    c_all = jnp.concatenate([c_prompt, c_sample], axis=0)
    pad = (-c_all.shape[0]) % 8
    c_all = jnp.pad(c_all, ((0, pad), (0, 0)))
    mod = _ada_mod(c_all, p["w_ada"], p["b_ada"])
    mod_p = mod[:batch].reshape(batch, 1, 6 * d)
    mod_s = mod[batch:batch + n_s]

    a_re, a_im, bb_re, bb_im = _ssm_prep(p["lam_re"], p["lam_im"], p["log_dt"], p["b_re"], p["b_im"])
    ssm = (a_re, a_im, bb_re, bb_im, _blockdiag_out(p["c_re"]), _blockdiag_out(-p["c_im"]),
           p["d_skip"].astype(F32).reshape(1, SSM_WIDTH))

    xp = x_prompt.reshape(n_p, d)
    xs = x_sample.reshape(n_s, d)
    proj_p = _in_proj(xp, mod_p, False, 512, t_len, p["norm1_g"], p["w_in"], p["q_norm_g"], p["k_norm_g"])
    proj_s = _in_proj(xs, mod_s, True, n_s, n_s, p["norm1_g"], p["w_in"], p["q_norm_g"], p["k_norm_g"])
    u_p, q_p, k_p, kb_p, v_p, vb_p, sga_p, sgb_p = proj_p
    u_s, q_s, k_s, _, v_s, _, sga_s, sgb_s = proj_s

    g_p, st_re_p, st_im_p = _ssm_prompt(u_p.reshape(batch, t_len, SSM_WIDTH), ssm)
    g_p = g_p.reshape(n_p, SSM_WIDTH)
    g_s, st_re_s, st_im_s = _ssm_sample(u_s, s0_re.reshape(n_s, SSM_FLAT), s0_im.reshape(n_s, SSM_FLAT), ssm)

    o_p = _attn_prompt(q_p, kb_p, vb_p, p["sb_bias"], batch, t_len)
    o_s = _attn_sample(q_s, cache_k, cache_v, page_table, p["sb_bias"])

    x1_p, h2_p, gates_p = _post_mixer(xp, g_p, o_p, sga_p, sgb_p, mod_p, False, 512, p)
    x1_s, h2_s, gates_s = _post_mixer(xs, g_s, o_s, sga_s, sgb_s, mod_s, True, n_s, p)

    y_p = _moe(h2_p, gates_p.T, x1_p, mod_p, False, 1024, p)
    y_s = _moe(h2_s, gates_s.T, x1_s, mod_s, True, n_s, p)

    hd = (SB_HEADS, SB_HEAD_DIM)
    gs = (SSM_GROUPS, SSM_STATE)

    def kv_out(a_t, n_seq, seq_len):
        return a_t.reshape(n_seq, *hd, seq_len).transpose(0, 3, 1, 2)

    return (y_p.reshape(batch, t_len, d), y_s.reshape(n_s, 1, d),
            kv_out(k_p, batch, t_len), kv_out(v_p, batch, t_len),
            st_re_p.reshape(batch, *gs), st_im_p.reshape(batch, *gs),
            kv_out(k_s, 1, n_s).reshape(n_s, 1, *hd), kv_out(v_s, 1, n_s).reshape(n_s, 1, *hd),
            st_re_s.reshape(n_s, *gs), st_im_s.reshape(n_s, *gs))


def kernel(x_prompt, x_sample, cache_k, cache_v, state_ssm_re, state_ssm_im, page_table, c_prompt, c_sample,
           w_ada, b_ada, norm1_g, norm2_g, w_in, q_norm_g, k_norm_g, sb_bias, ssm_lambda_re, ssm_lambda_im,
           ssm_log_dt, ssm_b_re, ssm_b_im, ssm_c_re, ssm_c_im, ssm_d, w_glu_a, w_glu_b, w_sb_out, w_out,
           w_router, router_bias, w_gate_e, w_up_e, w_down_e, w_gate_s, w_up_s, w_down_s):
    depth = w_in.shape[0]
    assert depth == 1, "single-layer step"
    l = 0
    p = dict(
        w_ada=w_ada[l], b_ada=b_ada[l],
        norm1_g=norm1_g[l], norm2_g=norm2_g[l].reshape(1, D_MODEL),
        w_in=w_in[l].astype(BF16), q_norm_g=q_norm_g[l], k_norm_g=k_norm_g[l], sb_bias=sb_bias[l],
        lam_re=ssm_lambda_re[l], lam_im=ssm_lambda_im[l], log_dt=ssm_log_dt[l],
        b_re=ssm_b_re[l], b_im=ssm_b_im[l], c_re=ssm_c_re[l], c_im=ssm_c_im[l], d_skip=ssm_d[l],
        w_glu_a=w_glu_a[l].astype(BF16), w_glu_b=w_glu_b[l].astype(BF16),
        w_sb_out=w_sb_out[l].astype(BF16), w_out=w_out[l].astype(BF16),
        w_router_t=w_router[l].T, router_bias=router_bias[l].reshape(N_EXPERTS, 1),
        w_gate_e=w_gate_e[l].astype(BF16), w_up_e=w_up_e[l].astype(BF16),
        w_down_e=w_down_e[l].astype(BF16).reshape(N_EXPERTS // EXPERTS_PER_STEP,
                                                   EXPERTS_PER_STEP * EXPERT_HIDDEN, D_MODEL),
        w_gate_s=w_gate_s[l].astype(BF16), w_up_s=w_up_s[l].astype(BF16), w_down_s=w_down_s[l].astype(BF16),
    )
    outs = _layer(x_prompt, x_sample, cache_k[l], cache_v[l], state_ssm_re[l], state_ssm_im[l],
                  page_table, c_prompt, c_sample, p)
    y_p, y_s = outs[0], outs[1]
    return (y_p, y_s) + tuple(o[None] for o in outs[2:])
```

```python
import functools
import math

import jax
import jax.numpy as jnp
from jax import lax
from jax.experimental import pallas as pl
from jax.experimental.pallas import tpu as pltpu

F32 = jnp.float32
BF16 = jnp.bfloat16

D_MODEL = 1024
SSM_WIDTH = 512
SSM_GROUP = 16
SSM_GROUPS = 32
SSM_STATE = 64
SSM_FLAT = SSM_GROUPS * SSM_STATE
EIG_CLIP = -1e-4
SB_HEADS = 8
SB_HEAD_DIM = 64
SB_WIDTH = 512
PAGE_SIZE = 128
N_EXPERTS = 64
TOP_K = 8
N_GROUPS = 8
TOPK_GROUPS = 4
GROUP_SIZE = N_EXPERTS // N_GROUPS
EXPERT_HIDDEN = 256
EXPERTS_PER_STEP = 4
ROUTE_BLOCK = 256
ROUTE_CAP = 64
ROUTED_SCALE = 2.5
RMS_EPS = 1e-6

LOG2E = math.log2(math.e)
LOGIT_SCALE_LOG2 = SB_HEAD_DIM ** -0.5 * LOG2E
HEADS_PER_STEP = 4
HEAD_LANES = HEADS_PER_STEP * SB_HEAD_DIM
VMEM_LIMIT = 48 * 1024 * 1024


def _cparams(sem):
    return pltpu.CompilerParams(dimension_semantics=sem, vmem_limit_bytes=VMEM_LIMIT)


def _dot(a, b):
    return jnp.dot(a, b, preferred_element_type=F32)


def _dot_t(a, b):
    return lax.dot_general(a, b, (((1,), (1,)), ((), ())), preferred_element_type=F32)


def _split_bf16(x):
    hi = x.astype(BF16)
    lo = (x - hi.astype(F32)).astype(BF16)
    return hi, lo


def _silu(x):
    return x * jax.nn.sigmoid(x)


def _gelu_tanh(x):
    c = math.sqrt(2.0 / math.pi)
    return 0.5 * x * (1.0 + jnp.tanh(c * (x + 0.044715 * (x * x * x))))


def _ada_kernel(c_ref, w_ref, b_ref, o_ref):
    s = _silu(c_ref[...])
    s_hi, s_lo = _split_bf16(s)
    w_hi, w_lo = _split_bf16(w_ref[...])
    o_ref[...] = _dot(s_hi, w_hi) + _dot(s_hi, w_lo) + _dot(s_lo, w_hi) + b_ref[...]


def _ada_mod(c, w_ada, b_ada):
    rows, d = c.shape
    cols = w_ada.shape[1]
    tn = 1024
    return pl.pallas_call(
        _ada_kernel,
        out_shape=jax.ShapeDtypeStruct((rows, cols), F32),
        grid=(cols // tn,),
        in_specs=[pl.BlockSpec((rows, d), lambda j: (0, 0)),
                  pl.BlockSpec((d, tn), lambda j: (0, j)),
                  pl.BlockSpec((1, tn), lambda j: (0, j))],
        out_specs=pl.BlockSpec((rows, tn), lambda j: (0, j)),
        compiler_params=_cparams(("parallel",)),
        name="ada_mod",
    )(c, w_ada, b_ada.reshape(1, cols))


def _ssm_prep_kernel(lre_ref, lim_ref, ldt_ref, bre_ref, bim_ref,
                     are_ref, aim_ref, bbre_ref, bbim_ref):
    dt = jnp.exp(ldt_ref[...])
    lre = jnp.minimum(lre_ref[...], EIG_CLIP)
    lim = lim_ref[...]
    mag = jnp.exp(lre * dt)
    a_re = mag * jnp.cos(lim * dt)
    a_im = mag * jnp.sin(lim * dt)
    den = lre * lre + lim * lim
    f_re = ((a_re - 1.0) * lre + a_im * lim) / den
    f_im = (a_im * lre - (a_re - 1.0) * lim) / den
    br = bre_ref[...]
    bi = bim_ref[...]
    are_ref[...] = a_re
    aim_ref[...] = a_im
    bbre_ref[...] = f_re * br - f_im * bi
    bbim_ref[...] = f_re * bi + f_im * br


def _ssm_prep(lam_re, lam_im, log_dt, b_re, b_im):
    rep = lambda a: jnp.repeat(a.astype(F32), SSM_GROUP, axis=0)
    lre, lim = rep(lam_re), rep(lam_im)
    ldt = rep(jnp.broadcast_to(log_dt.astype(F32)[:, None], (SSM_GROUPS, SSM_STATE)))
    tr = lambda b: b.astype(F32).transpose(0, 2, 1).reshape(SSM_WIDTH, SSM_STATE)
    shp = jax.ShapeDtypeStruct((SSM_WIDTH, SSM_STATE), F32)
    a_re, a_im, bb_re, bb_im = pl.pallas_call(
        _ssm_prep_kernel, out_shape=(shp, shp, shp, shp), name="ssm_prep",
    )(lre, lim, ldt, tr(b_re), tr(b_im))
    eye = jnp.eye(SSM_GROUPS, dtype=F32)

    def blockdiag_in(bb):
        bb4 = bb.reshape(SSM_GROUPS, SSM_GROUP, 1, SSM_STATE) * eye.reshape(SSM_GROUPS, 1, SSM_GROUPS, 1)
        return bb4.reshape(SSM_WIDTH, SSM_FLAT).astype(BF16)

    a_re = a_re[::SSM_GROUP].reshape(1, SSM_FLAT)
    a_im = a_im[::SSM_GROUP].reshape(1, SSM_FLAT)
    return a_re, a_im, blockdiag_in(bb_re), blockdiag_in(bb_im)


def _blockdiag_out(c):
    eye = jnp.eye(SSM_GROUPS, dtype=F32)
    c4 = c.astype(F32).transpose(0, 2, 1).reshape(SSM_GROUPS, SSM_STATE, 1, SSM_GROUP)
    return (c4 * eye.reshape(SSM_GROUPS, 1, SSM_GROUPS, 1)).reshape(SSM_FLAT, SSM_WIDTH).astype(BF16)


def _inproj_kernel(x_ref, shift_ref, scale_ref, g1_ref, w_ref, qg_ref, kg_ref, hm_ref,
                   u_ref, q_ref, k_ref, kb_ref, v_ref, vb_ref, sga_ref, sgb_ref):
    x = x_ref[...]
    r = lax.rsqrt(jnp.mean(x * x, axis=-1, keepdims=True) + RMS_EPS)
    h = (x * r) * g1_ref[...]
    h = h * (1.0 + scale_ref[...]) + shift_ref[...]
    hb = h.astype(BF16)
    hm = hm_ref[...]

    def head_norm(t, g):
        ms = _dot((t * t).astype(BF16), hm)
        return (t * lax.rsqrt(ms + RMS_EPS)) * g

    u_ref[...] = _dot(hb, w_ref[:, 0:512])
    q = head_norm(_dot(hb, w_ref[:, 512:1024]), qg_ref[...])
    q_ref[...] = (q * LOGIT_SCALE_LOG2).astype(BF16)
    k = head_norm(_dot(hb, w_ref[:, 1024:1536]), kg_ref[...])
    k_ref[...] = k.T
    kb_ref[...] = k.astype(BF16)
    v = _dot(hb, w_ref[:, 1536:2048])
    v_ref[...] = v.T
    vb_ref[...] = v.astype(BF16)
    sga_ref[...] = jax.nn.sigmoid(_dot(hb, w_ref[:, 2048:3072])).astype(BF16)
    sgb_ref[...] = jax.nn.sigmoid(_dot(hb, w_ref[:, 3072:4096])).astype(BF16)


def _head_mean_matrix():
    idx = jnp.arange(SB_WIDTH) // SB_HEAD_DIM
    return ((idx[:, None] == idx[None, :]).astype(F32) / SB_HEAD_DIM).astype(BF16)


def _mod_spec(per_token, tm, rows_per_mod, chunk):
    if per_token:
        return pl.BlockSpec((tm, D_MODEL), lambda i: (i, chunk))
    return pl.BlockSpec((None, 1, D_MODEL), lambda i: (i // rows_per_mod, 0, chunk))


def _in_proj(x, mod, per_token, tm, seq_len, norm1_g, w_in_b, q_g, k_g):
    n = x.shape[0]
    tiles_per_mod = (n // mod.shape[0]) // tm if not per_token else 1
    tiles_per_seq = seq_len // tm
    row = lambda i: (i, 0)
    const = lambda i: (0, 0)
    f32o = lambda w: jax.ShapeDtypeStruct((n, w), F32)
    b16o = lambda w: jax.ShapeDtypeStruct((n, w), BF16)
    kvo = jax.ShapeDtypeStruct((n // seq_len, SB_WIDTH, seq_len), F32)
    kv_blk = pl.BlockSpec((None, SB_WIDTH, tm), lambda i: (i // tiles_per_seq, 0, i % tiles_per_seq))
    blk = lambda w: pl.BlockSpec((tm, w), row)
    return pl.pallas_call(
        _inproj_kernel,
        out_shape=(f32o(512), b16o(512), kvo, b16o(512), kvo, b16o(512), b16o(1024), b16o(1024)),
        grid=(n // tm,),
        in_specs=[blk(D_MODEL),
                  _mod_spec(per_token, tm, tiles_per_mod, 0),
                  _mod_spec(per_token, tm, tiles_per_mod, 1),
                  pl.BlockSpec((1, D_MODEL), const),
                  pl.BlockSpec(w_in_b.shape, const),
                  pl.BlockSpec((1, SB_WIDTH), const),
                  pl.BlockSpec((1, SB_WIDTH), const),
                  pl.BlockSpec((SB_WIDTH, SB_WIDTH), const)],
        out_specs=(blk(512), blk(512), kv_blk, blk(512), kv_blk, blk(512), blk(1024), blk(1024)),
        compiler_params=_cparams(("parallel",)),
        name="in_proj",
    )(x, mod, mod, norm1_g.reshape(1, D_MODEL), w_in_b,
      jnp.tile(q_g, SB_HEADS).reshape(1, SB_WIDTH), jnp.tile(k_g, SB_HEADS).reshape(1, SB_WIDTH),
      _head_mean_matrix())


def _ssm_drive(ub, bb_ref):
    half_c, half_s = SSM_WIDTH // 2, SSM_FLAT // 2
    lo = _dot(ub[:, :half_c], bb_ref[:half_c, :half_s])
    hi = _dot(ub[:, half_c:], bb_ref[half_c:, half_s:])
    return lo, hi


def _ssm_readout(xre, xim, ccre_ref, ccim_ref, d_ref, u):
    half_c, half_s = SSM_WIDTH // 2, SSM_FLAT // 2
    xr = xre.astype(BF16)
    xi = xim.astype(BF16)
    y_lo = _dot(xr[:, :half_s], ccre_ref[:half_s, :half_c]) + _dot(xi[:, :half_s], ccim_ref[:half_s, :half_c])
    y_hi = _dot(xr[:, half_s:], ccre_ref[half_s:, half_c:]) + _dot(xi[:, half_s:], ccim_ref[half_s:, half_c:])
    y = jnp.concatenate([y_lo, y_hi], axis=1) + d_ref[...] * u
    return _gelu_tanh(y)


def _ssm_prompt_kernel(u_ref, bbre_ref, bbim_ref, are_ref, aim_ref, ccre_ref, ccim_ref, d_ref,
                       g_ref, sre_ref, sim_ref, ubuf, gbuf, xre, xim, cre, cim, *, lane_chunk):
    batch, tt, _ = u_ref.shape
    rows = tt * batch
    half_s = SSM_FLAT // 2

    @pl.when(pl.program_id(0) == 0)
    def _():
        cre[...] = jnp.zeros_like(cre)
        cim[...] = jnp.zeros_like(cim)

    n_lane_tiles = SSM_WIDTH // 128
    for b in range(batch):
        for c in range(n_lane_tiles):
            ubuf[c, pl.ds(b, tt, stride=batch), :] = u_ref[b, :, c * 128:(c + 1) * 128]
    u = jnp.concatenate([ubuf[c] for c in range(n_lane_tiles)], axis=1)
    ub = u.astype(BF16)
    lo, hi = _ssm_drive(ub, bbre_ref)
    xre[:, :half_s] = lo
    xre[:, half_s:] = hi
    lo, hi = _ssm_drive(ub, bbim_ref)
    xim[:, :half_s] = lo
    xim[:, half_s:] = hi

    first_step = lax.broadcasted_iota(jnp.int32, (8, lane_chunk), 0) < batch
    for c0 in range(0, SSM_FLAT, lane_chunk):
        cols = pl.ds(c0, lane_chunk)
        ar = jnp.broadcast_to(are_ref[:, cols], (8, lane_chunk))
        ai = jnp.broadcast_to(aim_ref[:, cols], (8, lane_chunk))

        def step(i, carry, cols=cols, ar=ar, ai=ai):
            c_r, c_i = carry
            r0 = pl.multiple_of(i * 8, 8)
            br = xre[pl.ds(r0, 8), cols]
            bi = xim[pl.ds(r0, 8), cols]
            p_r = pltpu.roll(c_r, batch, 0)
            p_i = pltpu.roll(c_i, batch, 0)
            y_r = ar * p_r - ai * p_i + br
            y_i = ar * p_i + ai * p_r + bi
            q_r = pltpu.roll(y_r, batch, 0)
            q_i = pltpu.roll(y_i, batch, 0)
            z_r = ar * q_r - ai * q_i + br
            z_i = ar * q_i + ai * q_r + bi
            xre[pl.ds(r0, 8), cols] = jnp.where(first_step, y_r, z_r)
            xim[pl.ds(r0, 8), cols] = jnp.where(first_step, y_i, z_i)
            return z_r, z_i

        c_r, c_i = lax.fori_loop(0, rows // 8, step, (cre[:, cols], cim[:, cols]))
        cre[:, cols] = c_r
        cim[:, cols] = c_i

    g = _ssm_readout(xre[...], xim[...], ccre_ref, ccim_ref, d_ref, u)
    for c in range(n_lane_tiles):
        gbuf[c] = g[:, c * 128:(c + 1) * 128]
    for b in range(batch):
        g_ref[b] = jnp.concatenate([gbuf[c, pl.ds(b, tt, stride=batch), :] for c in range(n_lane_tiles)],
                                   axis=1).astype(BF16)
    sre_ref[...] = cre[...]
    sim_ref[...] = cim[...]


def _ssm_prompt(u, ssm, tt=128, lane_chunk=512):
    a_re, a_im, bb_re, bb_im, cc_re, cc_im, d_skip = ssm
    batch, t_len, _ = u.shape
    assert 2 * batch == 8, "one sublane tile must hold exactly two time steps"
    rows = tt * batch
    const = lambda j: (0, 0)
    full = lambda a: pl.BlockSpec(a.shape, const)
    seq_blk = pl.BlockSpec((batch, tt, SSM_WIDTH), lambda j: (0, j, 0))
    g, s_re, s_im = pl.pallas_call(
        functools.partial(_ssm_prompt_kernel, lane_chunk=lane_chunk),
        out_shape=(jax.ShapeDtypeStruct((batch, t_len, SSM_WIDTH), BF16),
                   jax.ShapeDtypeStruct((8, SSM_FLAT), F32),
                   jax.ShapeDtypeStruct((8, SSM_FLAT), F32)),
        grid=(t_len // tt,),
        in_specs=[seq_blk,
                  full(bb_re), full(bb_im), full(a_re), full(a_im), full(cc_re), full(cc_im), full(d_skip)],
        out_specs=(seq_blk, pl.BlockSpec((8, SSM_FLAT), const), pl.BlockSpec((8, SSM_FLAT), const)),
        scratch_shapes=[pltpu.VMEM((SSM_WIDTH // 128, rows, 128), F32),
                        pltpu.VMEM((SSM_WIDTH // 128, rows, 128), F32),
                        pltpu.VMEM((rows, SSM_FLAT), F32), pltpu.VMEM((rows, SSM_FLAT), F32),
                        pltpu.VMEM((8, SSM_FLAT), F32), pltpu.VMEM((8, SSM_FLAT), F32)],
        compiler_params=_cparams(("arbitrary",)),
        name="ssm_prompt",
    )(u, bb_re, bb_im, a_re, a_im, cc_re, cc_im, d_skip)
    return g, s_re[batch:], s_im[batch:]


def _ssm_sample_kernel(u_ref, s0re_ref, s0im_ref, bbre_ref, bbim_ref, are_ref, aim_ref,
                       ccre_ref, ccim_ref, d_ref, g_ref, sre_ref, sim_ref):
    u = u_ref[...]
    ub = u.astype(BF16)
    ar, ai = are_ref[...], aim_ref[...]
    s_r, s_i = s0re_ref[...], s0im_ref[...]
    lo, hi = _ssm_drive(ub, bbre_ref)
    x_r = ar * s_r - ai * s_i + jnp.concatenate([lo, hi], axis=1)
    lo, hi = _ssm_drive(ub, bbim_ref)
    x_i = ar * s_i + ai * s_r + jnp.concatenate([lo, hi], axis=1)
    sre_ref[...] = x_r
    sim_ref[...] = x_i
    g_ref[...] = _ssm_readout(x_r, x_i, ccre_ref, ccim_ref, d_ref, u).astype(BF16)


def _ssm_sample(u, s_re, s_im, ssm):
    a_re, a_im, bb_re, bb_im, cc_re, cc_im, d_skip = ssm
    n = u.shape[0]
    return pl.pallas_call(
        _ssm_sample_kernel,
        out_shape=(jax.ShapeDtypeStruct((n, SSM_WIDTH), BF16),
                   jax.ShapeDtypeStruct((n, SSM_FLAT), F32),
                   jax.ShapeDtypeStruct((n, SSM_FLAT), F32)),
        compiler_params=pltpu.CompilerParams(vmem_limit_bytes=VMEM_LIMIT),
        name="ssm_sample",
    )(u, s_re, s_im, bb_re, bb_im, a_re, a_im, cc_re, cc_im, d_skip)


def _log_sigmoid_pair(z2):
    sign = jnp.uint32(0x80000000)
    neg_abs = lax.bitcast_convert_type(lax.bitcast_convert_type(z2, jnp.uint32) | sign, F32)
    soft = jnp.log2(1.0 + jnp.exp2(neg_abs))
    lsig = jnp.minimum(z2, 0.0) - soft
    return lsig, lsig - z2


def _attn_prompt_kernel(bias_ref, q_ref, k_ref, v_ref, tri_ref, o_ref, acc_ref, run_ref, *, tq, tk):
    qi = pl.program_id(1)
    nh = HEADS_PER_STEP
    ngrp = SB_HEADS // nh
    lane_head = lax.broadcasted_iota(jnp.int32, (1, HEAD_LANES), 1) // SB_HEAD_DIM
    head_mask = [(lane_head == h).astype(F32).astype(BF16) for h in range(nh)]
    qs, bias = [], []
    for g in range(ngrp):
        q = q_ref[:, g * HEAD_LANES:(g + 1) * HEAD_LANES]
        qs.append(jnp.concatenate([q * head_mask[h] for h in range(nh)], axis=0))
        bias.append([bias_ref[g * nh + h] for h in range(nh)])
    tri = tri_ref[...]
    acc_ref[...] = jnp.zeros_like(acc_ref)
    run_ref[...] = jnp.zeros_like(run_ref)

    def block(js, masked):
        if masked:
            k0 = pl.multiple_of(js[0] * tk, tk)
            kpos = k0 + lax.broadcasted_iota(jnp.int32, (nh * tq, tk), 1)
            qpos = qi * tq + (lax.broadcasted_iota(jnp.int32, (nh * tq, tk), 0) & (tq - 1))
            valid = kpos < qpos
        for g in range(ngrp):
            lanes = slice(g * HEAD_LANES, (g + 1) * HEAD_LANES)
            run = run_ref[g]
            pv = None
            for j in js:
                k0 = pl.multiple_of(j * tk, tk)
                zs = _dot_t(qs[g], k_ref[pl.ds(k0, tk), lanes])
                lsig_parts, stay_parts = [], []
                for h in range(nh):
                    lsig, stay = _log_sigmoid_pair(zs[h * tq:(h + 1) * tq] + bias[g][h])
                    lsig_parts.append(lsig)
                    stay_parts.append(stay)
                lsig = jnp.concatenate(lsig_parts, axis=0)
                stay = jnp.concatenate(stay_parts, axis=0)
                if masked:
                    stay = jnp.where(valid, stay, 0.0)
                after = _dot(stay.astype(BF16), tri)
                w = jnp.exp2(lsig + after + run)
                if masked:
                    w = jnp.where(valid, w, 0.0)
                part = _dot(w.astype(BF16), v_ref[pl.ds(k0, tk), lanes])
                pv = part if pv is None else pv + part
                run = run + (after[:, 0:1] + stay[:, 0:1])
            acc_ref[g] += pv
            run_ref[g] = run

    j_diag = (qi * tq) // tk
    block((j_diag,), True)

    @pl.when(j_diag % 2 == 1)
    def _():
        block((j_diag - 1,), False)

    def body(jj, carry):
        j = 2 * (j_diag // 2 - jj) - 1
        block((j, j - 1), False)
        return carry

    lax.fori_loop(0, j_diag // 2, body, 0)
    for g in range(ngrp):
        acc = acc_ref[g]
        out = jnp.zeros((tq, HEAD_LANES), F32)
        for h in range(nh):
            out = jnp.where(lane_head == h, acc[h * tq:(h + 1) * tq], out)
        o_ref[:, g * HEAD_LANES:(g + 1) * HEAD_LANES] = out.astype(BF16)


def _strict_upper(tk):
    idx = jnp.arange(tk)
    return (idx[:, None] > idx[None, :]).astype(BF16)


def _attn_prompt(q, kb, vb, sb_bias, batch, t_len, tq=256, tk=256):
    assert tq & (tq - 1) == 0 and tq <= tk, "one masked key block must cover the diagonal"
    n = q.shape[0]
    nq = t_len // tq
    ngrp = SB_HEADS // HEADS_PER_STEP
    return pl.pallas_call(
        functools.partial(_attn_prompt_kernel, tq=tq, tk=tk),
        out_shape=jax.ShapeDtypeStruct((n, SB_WIDTH), BF16),
        grid=(batch, nq),
        in_specs=[pl.BlockSpec(memory_space=pltpu.SMEM),
                  pl.BlockSpec((tq, SB_WIDTH), lambda b, i: (b * nq + i, 0)),
                  pl.BlockSpec((t_len, SB_WIDTH), lambda b, i: (b, 0)),
                  pl.BlockSpec((t_len, SB_WIDTH), lambda b, i: (b, 0)),
                  pl.BlockSpec((tk, tk), lambda b, i: (0, 0))],
        out_specs=pl.BlockSpec((tq, SB_WIDTH), lambda b, i: (b * nq + i, 0)),
        scratch_shapes=[pltpu.VMEM((ngrp, HEADS_PER_STEP * tq, HEAD_LANES), F32),
                        pltpu.VMEM((ngrp, HEADS_PER_STEP * tq, 1), F32)],
        compiler_params=_cparams(("parallel", "arbitrary")),
        name="attn_prompt",
    )(sb_bias.astype(F32) * LOG2E, q, kb, vb, _strict_upper(tk))


def _attn_sample_kernel(pt_ref, q_ref, bias_ref, tri_ref, later_ref, *refs, n_pages):
    del pt_ref
    k_refs = refs[:n_pages]
    v_refs = refs[n_pages:2 * n_pages]
    o_ref = refs[2 * n_pages]
    hd = (SB_HEADS, SB_HEAD_DIM, PAGE_SIZE)
    q = q_ref[0].astype(F32)
    q_col = jnp.broadcast_to(q, (PAGE_SIZE, SB_WIDTH)).T
    q3 = q_col.reshape(hd)
    z = jnp.concatenate([jnp.sum(k_refs[p][0].reshape(hd) * q3, axis=1) for p in range(n_pages)], axis=0)
    lsig, stay = _log_sigmoid_pair(z + bias_ref[...])
    after = _dot(stay.astype(BF16), tri_ref[...])
    total = jnp.broadcast_to(after[:, 0:1] + stay[:, 0:1], stay.shape)
    t_hi, t_lo = _split_bf16(total)
    later = later_ref[...]
    run = _dot(later, t_hi) + _dot(later, t_lo)
    w = jnp.exp2(lsig + after + run)
    acc = [jnp.zeros((SB_HEAD_DIM, PAGE_SIZE), F32) for _ in range(SB_HEADS)]
    for p in range(n_pages):
        for h in range(SB_HEADS):
            r = p * SB_HEADS + h
            acc[h] = acc[h] + v_refs[p][0, h * SB_HEAD_DIM:(h + 1) * SB_HEAD_DIM, :] * w[r:r + 1, :]
    a_hi, a_lo = _split_bf16(jnp.concatenate(acc, axis=0))
    ones = jnp.ones((8, PAGE_SIZE), BF16)
    o_ref[0] = (_dot_t(ones, a_hi) + _dot_t(ones, a_lo))[0:1].astype(BF16)


def _attn_sample(q, cache_k, cache_v, page_table, sb_bias):
    n, n_pages = page_table.shape
    n_pool = cache_k.shape[0]
    ck = cache_k.transpose(0, 2, 3, 1).reshape(n_pool, SB_WIDTH, PAGE_SIZE)
    cv = cache_v.transpose(0, 2, 3, 1).reshape(n_pool, SB_WIDTH, PAGE_SIZE)
    page_spec = lambda p: pl.BlockSpec((1, SB_WIDTH, PAGE_SIZE), lambda i, pt, p=p: (pt[i * n_pages + p], 0, 0))
    tok_spec = pl.BlockSpec((1, 1, SB_WIDTH), lambda i, pt: (i, 0, 0))
    rows = n_pages * SB_HEADS
    const = lambda shape: pl.BlockSpec(shape, lambda i, pt: (0, 0))
    r = jnp.arange(rows)
    later = ((r[:, None] % SB_HEADS == r[None, :] % SB_HEADS)
             & (r[None, :] // SB_HEADS > r[:, None] // SB_HEADS)).astype(BF16)
    out = pl.pallas_call(
        functools.partial(_attn_sample_kernel, n_pages=n_pages),
        out_shape=jax.ShapeDtypeStruct((n, 1, SB_WIDTH), BF16),
        grid_spec=pltpu.PrefetchScalarGridSpec(
            num_scalar_prefetch=1,
            grid=(n,),
            in_specs=[tok_spec, const((rows, 1)), const((PAGE_SIZE, PAGE_SIZE)), const((rows, rows))]
                     + [page_spec(p) for p in range(n_pages)] * 2,
            out_specs=tok_spec),
        compiler_params=_cparams(("parallel",)),
        name="attn_sample",
    )(page_table.reshape(-1).astype(jnp.int32), q.reshape(n, 1, SB_WIDTH),
      jnp.tile(sb_bias.astype(F32) * LOG2E, n_pages).reshape(rows, 1), _strict_upper(PAGE_SIZE), later,
      *([ck] * n_pages), *([cv] * n_pages))
    return out.reshape(n, SB_WIDTH)


def _first_index_of_max(vals, row_id, n_rows):
    m = jnp.max(vals, axis=0, keepdims=True)
    first = jnp.min(jnp.where(vals == m, row_id, n_rows), axis=0, keepdims=True)
    return m, first


def _route(logits_t, rbias):
    tm = logits_t.shape[1]
    scores = jax.nn.sigmoid(logits_t)
    biased = scores + rbias
    row8 = lax.broadcasted_iota(jnp.int32, (GROUP_SIZE, tm), 0)
    neg = jnp.float32(-jnp.inf)
    grp_score = []
    for g in range(N_GROUPS):
        blk = biased[g * GROUP_SIZE:(g + 1) * GROUP_SIZE]
        m1, first = _first_index_of_max(blk, row8, GROUP_SIZE)
        m2 = jnp.max(jnp.where(row8 == first, neg, blk), axis=0, keepdims=True)
        grp_score.append(m1 + m2)
    masked = []
    for g in range(N_GROUPS):
        beaten_by = jnp.zeros((1, tm), jnp.int32)
        for o in range(N_GROUPS):
            if o == g:
                continue
            wins = (grp_score[o] > grp_score[g]) if o > g else (grp_score[o] >= grp_score[g])
            beaten_by = beaten_by + wins.astype(jnp.int32)
        keep = beaten_by < TOPK_GROUPS
        masked.append(jnp.where(keep, biased[g * GROUP_SIZE:(g + 1) * GROUP_SIZE], neg))
    cur = jnp.concatenate(masked, axis=0)
    row = lax.broadcasted_iota(jnp.int32, (N_EXPERTS, tm), 0)
    chosen = jnp.zeros((N_EXPERTS, tm), jnp.bool_)
    for _ in range(TOP_K):
        _, first = _first_index_of_max(cur, row, N_EXPERTS)
        pick = row == first
        chosen = jnp.logical_or(chosen, pick)
        cur = jnp.where(pick, neg, cur)
    sel = jnp.where(chosen, scores, 0.0)
    return sel / jnp.sum(sel, axis=0, keepdims=True) * ROUTED_SCALE


def _post_kernel(x_ref, g_ref, o_ref, sga_ref, sgb_ref, gate1_ref, shift2_ref, scale2_ref, g2_ref,
                 wga_ref, wgb_ref, wsb_ref, wout_ref, wr_ref, rb_ref,
                 x1_ref, h2_ref, gates_ref):
    g = g_ref[...]
    branch_a = _dot(g, wga_ref[...]) * jax.nn.sigmoid(_dot(g, wgb_ref[...]))
    branch_b = _dot(o_ref[...], wsb_ref[...])
    merged = sga_ref[...].astype(F32) * branch_a + sgb_ref[...].astype(F32) * branch_b
    x1 = x_ref[...] + gate1_ref[...] * _dot(merged.astype(BF16), wout_ref[...])
    x1_ref[...] = x1
    r = lax.rsqrt(jnp.mean(x1 * x1, axis=-1, keepdims=True) + RMS_EPS)
    h2 = (x1 * r) * g2_ref[...]
    h2 = h2 * (1.0 + scale2_ref[...]) + shift2_ref[...]
    h2_ref[...] = h2.astype(BF16)
    h_hi, h_lo = _split_bf16(h2)
    w_hi, w_lo = _split_bf16(wr_ref[...])
    logits_t = _dot_t(w_hi, h_hi) + _dot_t(w_hi, h_lo) + _dot_t(w_lo, h_hi)
    gates_ref[...] = _route(logits_t, rb_ref[...])


def _post_mixer(x, g, o, sga, sgb, mod, per_token, tm, p):
    n = x.shape[0]
    tiles_per_mod = (n // mod.shape[0]) // tm if not per_token else 1
    row = lambda i: (i, 0)
    const = lambda i: (0, 0)
    blk = lambda w: pl.BlockSpec((tm, w), row)
    full = lambda a: pl.BlockSpec(a.shape, const)
    ms = lambda c: _mod_spec(per_token, tm, tiles_per_mod, c)
    weights = (p["w_glu_a"], p["w_glu_b"], p["w_sb_out"], p["w_out"], p["w_router_t"], p["router_bias"])
    return pl.pallas_call(
        _post_kernel,
        out_shape=(jax.ShapeDtypeStruct((n, D_MODEL), F32),
                   jax.ShapeDtypeStruct((n, D_MODEL), BF16),
                   jax.ShapeDtypeStruct((N_EXPERTS, n), F32)),
        grid=(n // tm,),
        in_specs=[blk(D_MODEL), blk(512), blk(512), blk(1024), blk(1024), ms(2), ms(3), ms(4),
                  pl.BlockSpec((1, D_MODEL), const)] + [full(w) for w in weights],
        out_specs=(blk(D_MODEL), blk(D_MODEL), pl.BlockSpec((N_EXPERTS, tm), lambda i: (0, i))),
        compiler_params=_cparams(("parallel",)),
        name="post_mixer",
    )(x, g, o, sga, sgb, mod, mod, mod, p["norm2_g"], *weights)


def _moe_kernel(h_ref, gates_ref, x1_ref, gate2_ref, wgs_ref, wus_ref, wds_ref,
                wg_ref, wu_ref, wd_ref, y_ref, acc_ref):
    j = pl.program_id(1)
    h = h_ref[...]

    @pl.when(j == 0)
    def _():
        act = _silu(_dot(h, wgs_ref[...])) * _dot(h, wus_ref[...])
        acc_ref[...] = _dot(act.astype(BF16), wds_ref[...])

    gates = gates_ref[...]
    lane = lax.broadcasted_iota(jnp.int32, gates.shape, 1)
    scaled = []
    for s in range(EXPERTS_PER_STEP):
        act = _silu(_dot(h, wg_ref[s])) * _dot(h, wu_ref[s])
        gcol = jnp.sum(jnp.where(lane == j * EXPERTS_PER_STEP + s, gates, 0.0), axis=1, keepdims=True)
        scaled.append((act * gcol).astype(BF16))
    acc_ref[...] += _dot(jnp.concatenate(scaled, axis=1), wd_ref[...])

    @pl.when(j == pl.num_programs(1) - 1)
    def _():
        y_ref[...] = x1_ref[...] + gate2_ref[...] * acc_ref[...]


def _moe(h2, gates, x1, mod, per_token, tm, p):
    n = h2.shape[0]
    tiles_per_mod = (n // mod.shape[0]) // tm if not per_token else 1
    row = lambda i, e: (i, 0)
    const = lambda i, e: (0, 0)
    exp = lambda i, e: (e, 0, 0)
    if per_token:
        gate2_spec = pl.BlockSpec((tm, D_MODEL), lambda i, e: (i, 5))
    else:
        gate2_spec = pl.BlockSpec((None, 1, D_MODEL), lambda i, e: (i // tiles_per_mod, 0, 5))
    wide = EXPERTS_PER_STEP * EXPERT_HIDDEN
    return pl.pallas_call(
        _moe_kernel,
        out_shape=jax.ShapeDtypeStruct((n, D_MODEL), F32),
        grid=(n // tm, N_EXPERTS // EXPERTS_PER_STEP),
        in_specs=[pl.BlockSpec((tm, D_MODEL), row),
                  pl.BlockSpec((tm, N_EXPERTS), row),
                  pl.BlockSpec((tm, D_MODEL), row),
                  gate2_spec,
                  pl.BlockSpec((D_MODEL, EXPERT_HIDDEN), const),
                  pl.BlockSpec((D_MODEL, EXPERT_HIDDEN), const),
                  pl.BlockSpec((EXPERT_HIDDEN, D_MODEL), const),
                  pl.BlockSpec((EXPERTS_PER_STEP, D_MODEL, EXPERT_HIDDEN), exp),
                  pl.BlockSpec((EXPERTS_PER_STEP, D_MODEL, EXPERT_HIDDEN), exp),
                  pl.BlockSpec((None, wide, D_MODEL), exp)],
        out_specs=pl.BlockSpec((tm, D_MODEL), row),
        scratch_shapes=[pltpu.VMEM((tm, D_MODEL), F32)],
        compiler_params=_cparams(("parallel", "arbitrary")),
        name="moe",
    )(h2, gates, x1, mod, p["w_gate_s"], p["w_up_s"], p["w_down_s"], p["w_gate_e"], p["w_up_e"], p["w_down_e"])


def _dispatch_kernel(gates_ref, h_ref, tri_ref, xs_ref, flag_ref):
    g = gates_ref[...]
    sel = g > 0.0
    rank = _dot(sel.astype(F32).astype(BF16), tri_ref[...])
    rank = jnp.where(sel, rank, -1.0)
    over = jnp.where(rank >= ROUTE_CAP, 1.0, 0.0)
    worst = jnp.max(jnp.max(over, axis=1, keepdims=True), axis=0, keepdims=True)
    flag_ref[...] = jnp.broadcast_to(worst, flag_ref.shape)
    slot = lax.broadcasted_iota(jnp.int32, (ROUTE_CAP, g.shape[1]), 0).astype(F32)
    h = h_ref[...]
    per_dot = 8
    for e0 in range(0, N_EXPERTS, per_dot):
        place = jnp.concatenate([(slot == rank[e:e + 1, :]).astype(F32).astype(BF16)
                                 for e in range(e0, e0 + per_dot)], axis=0)
        xs_ref[e0 * ROUTE_CAP:(e0 + per_dot) * ROUTE_CAP, :] = _dot(place, h).astype(BF16)


def _expert_kernel(x_ref, wg_ref, wu_ref, wd_ref, y_ref):
    nb, cap, d = x_ref.shape
    x = x_ref[...].reshape(nb * cap, d)
    n_part = 2 if nb % 2 == 0 else 1
    rows = nb * cap // n_part
    for r in range(n_part):
        xr = x[r * rows:(r + 1) * rows]
        act = _silu(_dot(xr, wg_ref[...])) * _dot(xr, wu_ref[...])
        y = _dot(act.astype(BF16), wd_ref[...]).astype(BF16)
        y_ref[r * (nb // n_part):(r + 1) * (nb // n_part)] = y.reshape(nb // n_part, cap, d)


def _combine_kernel(gates_ref, ys_ref, h_ref, x1_ref, gate2_ref, wgs_ref, wus_ref, wds_ref,
                    tri_ref, expand_ref, y_ref):
    g = gates_ref[...]
    sel = g > 0.0
    rank = _dot(tri_ref[...], sel.astype(F32).astype(BF16))
    rank = jnp.where(sel, rank, -1.0)
    expand = expand_ref[...]
    rank_of_slot = _dot(rank.astype(BF16), expand)
    slot = (lax.broadcasted_iota(jnp.int32, rank_of_slot.shape, 1) & (ROUTE_CAP - 1)).astype(F32)
    hit = slot == rank_of_slot
    gate_of_slot = _dot(g.astype(BF16), expand)
    routed = _dot(jnp.where(hit, gate_of_slot, 0.0).astype(BF16), ys_ref[...])
    h = h_ref[...]
    act = _silu(_dot(h, wgs_ref[...])) * _dot(h, wus_ref[...])
    shared = _dot(act.astype(BF16), wds_ref[...])
    y_ref[...] = x1_ref[...] + gate2_ref[...] * (shared + routed)


def _moe_routed(h2, gates_t, x1, mod, p):
    assert ROUTE_CAP & (ROUTE_CAP - 1) == 0
    n = h2.shape[0]
    bt = ROUTE_BLOCK
    nb = n // bt
    slots = N_EXPERTS * ROUTE_CAP
    tiles_per_mod = (n // mod.shape[0]) // bt
    later = _strict_upper(bt)
    const2 = lambda b: (0, 0)
    xs, flags = pl.pallas_call(
        _dispatch_kernel,
        out_shape=(jax.ShapeDtypeStruct((nb, slots, D_MODEL), BF16),
                   jax.ShapeDtypeStruct((nb, 8, 128), F32)),
        grid=(nb,),
        in_specs=[pl.BlockSpec((N_EXPERTS, bt), lambda b: (0, b)),
                  pl.BlockSpec((bt, D_MODEL), lambda b: (b, 0)),
                  pl.BlockSpec((bt, bt), const2)],
        out_specs=(pl.BlockSpec((None, slots, D_MODEL), lambda b: (b, 0, 0)),
                   pl.BlockSpec((None, 8, 128), lambda b: (b, 0, 0))),
        compiler_params=_cparams(("parallel",)),
        name="moe_dispatch",
    )(gates_t, h2, later.T)
    gates = gates_t.T

    def routed(_):
        n_half = 2
        xblk = pl.BlockSpec((nb // n_half, None, ROUTE_CAP, D_MODEL), lambda e, f: (f, e, 0, 0))
        ys = pl.pallas_call(
            _expert_kernel,
            out_shape=jax.ShapeDtypeStruct((nb, N_EXPERTS, ROUTE_CAP, D_MODEL), BF16),
            grid=(N_EXPERTS, n_half),
            in_specs=[xblk,
                      pl.BlockSpec((None, D_MODEL, EXPERT_HIDDEN), lambda e, f: (e, 0, 0)),
                      pl.BlockSpec((None, D_MODEL, EXPERT_HIDDEN), lambda e, f: (e, 0, 0)),
                      pl.BlockSpec((None, EXPERT_HIDDEN, D_MODEL), lambda e, f: (e, 0, 0))],
            out_specs=xblk,
            compiler_params=_cparams(("parallel", "arbitrary")),
            name="moe_experts",
        )(xs.reshape(nb, N_EXPERTS, ROUTE_CAP, D_MODEL), p["w_gate_e"], p["w_up_e"],
          p["w_down_e"].reshape(N_EXPERTS, EXPERT_HIDDEN, D_MODEL))
        expand = (jnp.arange(slots)[None, :] // ROUTE_CAP == jnp.arange(N_EXPERTS)[:, None]).astype(BF16)
        blk = lambda w: pl.BlockSpec((bt, w), lambda b: (b, 0))
        full = lambda a: pl.BlockSpec(a.shape, const2)
        return pl.pallas_call(
            _combine_kernel,
            out_shape=jax.ShapeDtypeStruct((n, D_MODEL), F32),
            grid=(nb,),
            in_specs=[blk(N_EXPERTS),
                      pl.BlockSpec((None, slots, D_MODEL), lambda b: (b, 0, 0)),
                      blk(D_MODEL), blk(D_MODEL), _mod_spec(False, bt, tiles_per_mod, 5),
                      full(p["w_gate_s"]), full(p["w_up_s"]), full(p["w_down_s"]), full(later), full(expand)],
            out_specs=blk(D_MODEL),
            compiler_params=_cparams(("parallel",)),
            name="moe_combine",
        )(gates, ys.reshape(nb, slots, D_MODEL), h2, x1, mod, p["w_gate_s"], p["w_up_s"], p["w_down_s"],
          later, expand)

    def dense(_):
        return _moe(h2, gates, x1, mod, False, 1024, p)

    return lax.cond(jnp.max(flags) > 0.0, dense, routed, None)


def _layer(x_prompt, x_sample, cache_k, cache_v, s0_re, s0_im, page_table, c_prompt, c_sample, p):
    batch, t_len, d = x_prompt.shape
    n_s = x_sample.shape[0]
    n_p = batch * t_len

    c_all = jnp.concatenate([c_prompt, c_sample], axis=0)
    pad = (-c_all.shape[0]) % 8
    c_all = jnp.pad(c_all, ((0, pad), (0, 0)))
    mod = _ada_mod(c_all, p["w_ada"], p["b_ada"])
    mod_p = mod[:batch].reshape(batch, 1, 6 * d)
    mod_s = mod[batch:batch + n_s]

    a_re, a_im, bb_re, bb_im = _ssm_prep(p["lam_re"], p["lam_im"], p["log_dt"], p["b_re"], p["b_im"])
    ssm = (a_re, a_im, bb_re, bb_im, _blockdiag_out(p["c_re"]), _blockdiag_out(-p["c_im"]),
           p["d_skip"].astype(F32).reshape(1, SSM_WIDTH))

    xp = x_prompt.reshape(n_p, d)
    xs = x_sample.reshape(n_s, d)
    proj_p = _in_proj(xp, mod_p, False, 512, t_len, p["norm1_g"], p["w_in"], p["q_norm_g"], p["k_norm_g"])
    proj_s = _in_proj(xs, mod_s, True, n_s, n_s, p["norm1_g"], p["w_in"], p["q_norm_g"], p["k_norm_g"])
    u_p, q_p, k_p, kb_p, v_p, vb_p, sga_p, sgb_p = proj_p
    u_s, q_s, k_s, _, v_s, _, sga_s, sgb_s = proj_s

    g_p, st_re_p, st_im_p = _ssm_prompt(u_p.reshape(batch, t_len, SSM_WIDTH), ssm)
    g_p = g_p.reshape(n_p, SSM_WIDTH)
    g_s, st_re_s, st_im_s = _ssm_sample(u_s, s0_re.reshape(n_s, SSM_FLAT), s0_im.reshape(n_s, SSM_FLAT), ssm)

    o_p = _attn_prompt(q_p, kb_p, vb_p, p["sb_bias"], batch, t_len)
    o_s = _attn_sample(q_s, cache_k, cache_v, page_table, p["sb_bias"])

    x1_p, h2_p, gates_p = _post_mixer(xp, g_p, o_p, sga_p, sgb_p, mod_p, False, 512, p)
    x1_s, h2_s, gates_s = _post_mixer(xs, g_s, o_s, sga_s, sgb_s, mod_s, True, n_s, p)

    y_p = _moe_routed(h2_p, gates_p, x1_p, mod_p, p)
    y_s = _moe(h2_s, gates_s.T, x1_s, mod_s, True, n_s, p)

    hd = (SB_HEADS, SB_HEAD_DIM)
    gs = (SSM_GROUPS, SSM_STATE)

    def kv_out(a_t, n_seq, seq_len):
        return a_t.reshape(n_seq, *hd, seq_len).transpose(0, 3, 1, 2)

    return (y_p.reshape(batch, t_len, d), y_s.reshape(n_s, 1, d),
            kv_out(k_p, batch, t_len), kv_out(v_p, batch, t_len),
            st_re_p.reshape(batch, *gs), st_im_p.reshape(batch, *gs),
            kv_out(k_s, 1, n_s).reshape(n_s, 1, *hd), kv_out(v_s, 1, n_s).reshape(n_s, 1, *hd),
            st_re_s.reshape(n_s, *gs), st_im_s.reshape(n_s, *gs))


def kernel(x_prompt, x_sample, cache_k, cache_v, state_ssm_re, state_ssm_im, page_table, c_prompt, c_sample,
           w_ada, b_ada, norm1_g, norm2_g, w_in, q_norm_g, k_norm_g, sb_bias, ssm_lambda_re, ssm_lambda_im,
           ssm_log_dt, ssm_b_re, ssm_b_im, ssm_c_re, ssm_c_im, ssm_d, w_glu_a, w_glu_b, w_sb_out, w_out,
           w_router, router_bias, w_gate_e, w_up_e, w_down_e, w_gate_s, w_up_s, w_down_s):
    depth = w_in.shape[0]
    assert depth == 1, "single-layer step"
    l = 0
    p = dict(
        w_ada=w_ada[l], b_ada=b_ada[l],
        norm1_g=norm1_g[l], norm2_g=norm2_g[l].reshape(1, D_MODEL),
        w_in=w_in[l].astype(BF16), q_norm_g=q_norm_g[l], k_norm_g=k_norm_g[l], sb_bias=sb_bias[l],
        lam_re=ssm_lambda_re[l], lam_im=ssm_lambda_im[l], log_dt=ssm_log_dt[l],
        b_re=ssm_b_re[l], b_im=ssm_b_im[l], c_re=ssm_c_re[l], c_im=ssm_c_im[l], d_skip=ssm_d[l],
        w_glu_a=w_glu_a[l].astype(BF16), w_glu_b=w_glu_b[l].astype(BF16),
        w_sb_out=w_sb_out[l].astype(BF16), w_out=w_out[l].astype(BF16),
        w_router_t=w_router[l].T, router_bias=router_bias[l].reshape(N_EXPERTS, 1),
        w_gate_e=w_gate_e[l].astype(BF16), w_up_e=w_up_e[l].astype(BF16),
        w_down_e=w_down_e[l].astype(BF16).reshape(N_EXPERTS // EXPERTS_PER_STEP,
                                                   EXPERTS_PER_STEP * EXPERT_HIDDEN, D_MODEL),
        w_gate_s=w_gate_s[l].astype(BF16), w_up_s=w_up_s[l].astype(BF16), w_down_s=w_down_s[l].astype(BF16),
    )
    outs = _layer(x_prompt, x_sample, cache_k[l], cache_v[l], state_ssm_re[l], state_ssm_im[l],
                  page_table, c_prompt, c_sample, p)
    y_p, y_s = outs[0], outs[1]
    return (y_p, y_s) + tuple(o[None] for o in outs[2:])
```

```python
import functools
import math

import jax
import jax.numpy as jnp
from jax import lax
from jax.experimental import pallas as pl
from jax.experimental.pallas import tpu as pltpu

F32 = jnp.float32
BF16 = jnp.bfloat16

D_MODEL = 1024
SSM_WIDTH = 512
SSM_GROUP = 16
SSM_GROUPS = 32
SSM_STATE = 64
SSM_FLAT = SSM_GROUPS * SSM_STATE
EIG_CLIP = -1e-4
SB_HEADS = 8
SB_HEAD_DIM = 64
SB_WIDTH = 512
PAGE_SIZE = 128
N_EXPERTS = 64
TOP_K = 8
N_GROUPS = 8
TOPK_GROUPS = 4
GROUP_SIZE = N_EXPERTS // N_GROUPS
EXPERT_HIDDEN = 256
EXPERTS_PER_STEP = 4
ROUTE_BLOCK = 256
ROUTE_CAP = 64
MAX_FIX_BLOCKS = 8
ROUTED_SCALE = 2.5
RMS_EPS = 1e-6

LOG2E = math.log2(math.e)
LOGIT_SCALE_LOG2 = SB_HEAD_DIM ** -0.5 * LOG2E
HEADS_PER_STEP = 4
HEAD_LANES = HEADS_PER_STEP * SB_HEAD_DIM
VMEM_LIMIT = 48 * 1024 * 1024


def _cparams(sem):
    return pltpu.CompilerParams(dimension_semantics=sem, vmem_limit_bytes=VMEM_LIMIT)


def _dot(a, b):
    return jnp.dot(a, b, preferred_element_type=F32)


def _dot_t(a, b):
    return lax.dot_general(a, b, (((1,), (1,)), ((), ())), preferred_element_type=F32)


def _split_bf16(x):
    hi = x.astype(BF16)
    lo = (x - hi.astype(F32)).astype(BF16)
    return hi, lo


def _silu(x):
    return x * jax.nn.sigmoid(x)


def _gelu_tanh(x):
    c = math.sqrt(2.0 / math.pi)
    return 0.5 * x * (1.0 + jnp.tanh(c * (x + 0.044715 * (x * x * x))))


def _ada_kernel(c_ref, w_ref, b_ref, o_ref):
    s = _silu(c_ref[...])
    s_hi, s_lo = _split_bf16(s)
    w_hi, w_lo = _split_bf16(w_ref[...])
    o_ref[...] = _dot(s_hi, w_hi) + _dot(s_hi, w_lo) + _dot(s_lo, w_hi) + b_ref[...]


def _ada_mod(c, w_ada, b_ada):
    rows, d = c.shape
    cols = w_ada.shape[1]
    tn = 1024
    return pl.pallas_call(
        _ada_kernel,
        out_shape=jax.ShapeDtypeStruct((rows, cols), F32),
        grid=(cols // tn,),
        in_specs=[pl.BlockSpec((rows, d), lambda j: (0, 0)),
                  pl.BlockSpec((d, tn), lambda j: (0, j)),
                  pl.BlockSpec((1, tn), lambda j: (0, j))],
        out_specs=pl.BlockSpec((rows, tn), lambda j: (0, j)),
        compiler_params=_cparams(("parallel",)),
        name="ada_mod",
    )(c, w_ada, b_ada.reshape(1, cols))


def _ssm_prep_kernel(lre_ref, lim_ref, ldt_ref, bre_ref, bim_ref,
                     are_ref, aim_ref, bbre_ref, bbim_ref):
    dt = jnp.exp(ldt_ref[...])
    lre = jnp.minimum(lre_ref[...], EIG_CLIP)
    lim = lim_ref[...]
    mag = jnp.exp(lre * dt)
    a_re = mag * jnp.cos(lim * dt)
    a_im = mag * jnp.sin(lim * dt)
    den = lre * lre + lim * lim
    f_re = ((a_re - 1.0) * lre + a_im * lim) / den
    f_im = (a_im * lre - (a_re - 1.0) * lim) / den
    br = bre_ref[...]
    bi = bim_ref[...]
    are_ref[...] = a_re
    aim_ref[...] = a_im
    bbre_ref[...] = f_re * br - f_im * bi
    bbim_ref[...] = f_re * bi + f_im * br


def _ssm_prep(lam_re, lam_im, log_dt, b_re, b_im):
    rep = lambda a: jnp.repeat(a.astype(F32), SSM_GROUP, axis=0)
    lre, lim = rep(lam_re), rep(lam_im)
    ldt = rep(jnp.broadcast_to(log_dt.astype(F32)[:, None], (SSM_GROUPS, SSM_STATE)))
    tr = lambda b: b.astype(F32).transpose(0, 2, 1).reshape(SSM_WIDTH, SSM_STATE)
    shp = jax.ShapeDtypeStruct((SSM_WIDTH, SSM_STATE), F32)
    a_re, a_im, bb_re, bb_im = pl.pallas_call(
        _ssm_prep_kernel, out_shape=(shp, shp, shp, shp), name="ssm_prep",
    )(lre, lim, ldt, tr(b_re), tr(b_im))
    eye = jnp.eye(SSM_GROUPS, dtype=F32)

    def blockdiag_in(bb):
        bb4 = bb.reshape(SSM_GROUPS, SSM_GROUP, 1, SSM_STATE) * eye.reshape(SSM_GROUPS, 1, SSM_GROUPS, 1)
        return bb4.reshape(SSM_WIDTH, SSM_FLAT).astype(BF16)

    a_re = a_re[::SSM_GROUP].reshape(1, SSM_FLAT)
    a_im = a_im[::SSM_GROUP].reshape(1, SSM_FLAT)
    return a_re, a_im, blockdiag_in(bb_re), blockdiag_in(bb_im)


def _blockdiag_out(c):
    eye = jnp.eye(SSM_GROUPS, dtype=F32)
    c4 = c.astype(F32).transpose(0, 2, 1).reshape(SSM_GROUPS, SSM_STATE, 1, SSM_GROUP)
    return (c4 * eye.reshape(SSM_GROUPS, 1, SSM_GROUPS, 1)).reshape(SSM_FLAT, SSM_WIDTH).astype(BF16)


def _inproj_kernel(x_ref, shift_ref, scale_ref, g1_ref, w_ref, qg_ref, kg_ref, hm_ref,
                   u_ref, q_ref, k_ref, kb_ref, v_ref, vb_ref, sga_ref, sgb_ref):
    x = x_ref[...]
    r = lax.rsqrt(jnp.mean(x * x, axis=-1, keepdims=True) + RMS_EPS)
    h = (x * r) * g1_ref[...]
    h = h * (1.0 + scale_ref[...]) + shift_ref[...]
    hb = h.astype(BF16)
    hm = hm_ref[...]

    def head_norm(t, g):
        ms = _dot((t * t).astype(BF16), hm)
        return (t * lax.rsqrt(ms + RMS_EPS)) * g

    u_ref[...] = _dot(hb, w_ref[:, 0:512])
    q = head_norm(_dot(hb, w_ref[:, 512:1024]), qg_ref[...])
    q_ref[...] = (q * LOGIT_SCALE_LOG2).astype(BF16)
    k = head_norm(_dot(hb, w_ref[:, 1024:1536]), kg_ref[...])
    k_ref[...] = k.T
    kb_ref[...] = k.astype(BF16)
    v = _dot(hb, w_ref[:, 1536:2048])
    v_ref[...] = v.T
    vb_ref[...] = v.astype(BF16)
    sga_ref[...] = jax.nn.sigmoid(_dot(hb, w_ref[:, 2048:3072])).astype(BF16)
    sgb_ref[...] = jax.nn.sigmoid(_dot(hb, w_ref[:, 3072:4096])).astype(BF16)


def _head_mean_matrix():
    idx = jnp.arange(SB_WIDTH) // SB_HEAD_DIM
    return ((idx[:, None] == idx[None, :]).astype(F32) / SB_HEAD_DIM).astype(BF16)


def _mod_spec(per_token, tm, rows_per_mod, chunk):
    if per_token:
        return pl.BlockSpec((tm, D_MODEL), lambda i: (i, chunk))
    return pl.BlockSpec((None, 1, D_MODEL), lambda i: (i // rows_per_mod, 0, chunk))


def _in_proj(x, mod, per_token, tm, seq_len, norm1_g, w_in_b, q_g, k_g):
    n = x.shape[0]
    tiles_per_mod = (n // mod.shape[0]) // tm if not per_token else 1
    tiles_per_seq = seq_len // tm
    row = lambda i: (i, 0)
    const = lambda i: (0, 0)
    f32o = lambda w: jax.ShapeDtypeStruct((n, w), F32)
    b16o = lambda w: jax.ShapeDtypeStruct((n, w), BF16)
    kvo = jax.ShapeDtypeStruct((n // seq_len, SB_WIDTH, seq_len), F32)
    kv_blk = pl.BlockSpec((None, SB_WIDTH, tm), lambda i: (i // tiles_per_seq, 0, i % tiles_per_seq))
    blk = lambda w: pl.BlockSpec((tm, w), row)
    return pl.pallas_call(
        _inproj_kernel,
        out_shape=(f32o(512), b16o(512), kvo, b16o(512), kvo, b16o(512), b16o(1024), b16o(1024)),
        grid=(n // tm,),
        in_specs=[blk(D_MODEL),
                  _mod_spec(per_token, tm, tiles_per_mod, 0),
                  _mod_spec(per_token, tm, tiles_per_mod, 1),
                  pl.BlockSpec((1, D_MODEL), const),
                  pl.BlockSpec(w_in_b.shape, const),
                  pl.BlockSpec((1, SB_WIDTH), const),
                  pl.BlockSpec((1, SB_WIDTH), const),
                  pl.BlockSpec((SB_WIDTH, SB_WIDTH), const)],
        out_specs=(blk(512), blk(512), kv_blk, blk(512), kv_blk, blk(512), blk(1024), blk(1024)),
        compiler_params=_cparams(("parallel",)),
        name="in_proj",
    )(x, mod, mod, norm1_g.reshape(1, D_MODEL), w_in_b,
      jnp.tile(q_g, SB_HEADS).reshape(1, SB_WIDTH), jnp.tile(k_g, SB_HEADS).reshape(1, SB_WIDTH),
      _head_mean_matrix())


def _ssm_drive(ub, bb_ref):
    half_c, half_s = SSM_WIDTH // 2, SSM_FLAT // 2
    lo = _dot(ub[:, :half_c], bb_ref[:half_c, :half_s])
    hi = _dot(ub[:, half_c:], bb_ref[half_c:, half_s:])
    return lo, hi


def _ssm_readout(xre, xim, ccre_ref, ccim_ref, d_ref, u):
    half_c, half_s = SSM_WIDTH // 2, SSM_FLAT // 2
    xr = xre.astype(BF16)
    xi = xim.astype(BF16)
    y_lo = _dot(xr[:, :half_s], ccre_ref[:half_s, :half_c]) + _dot(xi[:, :half_s], ccim_ref[:half_s, :half_c])
    y_hi = _dot(xr[:, half_s:], ccre_ref[half_s:, half_c:]) + _dot(xi[:, half_s:], ccim_ref[half_s:, half_c:])
    y = jnp.concatenate([y_lo, y_hi], axis=1) + d_ref[...] * u
    return _gelu_tanh(y)


def _ssm_prompt_kernel(u_ref, bbre_ref, bbim_ref, are_ref, aim_ref, ccre_ref, ccim_ref, d_ref,
                       g_ref, sre_ref, sim_ref, ubuf, gbuf, xre, xim, cre, cim, *, lane_chunk):
    batch, tt, _ = u_ref.shape
    rows = tt * batch
    half_s = SSM_FLAT // 2

    @pl.when(pl.program_id(0) == 0)
    def _():
        cre[...] = jnp.zeros_like(cre)
        cim[...] = jnp.zeros_like(cim)

    n_lane_tiles = SSM_WIDTH // 128
    for b in range(batch):
        for c in range(n_lane_tiles):
            ubuf[c, pl.ds(b, tt, stride=batch), :] = u_ref[b, :, c * 128:(c + 1) * 128]
    u = jnp.concatenate([ubuf[c] for c in range(n_lane_tiles)], axis=1)
    ub = u.astype(BF16)
    lo, hi = _ssm_drive(ub, bbre_ref)
    xre[:, :half_s] = lo
    xre[:, half_s:] = hi
    lo, hi = _ssm_drive(ub, bbim_ref)
    xim[:, :half_s] = lo
    xim[:, half_s:] = hi

    first_step = lax.broadcasted_iota(jnp.int32, (8, lane_chunk), 0) < batch
    for c0 in range(0, SSM_FLAT, lane_chunk):
        cols = pl.ds(c0, lane_chunk)
        ar = jnp.broadcast_to(are_ref[:, cols], (8, lane_chunk))
        ai = jnp.broadcast_to(aim_ref[:, cols], (8, lane_chunk))

        def step(i, carry, cols=cols, ar=ar, ai=ai):
            c_r, c_i = carry
            r0 = pl.multiple_of(i * 8, 8)
            br = xre[pl.ds(r0, 8), cols]
            bi = xim[pl.ds(r0, 8), cols]
            p_r = pltpu.roll(c_r, batch, 0)
            p_i = pltpu.roll(c_i, batch, 0)
            y_r = ar * p_r - ai * p_i + br
            y_i = ar * p_i + ai * p_r + bi
            q_r = pltpu.roll(y_r, batch, 0)
            q_i = pltpu.roll(y_i, batch, 0)
            z_r = ar * q_r - ai * q_i + br
            z_i = ar * q_i + ai * q_r + bi
            xre[pl.ds(r0, 8), cols] = jnp.where(first_step, y_r, z_r)
            xim[pl.ds(r0, 8), cols] = jnp.where(first_step, y_i, z_i)
            return z_r, z_i

        c_r, c_i = lax.fori_loop(0, rows // 8, step, (cre[:, cols], cim[:, cols]))
        cre[:, cols] = c_r
        cim[:, cols] = c_i

    g = _ssm_readout(xre[...], xim[...], ccre_ref, ccim_ref, d_ref, u)
    for c in range(n_lane_tiles):
        gbuf[c] = g[:, c * 128:(c + 1) * 128]
    for b in range(batch):
        g_ref[b] = jnp.concatenate([gbuf[c, pl.ds(b, tt, stride=batch), :] for c in range(n_lane_tiles)],
                                   axis=1).astype(BF16)
    sre_ref[...] = cre[...]
    sim_ref[...] = cim[...]


def _ssm_prompt(u, ssm, tt=128, lane_chunk=512):
    a_re, a_im, bb_re, bb_im, cc_re, cc_im, d_skip = ssm
    batch, t_len, _ = u.shape
    assert 2 * batch == 8, "one sublane tile must hold exactly two time steps"
    rows = tt * batch
    const = lambda j: (0, 0)
    full = lambda a: pl.BlockSpec(a.shape, const)
    seq_blk = pl.BlockSpec((batch, tt, SSM_WIDTH), lambda j: (0, j, 0))
    g, s_re, s_im = pl.pallas_call(
        functools.partial(_ssm_prompt_kernel, lane_chunk=lane_chunk),
        out_shape=(jax.ShapeDtypeStruct((batch, t_len, SSM_WIDTH), BF16),
                   jax.ShapeDtypeStruct((8, SSM_FLAT), F32),
                   jax.ShapeDtypeStruct((8, SSM_FLAT), F32)),
        grid=(t_len // tt,),
        in_specs=[seq_blk,
                  full(bb_re), full(bb_im), full(a_re), full(a_im), full(cc_re), full(cc_im), full(d_skip)],
        out_specs=(seq_blk, pl.BlockSpec((8, SSM_FLAT), const), pl.BlockSpec((8, SSM_FLAT), const)),
        scratch_shapes=[pltpu.VMEM((SSM_WIDTH // 128, rows, 128), F32),
                        pltpu.VMEM((SSM_WIDTH // 128, rows, 128), F32),
                        pltpu.VMEM((rows, SSM_FLAT), F32), pltpu.VMEM((rows, SSM_FLAT), F32),
                        pltpu.VMEM((8, SSM_FLAT), F32), pltpu.VMEM((8, SSM_FLAT), F32)],
        compiler_params=_cparams(("arbitrary",)),
        name="ssm_prompt",
    )(u, bb_re, bb_im, a_re, a_im, cc_re, cc_im, d_skip)
    return g, s_re[batch:], s_im[batch:]


def _ssm_sample_kernel(u_ref, s0re_ref, s0im_ref, bbre_ref, bbim_ref, are_ref, aim_ref,
                       ccre_ref, ccim_ref, d_ref, g_ref, sre_ref, sim_ref):
    u = u_ref[...]
    ub = u.astype(BF16)
    ar, ai = are_ref[...], aim_ref[...]
    s_r, s_i = s0re_ref[...], s0im_ref[...]
    lo, hi = _ssm_drive(ub, bbre_ref)
    x_r = ar * s_r - ai * s_i + jnp.concatenate([lo, hi], axis=1)
    lo, hi = _ssm_drive(ub, bbim_ref)
    x_i = ar * s_i + ai * s_r + jnp.concatenate([lo, hi], axis=1)
    sre_ref[...] = x_r
    sim_ref[...] = x_i
    g_ref[...] = _ssm_readout(x_r, x_i, ccre_ref, ccim_ref, d_ref, u).astype(BF16)


def _ssm_sample(u, s_re, s_im, ssm):
    a_re, a_im, bb_re, bb_im, cc_re, cc_im, d_skip = ssm
    n = u.shape[0]
    return pl.pallas_call(
        _ssm_sample_kernel,
        out_shape=(jax.ShapeDtypeStruct((n, SSM_WIDTH), BF16),
                   jax.ShapeDtypeStruct((n, SSM_FLAT), F32),
                   jax.ShapeDtypeStruct((n, SSM_FLAT), F32)),
        compiler_params=pltpu.CompilerParams(vmem_limit_bytes=VMEM_LIMIT),
        name="ssm_sample",
    )(u, s_re, s_im, bb_re, bb_im, a_re, a_im, cc_re, cc_im, d_skip)


def _log_sigmoid_pair(z2):
    sign = jnp.uint32(0x80000000)
    neg_abs = lax.bitcast_convert_type(lax.bitcast_convert_type(z2, jnp.uint32) | sign, F32)
    soft = jnp.log2(1.0 + jnp.exp2(neg_abs))
    lsig = jnp.minimum(z2, 0.0) - soft
    return lsig, lsig - z2


def _attn_prompt_kernel(bias_ref, q_ref, k_ref, v_ref, tri_ref, o_ref, acc_ref, run_ref, *, tq, tk):
    qi = pl.program_id(1)
    nh = HEADS_PER_STEP
    ngrp = SB_HEADS // nh
    lane_head = lax.broadcasted_iota(jnp.int32, (1, HEAD_LANES), 1) // SB_HEAD_DIM
    head_mask = [(lane_head == h).astype(F32).astype(BF16) for h in range(nh)]
    qs, bias = [], []
    for g in range(ngrp):
        q = q_ref[:, g * HEAD_LANES:(g + 1) * HEAD_LANES]
        qs.append(jnp.concatenate([q * head_mask[h] for h in range(nh)], axis=0))
        bias.append([bias_ref[g * nh + h] for h in range(nh)])
    tri = tri_ref[...]
    acc_ref[...] = jnp.zeros_like(acc_ref)
    run_ref[...] = jnp.zeros_like(run_ref)

    def block(js, masked):
        if masked:
            k0 = pl.multiple_of(js[0] * tk, tk)
            kpos = k0 + lax.broadcasted_iota(jnp.int32, (nh * tq, tk), 1)
            qpos = qi * tq + (lax.broadcasted_iota(jnp.int32, (nh * tq, tk), 0) & (tq - 1))
            valid = kpos < qpos
        for g in range(ngrp):
            lanes = slice(g * HEAD_LANES, (g + 1) * HEAD_LANES)
            run = run_ref[g]
            pv = None
            for j in js:
                k0 = pl.multiple_of(j * tk, tk)
                zs = _dot_t(qs[g], k_ref[pl.ds(k0, tk), lanes])
                lsig_parts, stay_parts = [], []
                for h in range(nh):
                    lsig, stay = _log_sigmoid_pair(zs[h * tq:(h + 1) * tq] + bias[g][h])
                    lsig_parts.append(lsig)
                    stay_parts.append(stay)
                lsig = jnp.concatenate(lsig_parts, axis=0)
                stay = jnp.concatenate(stay_parts, axis=0)
                if masked:
                    stay = jnp.where(valid, stay, 0.0)
                after = _dot(stay.astype(BF16), tri)
                w = jnp.exp2(lsig + after + run)
                if masked:
                    w = jnp.where(valid, w, 0.0)
                part = _dot(w.astype(BF16), v_ref[pl.ds(k0, tk), lanes])
                pv = part if pv is None else pv + part
                run = run + (after[:, 0:1] + stay[:, 0:1])
            acc_ref[g] += pv
            run_ref[g] = run

    j_diag = (qi * tq) // tk
    block((j_diag,), True)

    @pl.when(j_diag % 2 == 1)
    def _():
        block((j_diag - 1,), False)

    def body(jj, carry):
        j = 2 * (j_diag // 2 - jj) - 1
        block((j, j - 1), False)
        return carry

    lax.fori_loop(0, j_diag // 2, body, 0)
    for g in range(ngrp):
        acc = acc_ref[g]
        out = jnp.zeros((tq, HEAD_LANES), F32)
        for h in range(nh):
            out = jnp.where(lane_head == h, acc[h * tq:(h + 1) * tq], out)
        o_ref[:, g * HEAD_LANES:(g + 1) * HEAD_LANES] = out.astype(BF16)


def _strict_upper(tk):
    idx = jnp.arange(tk)
    return (idx[:, None] > idx[None, :]).astype(BF16)


def _attn_prompt(q, kb, vb, sb_bias, batch, t_len, tq=256, tk=256):
    assert tq & (tq - 1) == 0 and tq <= tk, "one masked key block must cover the diagonal"
    n = q.shape[0]
    nq = t_len // tq
    ngrp = SB_HEADS // HEADS_PER_STEP
    return pl.pallas_call(
        functools.partial(_attn_prompt_kernel, tq=tq, tk=tk),
        out_shape=jax.ShapeDtypeStruct((n, SB_WIDTH), BF16),
        grid=(batch, nq),
        in_specs=[pl.BlockSpec(memory_space=pltpu.SMEM),
                  pl.BlockSpec((tq, SB_WIDTH), lambda b, i: (b * nq + i, 0)),
                  pl.BlockSpec((t_len, SB_WIDTH), lambda b, i: (b, 0)),
                  pl.BlockSpec((t_len, SB_WIDTH), lambda b, i: (b, 0)),
                  pl.BlockSpec((tk, tk), lambda b, i: (0, 0))],
        out_specs=pl.BlockSpec((tq, SB_WIDTH), lambda b, i: (b * nq + i, 0)),
        scratch_shapes=[pltpu.VMEM((ngrp, HEADS_PER_STEP * tq, HEAD_LANES), F32),
                        pltpu.VMEM((ngrp, HEADS_PER_STEP * tq, 1), F32)],
        compiler_params=_cparams(("parallel", "arbitrary")),
        name="attn_prompt",
    )(sb_bias.astype(F32) * LOG2E, q, kb, vb, _strict_upper(tk))


def _attn_sample_kernel(pt_ref, q_ref, bias_ref, tri_ref, later_ref, *refs, n_pages):
    del pt_ref
    k_refs = refs[:n_pages]
    v_refs = refs[n_pages:2 * n_pages]
    o_ref = refs[2 * n_pages]
    hd = (SB_HEADS, SB_HEAD_DIM, PAGE_SIZE)
    q = q_ref[0].astype(F32)
    q_col = jnp.broadcast_to(q, (PAGE_SIZE, SB_WIDTH)).T
    q3 = q_col.reshape(hd)
    z = jnp.concatenate([jnp.sum(k_refs[p][0].reshape(hd) * q3, axis=1) for p in range(n_pages)], axis=0)
    lsig, stay = _log_sigmoid_pair(z + bias_ref[...])
    after = _dot(stay.astype(BF16), tri_ref[...])
    total = jnp.broadcast_to(after[:, 0:1] + stay[:, 0:1], stay.shape)
    t_hi, t_lo = _split_bf16(total)
    later = later_ref[...]
    run = _dot(later, t_hi) + _dot(later, t_lo)
    w = jnp.exp2(lsig + after + run)
    acc = [jnp.zeros((SB_HEAD_DIM, PAGE_SIZE), F32) for _ in range(SB_HEADS)]
    for p in range(n_pages):
        for h in range(SB_HEADS):
            r = p * SB_HEADS + h
            acc[h] = acc[h] + v_refs[p][0, h * SB_HEAD_DIM:(h + 1) * SB_HEAD_DIM, :] * w[r:r + 1, :]
    a_hi, a_lo = _split_bf16(jnp.concatenate(acc, axis=0))
    ones = jnp.ones((8, PAGE_SIZE), BF16)
    o_ref[0] = (_dot_t(ones, a_hi) + _dot_t(ones, a_lo))[0:1].astype(BF16)


def _attn_sample(q, cache_k, cache_v, page_table, sb_bias):
    n, n_pages = page_table.shape
    n_pool = cache_k.shape[0]
    ck = cache_k.transpose(0, 2, 3, 1).reshape(n_pool, SB_WIDTH, PAGE_SIZE)
    cv = cache_v.transpose(0, 2, 3, 1).reshape(n_pool, SB_WIDTH, PAGE_SIZE)
    page_spec = lambda p: pl.BlockSpec((1, SB_WIDTH, PAGE_SIZE), lambda i, pt, p=p: (pt[i * n_pages + p], 0, 0))
    tok_spec = pl.BlockSpec((1, 1, SB_WIDTH), lambda i, pt: (i, 0, 0))
    rows = n_pages * SB_HEADS
    const = lambda shape: pl.BlockSpec(shape, lambda i, pt: (0, 0))
    r = jnp.arange(rows)
    later = ((r[:, None] % SB_HEADS == r[None, :] % SB_HEADS)
             & (r[None, :] // SB_HEADS > r[:, None] // SB_HEADS)).astype(BF16)
    out = pl.pallas_call(
        functools.partial(_attn_sample_kernel, n_pages=n_pages),
        out_shape=jax.ShapeDtypeStruct((n, 1, SB_WIDTH), BF16),
        grid_spec=pltpu.PrefetchScalarGridSpec(
            num_scalar_prefetch=1,
            grid=(n,),
            in_specs=[tok_spec, const((rows, 1)), const((PAGE_SIZE, PAGE_SIZE)), const((rows, rows))]
                     + [page_spec(p) for p in range(n_pages)] * 2,
            out_specs=tok_spec),
        compiler_params=_cparams(("parallel",)),
        name="attn_sample",
    )(page_table.reshape(-1).astype(jnp.int32), q.reshape(n, 1, SB_WIDTH),
      jnp.tile(sb_bias.astype(F32) * LOG2E, n_pages).reshape(rows, 1), _strict_upper(PAGE_SIZE), later,
      *([ck] * n_pages), *([cv] * n_pages))
    return out.reshape(n, SB_WIDTH)


def _first_index_of_max(vals, row_id, n_rows):
    m = jnp.max(vals, axis=0, keepdims=True)
    first = jnp.min(jnp.where(vals == m, row_id, n_rows), axis=0, keepdims=True)
    return m, first


def _route(logits_t, rbias):
    tm = logits_t.shape[1]
    scores = jax.nn.sigmoid(logits_t)
    biased = scores + rbias
    row8 = lax.broadcasted_iota(jnp.int32, (GROUP_SIZE, tm), 0)
    neg = jnp.float32(-jnp.inf)
    grp_score = []
    for g in range(N_GROUPS):
        blk = biased[g * GROUP_SIZE:(g + 1) * GROUP_SIZE]
        m1, first = _first_index_of_max(blk, row8, GROUP_SIZE)
        m2 = jnp.max(jnp.where(row8 == first, neg, blk), axis=0, keepdims=True)
        grp_score.append(m1 + m2)
    masked = []
    for g in range(N_GROUPS):
        beaten_by = jnp.zeros((1, tm), jnp.int32)
        for o in range(N_GROUPS):
            if o == g:
                continue
            wins = (grp_score[o] > grp_score[g]) if o > g else (grp_score[o] >= grp_score[g])
            beaten_by = beaten_by + wins.astype(jnp.int32)
        keep = beaten_by < TOPK_GROUPS
        masked.append(jnp.where(keep, biased[g * GROUP_SIZE:(g + 1) * GROUP_SIZE], neg))
    cur = jnp.concatenate(masked, axis=0)
    row = lax.broadcasted_iota(jnp.int32, (N_EXPERTS, tm), 0)
    chosen = jnp.zeros((N_EXPERTS, tm), jnp.bool_)
    for _ in range(TOP_K):
        _, first = _first_index_of_max(cur, row, N_EXPERTS)
        pick = row == first
        chosen = jnp.logical_or(chosen, pick)
        cur = jnp.where(pick, neg, cur)
    sel = jnp.where(chosen, scores, 0.0)
    return sel / jnp.sum(sel, axis=0, keepdims=True) * ROUTED_SCALE


def _post_kernel(x_ref, g_ref, o_ref, sga_ref, sgb_ref, gate1_ref, shift2_ref, scale2_ref, g2_ref,
                 wga_ref, wgb_ref, wsb_ref, wout_ref, wr_ref, rb_ref,
                 x1_ref, h2_ref, gates_ref):
    g = g_ref[...]
    branch_a = _dot(g, wga_ref[...]) * jax.nn.sigmoid(_dot(g, wgb_ref[...]))
    branch_b = _dot(o_ref[...], wsb_ref[...])
    merged = sga_ref[...].astype(F32) * branch_a + sgb_ref[...].astype(F32) * branch_b
    x1 = x_ref[...] + gate1_ref[...] * _dot(merged.astype(BF16), wout_ref[...])
    x1_ref[...] = x1
    r = lax.rsqrt(jnp.mean(x1 * x1, axis=-1, keepdims=True) + RMS_EPS)
    h2 = (x1 * r) * g2_ref[...]
    h2 = h2 * (1.0 + scale2_ref[...]) + shift2_ref[...]
    h2_ref[...] = h2.astype(BF16)
    h_hi, h_lo = _split_bf16(h2)
    w_hi, w_lo = _split_bf16(wr_ref[...])
    logits_t = _dot_t(w_hi, h_hi) + _dot_t(w_hi, h_lo) + _dot_t(w_lo, h_hi)
    gates_ref[...] = _route(logits_t, rb_ref[...])


def _post_mixer(x, g, o, sga, sgb, mod, per_token, tm, p):
    n = x.shape[0]
    tiles_per_mod = (n // mod.shape[0]) // tm if not per_token else 1
    row = lambda i: (i, 0)
    const = lambda i: (0, 0)
    blk = lambda w: pl.BlockSpec((tm, w), row)
    full = lambda a: pl.BlockSpec(a.shape, const)
    ms = lambda c: _mod_spec(per_token, tm, tiles_per_mod, c)
    weights = (p["w_glu_a"], p["w_glu_b"], p["w_sb_out"], p["w_out"], p["w_router_t"], p["router_bias"])
    return pl.pallas_call(
        _post_kernel,
        out_shape=(jax.ShapeDtypeStruct((n, D_MODEL), F32),
                   jax.ShapeDtypeStruct((n, D_MODEL), BF16),
                   jax.ShapeDtypeStruct((N_EXPERTS, n), F32)),
        grid=(n // tm,),
        in_specs=[blk(D_MODEL), blk(512), blk(512), blk(1024), blk(1024), ms(2), ms(3), ms(4),
                  pl.BlockSpec((1, D_MODEL), const)] + [full(w) for w in weights],
        out_specs=(blk(D_MODEL), blk(D_MODEL), pl.BlockSpec((N_EXPERTS, tm), lambda i: (0, i))),
        compiler_params=_cparams(("parallel",)),
        name="post_mixer",
    )(x, g, o, sga, sgb, mod, mod, mod, p["norm2_g"], *weights)


def _moe_kernel(h_ref, gates_ref, x1_ref, gate2_ref, wgs_ref, wus_ref, wds_ref,
                wg_ref, wu_ref, wd_ref, y_ref, acc_ref):
    j = pl.program_id(1)
    h = h_ref[...]

    @pl.when(j == 0)
    def _():
        act = _silu(_dot(h, wgs_ref[...])) * _dot(h, wus_ref[...])
        acc_ref[...] = _dot(act.astype(BF16), wds_ref[...])

    gates = gates_ref[...]
    lane = lax.broadcasted_iota(jnp.int32, gates.shape, 1)
    scaled = []
    for s in range(EXPERTS_PER_STEP):
        act = _silu(_dot(h, wg_ref[s])) * _dot(h, wu_ref[s])
        gcol = jnp.sum(jnp.where(lane == j * EXPERTS_PER_STEP + s, gates, 0.0), axis=1, keepdims=True)
        scaled.append((act * gcol).astype(BF16))
    acc_ref[...] += _dot(jnp.concatenate(scaled, axis=1), wd_ref[...])

    @pl.when(j == pl.num_programs(1) - 1)
    def _():
        y_ref[...] = x1_ref[...] + gate2_ref[...] * acc_ref[...]


def _moe(h2, gates, x1, mod, per_token, tm, p):
    n = h2.shape[0]
    tiles_per_mod = (n // mod.shape[0]) // tm if not per_token else 1
    row = lambda i, e: (i, 0)
    const = lambda i, e: (0, 0)
    exp = lambda i, e: (e, 0, 0)
    if per_token:
        gate2_spec = pl.BlockSpec((tm, D_MODEL), lambda i, e: (i, 5))
    else:
        gate2_spec = pl.BlockSpec((None, 1, D_MODEL), lambda i, e: (i // tiles_per_mod, 0, 5))
    wide = EXPERTS_PER_STEP * EXPERT_HIDDEN
    return pl.pallas_call(
        _moe_kernel,
        out_shape=jax.ShapeDtypeStruct((n, D_MODEL), F32),
        grid=(n // tm, N_EXPERTS // EXPERTS_PER_STEP),
        in_specs=[pl.BlockSpec((tm, D_MODEL), row),
                  pl.BlockSpec((tm, N_EXPERTS), row),
                  pl.BlockSpec((tm, D_MODEL), row),
                  gate2_spec,
                  pl.BlockSpec((D_MODEL, EXPERT_HIDDEN), const),
                  pl.BlockSpec((D_MODEL, EXPERT_HIDDEN), const),
                  pl.BlockSpec((EXPERT_HIDDEN, D_MODEL), const),
                  pl.BlockSpec((EXPERTS_PER_STEP, D_MODEL, EXPERT_HIDDEN), exp),
                  pl.BlockSpec((EXPERTS_PER_STEP, D_MODEL, EXPERT_HIDDEN), exp),
                  pl.BlockSpec((None, wide, D_MODEL), exp)],
        out_specs=pl.BlockSpec((tm, D_MODEL), row),
        scratch_shapes=[pltpu.VMEM((tm, D_MODEL), F32)],
        compiler_params=_cparams(("parallel", "arbitrary")),
        name="moe",
    )(h2, gates, x1, mod, p["w_gate_s"], p["w_up_s"], p["w_down_s"], p["w_gate_e"], p["w_up_e"], p["w_down_e"])


def _dispatch_kernel(gates_ref, h_ref, tri_ref, xs_ref, flag_ref):
    g = gates_ref[...]
    sel = g > 0.0
    rank = _dot(sel.astype(F32).astype(BF16), tri_ref[...])
    rank = jnp.where(sel, rank, -1.0)
    over = jnp.where(rank >= ROUTE_CAP, 1.0, 0.0)
    worst = jnp.max(jnp.max(over, axis=1, keepdims=True), axis=0, keepdims=True)
    flag_ref[...] = jnp.broadcast_to(worst, flag_ref.shape)
    slot = lax.broadcasted_iota(jnp.int32, (ROUTE_CAP, g.shape[1]), 0).astype(F32)
    h = h_ref[...]
    per_dot = 8
    for e0 in range(0, N_EXPERTS, per_dot):
        place = jnp.concatenate([(slot == rank[e:e + 1, :]).astype(F32).astype(BF16)
                                 for e in range(e0, e0 + per_dot)], axis=0)
        xs_ref[e0 * ROUTE_CAP:(e0 + per_dot) * ROUTE_CAP, :] = _dot(place, h).astype(BF16)


def _expert_kernel(x_ref, wg_ref, wu_ref, wd_ref, y_ref):
    nb, cap, d = x_ref.shape
    x = x_ref[...].reshape(nb * cap, d)
    n_part = 2 if nb % 2 == 0 else 1
    rows = nb * cap // n_part
    for r in range(n_part):
        xr = x[r * rows:(r + 1) * rows]
        act = _silu(_dot(xr, wg_ref[...])) * _dot(xr, wu_ref[...])
        y = _dot(act.astype(BF16), wd_ref[...]).astype(BF16)
        y_ref[r * (nb // n_part):(r + 1) * (nb // n_part)] = y.reshape(nb // n_part, cap, d)


def _combine_kernel(gates_ref, ys_ref, h_ref, x1_ref, gate2_ref, wgs_ref, wus_ref, wds_ref,
                    tri_ref, expand_ref, y_ref):
    g = gates_ref[...]
    sel = g > 0.0
    rank = _dot(tri_ref[...], sel.astype(F32).astype(BF16))
    rank = jnp.where(sel, rank, -1.0)
    expand = expand_ref[...]
    rank_of_slot = _dot(rank.astype(BF16), expand)
    slot = (lax.broadcasted_iota(jnp.int32, rank_of_slot.shape, 1) & (ROUTE_CAP - 1)).astype(F32)
    hit = slot == rank_of_slot
    gate_of_slot = _dot(g.astype(BF16), expand)
    routed = _dot(jnp.where(hit, gate_of_slot, 0.0).astype(BF16), ys_ref[...])
    h = h_ref[...]
    act = _silu(_dot(h, wgs_ref[...])) * _dot(h, wus_ref[...])
    shared = _dot(act.astype(BF16), wds_ref[...])
    y_ref[...] = x1_ref[...] + gate2_ref[...] * (shared + routed)


def _moe_routed(h2, gates_t, x1, mod, p):
    assert ROUTE_CAP & (ROUTE_CAP - 1) == 0
    n = h2.shape[0]
    bt = ROUTE_BLOCK
    nb = n // bt
    slots = N_EXPERTS * ROUTE_CAP
    tiles_per_mod = (n // mod.shape[0]) // bt
    later = _strict_upper(bt)
    const2 = lambda b: (0, 0)
    xs, flags = pl.pallas_call(
        _dispatch_kernel,
        out_shape=(jax.ShapeDtypeStruct((nb, slots, D_MODEL), BF16),
                   jax.ShapeDtypeStruct((nb, 8, 128), F32)),
        grid=(nb,),
        in_specs=[pl.BlockSpec((N_EXPERTS, bt), lambda b: (0, b)),
                  pl.BlockSpec((bt, D_MODEL), lambda b: (b, 0)),
                  pl.BlockSpec((bt, bt), const2)],
        out_specs=(pl.BlockSpec((None, slots, D_MODEL), lambda b: (b, 0, 0)),
                   pl.BlockSpec((None, 8, 128), lambda b: (b, 0, 0))),
        compiler_params=_cparams(("parallel",)),
        name="moe_dispatch",
    )(gates_t, h2, later.T)
    gates = gates_t.T

    n_half = 2
    xblk = pl.BlockSpec((nb // n_half, None, ROUTE_CAP, D_MODEL), lambda e, f: (f, e, 0, 0))
    ys = pl.pallas_call(
        _expert_kernel,
        out_shape=jax.ShapeDtypeStruct((nb, N_EXPERTS, ROUTE_CAP, D_MODEL), BF16),
        grid=(N_EXPERTS, n_half),
        in_specs=[xblk,
                  pl.BlockSpec((None, D_MODEL, EXPERT_HIDDEN), lambda e, f: (e, 0, 0)),
                  pl.BlockSpec((None, D_MODEL, EXPERT_HIDDEN), lambda e, f: (e, 0, 0)),
                  pl.BlockSpec((None, EXPERT_HIDDEN, D_MODEL), lambda e, f: (e, 0, 0))],
        out_specs=xblk,
        compiler_params=_cparams(("parallel", "arbitrary")),
        name="moe_experts",
    )(xs.reshape(nb, N_EXPERTS, ROUTE_CAP, D_MODEL), p["w_gate_e"], p["w_up_e"],
      p["w_down_e"].reshape(N_EXPERTS, EXPERT_HIDDEN, D_MODEL))
    expand = (jnp.arange(slots)[None, :] // ROUTE_CAP == jnp.arange(N_EXPERTS)[:, None]).astype(BF16)
    blk = lambda w: pl.BlockSpec((bt, w), lambda b: (b, 0))
    full = lambda a: pl.BlockSpec(a.shape, const2)
    y = pl.pallas_call(
        _combine_kernel,
        out_shape=jax.ShapeDtypeStruct((n, D_MODEL), F32),
        grid=(nb,),
        in_specs=[blk(N_EXPERTS),
                  pl.BlockSpec((None, slots, D_MODEL), lambda b: (b, 0, 0)),
                  blk(D_MODEL), blk(D_MODEL), _mod_spec(False, bt, tiles_per_mod, 5),
                  full(p["w_gate_s"]), full(p["w_up_s"]), full(p["w_down_s"]), full(later), full(expand)],
        out_specs=blk(D_MODEL),
        compiler_params=_cparams(("parallel",)),
        name="moe_combine",
    )(gates, ys.reshape(nb, slots, D_MODEL), h2, x1, mod, p["w_gate_s"], p["w_up_s"], p["w_down_s"],
      later, expand)

    over = flags[:, 0, 0] > 0.0
    n_fix = jnp.sum(over.astype(jnp.int32))
    fix_first = jnp.argsort(jnp.logical_not(over), stable=True).astype(jnp.int32)
    case = (n_fix > 0).astype(jnp.int32) + (n_fix > MAX_FIX_BLOCKS).astype(jnp.int32)
    return lax.switch(case, [lambda: y,
                             lambda: _moe_fix(y, fix_first, n_fix.reshape(1), h2, gates, x1, mod, p),
                             lambda: _moe(h2, gates, x1, mod, False, 1024, p)])


def _moe_fix_kernel(blocks_ref, n_fix_ref, h_ref, gates_ref, x1_ref, gate2_ref, wgs_ref, wus_ref, wds_ref,
                    wg_ref, wu_ref, wd_ref, y_in_ref, y_ref, acc_ref):
    del blocks_ref, y_in_ref

    @pl.when(pl.program_id(0) < n_fix_ref[0])
    def _():
        _moe_kernel(h_ref, gates_ref, x1_ref, gate2_ref, wgs_ref, wus_ref, wds_ref,
                    wg_ref, wu_ref, wd_ref, y_ref, acc_ref)


def _moe_fix(y, blocks, n_fix, h2, gates, x1, mod, p):
    n = h2.shape[0]
    bt = ROUTE_BLOCK
    nb = n // bt
    n_steps = N_EXPERTS // EXPERTS_PER_STEP
    tiles_per_mod = (n // mod.shape[0]) // bt
    wide = EXPERTS_PER_STEP * EXPERT_HIDDEN

    def tile(i, bl, nf):
        return bl[jnp.minimum(i, jnp.maximum(nf[0] - 1, 0))]

    row = lambda i, j, bl, nf: (tile(i, bl, nf), 0)
    const = lambda i, j, bl, nf: (0, 0)
    exp = lambda i, j, bl, nf: (jnp.where(i < nf[0], j, n_steps - 1), 0, 0)
    return pl.pallas_call(
        _moe_fix_kernel,
        out_shape=jax.ShapeDtypeStruct((n, D_MODEL), F32),
        grid_spec=pltpu.PrefetchScalarGridSpec(
            num_scalar_prefetch=2,
            grid=(min(nb, MAX_FIX_BLOCKS), n_steps),
            in_specs=[pl.BlockSpec((bt, D_MODEL), row),
                      pl.BlockSpec((bt, N_EXPERTS), row),
                      pl.BlockSpec((bt, D_MODEL), row),
                      pl.BlockSpec((None, 1, D_MODEL), lambda i, j, bl, nf: (tile(i, bl, nf) // tiles_per_mod, 0, 5)),
                      pl.BlockSpec((D_MODEL, EXPERT_HIDDEN), const),
                      pl.BlockSpec((D_MODEL, EXPERT_HIDDEN), const),
                      pl.BlockSpec((EXPERT_HIDDEN, D_MODEL), const),
                      pl.BlockSpec((EXPERTS_PER_STEP, D_MODEL, EXPERT_HIDDEN), exp),
                      pl.BlockSpec((EXPERTS_PER_STEP, D_MODEL, EXPERT_HIDDEN), exp),
                      pl.BlockSpec((None, wide, D_MODEL), exp),
                      pl.BlockSpec(memory_space=pl.ANY)],
            out_specs=pl.BlockSpec((bt, D_MODEL), row),
            scratch_shapes=[pltpu.VMEM((bt, D_MODEL), F32)]),
        input_output_aliases={12: 0},
        compiler_params=_cparams(("arbitrary", "arbitrary")),
        name="moe_fix",
    )(blocks, n_fix, h2, gates, x1, mod, p["w_gate_s"], p["w_up_s"], p["w_down_s"],
      p["w_gate_e"], p["w_up_e"], p["w_down_e"], y)


def _layer(x_prompt, x_sample, cache_k, cache_v, s0_re, s0_im, page_table, c_prompt, c_sample, p):
    batch, t_len, d = x_prompt.shape
    n_s = x_sample.shape[0]
    n_p = batch * t_len

    c_all = jnp.concatenate([c_prompt, c_sample], axis=0)
    pad = (-c_all.shape[0]) % 8
    c_all = jnp.pad(c_all, ((0, pad), (0, 0)))
    mod = _ada_mod(c_all, p["w_ada"], p["b_ada"])
    mod_p = mod[:batch].reshape(batch, 1, 6 * d)
    mod_s = mod[batch:batch + n_s]

    a_re, a_im, bb_re, bb_im = _ssm_prep(p["lam_re"], p["lam_im"], p["log_dt"], p["b_re"], p["b_im"])
    ssm = (a_re, a_im, bb_re, bb_im, _blockdiag_out(p["c_re"]), _blockdiag_out(-p["c_im"]),
           p["d_skip"].astype(F32).reshape(1, SSM_WIDTH))

    xp = x_prompt.reshape(n_p, d)
    xs = x_sample.reshape(n_s, d)
    proj_p = _in_proj(xp, mod_p, False, 512, t_len, p["norm1_g"], p["w_in"], p["q_norm_g"], p["k_norm_g"])
    proj_s = _in_proj(xs, mod_s, True, n_s, n_s, p["norm1_g"], p["w_in"], p["q_norm_g"], p["k_norm_g"])
    u_p, q_p, k_p, kb_p, v_p, vb_p, sga_p, sgb_p = proj_p
    u_s, q_s, k_s, _, v_s, _, sga_s, sgb_s = proj_s

    g_p, st_re_p, st_im_p = _ssm_prompt(u_p.reshape(batch, t_len, SSM_WIDTH), ssm)
    g_p = g_p.reshape(n_p, SSM_WIDTH)
    g_s, st_re_s, st_im_s = _ssm_sample(u_s, s0_re.reshape(n_s, SSM_FLAT), s0_im.reshape(n_s, SSM_FLAT), ssm)

    o_p = _attn_prompt(q_p, kb_p, vb_p, p["sb_bias"], batch, t_len)
    o_s = _attn_sample(q_s, cache_k, cache_v, page_table, p["sb_bias"])

    x1_p, h2_p, gates_p = _post_mixer(xp, g_p, o_p, sga_p, sgb_p, mod_p, False, 512, p)
    x1_s, h2_s, gates_s = _post_mixer(xs, g_s, o_s, sga_s, sgb_s, mod_s, True, n_s, p)

    y_p = _moe_routed(h2_p, gates_p, x1_p, mod_p, p)
    y_s = _moe(h2_s, gates_s.T, x1_s, mod_s, True, n_s, p)

    hd = (SB_HEADS, SB_HEAD_DIM)
    gs = (SSM_GROUPS, SSM_STATE)

    def kv_out(a_t, n_seq, seq_len):
        return a_t.reshape(n_seq, *hd, seq_len).transpose(0, 3, 1, 2)

    return (y_p.reshape(batch, t_len, d), y_s.reshape(n_s, 1, d),
            kv_out(k_p, batch, t_len), kv_out(v_p, batch, t_len),
            st_re_p.reshape(batch, *gs), st_im_p.reshape(batch, *gs),
            kv_out(k_s, 1, n_s).reshape(n_s, 1, *hd), kv_out(v_s, 1, n_s).reshape(n_s, 1, *hd),
            st_re_s.reshape(n_s, *gs), st_im_s.reshape(n_s, *gs))


def kernel(x_prompt, x_sample, cache_k, cache_v, state_ssm_re, state_ssm_im, page_table, c_prompt, c_sample,
           w_ada, b_ada, norm1_g, norm2_g, w_in, q_norm_g, k_norm_g, sb_bias, ssm_lambda_re, ssm_lambda_im,
           ssm_log_dt, ssm_b_re, ssm_b_im, ssm_c_re, ssm_c_im, ssm_d, w_glu_a, w_glu_b, w_sb_out, w_out,
           w_router, router_bias, w_gate_e, w_up_e, w_down_e, w_gate_s, w_up_s, w_down_s):
    depth = w_in.shape[0]
    assert depth == 1, "single-layer step"
    l = 0
    p = dict(
        w_ada=w_ada[l], b_ada=b_ada[l],
        norm1_g=norm1_g[l], norm2_g=norm2_g[l].reshape(1, D_MODEL),
        w_in=w_in[l].astype(BF16), q_norm_g=q_norm_g[l], k_norm_g=k_norm_g[l], sb_bias=sb_bias[l],
        lam_re=ssm_lambda_re[l], lam_im=ssm_lambda_im[l], log_dt=ssm_log_dt[l],
        b_re=ssm_b_re[l], b_im=ssm_b_im[l], c_re=ssm_c_re[l], c_im=ssm_c_im[l], d_skip=ssm_d[l],
        w_glu_a=w_glu_a[l].astype(BF16), w_glu_b=w_glu_b[l].astype(BF16),
        w_sb_out=w_sb_out[l].astype(BF16), w_out=w_out[l].astype(BF16),
        w_router_t=w_router[l].T, router_bias=router_bias[l].reshape(N_EXPERTS, 1),
        w_gate_e=w_gate_e[l].astype(BF16), w_up_e=w_up_e[l].astype(BF16),
        w_down_e=w_down_e[l].astype(BF16).reshape(N_EXPERTS // EXPERTS_PER_STEP,
                                                   EXPERTS_PER_STEP * EXPERT_HIDDEN, D_MODEL),
        w_gate_s=w_gate_s[l].astype(BF16), w_up_s=w_up_s[l].astype(BF16), w_down_s=w_down_s[l].astype(BF16),
    )
    outs = _layer(x_prompt, x_sample, cache_k[l], cache_v[l], state_ssm_re[l], state_ssm_im[l],
                  page_table, c_prompt, c_sample, p)
    y_p, y_s = outs[0], outs[1]
    return (y_p, y_s) + tuple(o[None] for o in outs[2:])
```

```python
import functools
import math

import jax
import jax.numpy as jnp
from jax import lax
from jax.experimental import pallas as pl
from jax.experimental.pallas import tpu as pltpu

F32 = jnp.float32
BF16 = jnp.bfloat16

D_MODEL = 1024
SSM_WIDTH = 512
SSM_GROUP = 16
SSM_GROUPS = 32
SSM_STATE = 64
SSM_FLAT = SSM_GROUPS * SSM_STATE
EIG_CLIP = -1e-4
SB_HEADS = 8
SB_HEAD_DIM = 64
SB_WIDTH = 512
PAGE_SIZE = 128
N_EXPERTS = 64
TOP_K = 8
N_GROUPS = 8
TOPK_GROUPS = 4
GROUP_SIZE = N_EXPERTS // N_GROUPS
EXPERT_HIDDEN = 256
EXPERTS_PER_STEP = 4
ROUTE_BLOCK = 256
ROUTE_CAP = 64
MAX_FIX_BLOCKS = 8
ROUTED_SCALE = 2.5
RMS_EPS = 1e-6

LOG2E = math.log2(math.e)
LOGIT_SCALE_LOG2 = SB_HEAD_DIM ** -0.5 * LOG2E
HEADS_PER_STEP = 4
HEAD_LANES = HEADS_PER_STEP * SB_HEAD_DIM
VMEM_LIMIT = 48 * 1024 * 1024


def _cparams(sem):
    return pltpu.CompilerParams(dimension_semantics=sem, vmem_limit_bytes=VMEM_LIMIT)


def _dot(a, b):
    return jnp.dot(a, b, preferred_element_type=F32)


def _dot_t(a, b):
    return lax.dot_general(a, b, (((1,), (1,)), ((), ())), preferred_element_type=F32)


def _split_bf16(x):
    hi = x.astype(BF16)
    lo = (x - hi.astype(F32)).astype(BF16)
    return hi, lo


def _silu(x):
    return x * jax.nn.sigmoid(x)


def _gelu_tanh(x):
    c = math.sqrt(2.0 / math.pi)
    return 0.5 * x * (1.0 + jnp.tanh(c * (x + 0.044715 * (x * x * x))))


def _ada_kernel(c_ref, w_ref, b_ref, o_ref):
    s = _silu(c_ref[...])
    s_hi, s_lo = _split_bf16(s)
    w_hi, w_lo = _split_bf16(w_ref[...])
    o_ref[...] = _dot(s_hi, w_hi) + _dot(s_hi, w_lo) + _dot(s_lo, w_hi) + b_ref[...]


def _ada_mod(c, w_ada, b_ada):
    rows, d = c.shape
    cols = w_ada.shape[1]
    tn = 1024
    return pl.pallas_call(
        _ada_kernel,
        out_shape=jax.ShapeDtypeStruct((rows, cols), F32),
        grid=(cols // tn,),
        in_specs=[pl.BlockSpec((rows, d), lambda j: (0, 0)),
                  pl.BlockSpec((d, tn), lambda j: (0, j)),
                  pl.BlockSpec((1, tn), lambda j: (0, j))],
        out_specs=pl.BlockSpec((rows, tn), lambda j: (0, j)),
        compiler_params=_cparams(("parallel",)),
        name="ada_mod",
    )(c, w_ada, b_ada.reshape(1, cols))


def _ssm_prep_kernel(lre_ref, lim_ref, ldt_ref, bre_ref, bim_ref,
                     are_ref, aim_ref, bbre_ref, bbim_ref):
    dt = jnp.exp(ldt_ref[...])
    lre = jnp.minimum(lre_ref[...], EIG_CLIP)
    lim = lim_ref[...]
    mag = jnp.exp(lre * dt)
    a_re = mag * jnp.cos(lim * dt)
    a_im = mag * jnp.sin(lim * dt)
    den = lre * lre + lim * lim
    f_re = ((a_re - 1.0) * lre + a_im * lim) / den
    f_im = (a_im * lre - (a_re - 1.0) * lim) / den
    br = bre_ref[...]
    bi = bim_ref[...]
    are_ref[...] = a_re
    aim_ref[...] = a_im
    bbre_ref[...] = f_re * br - f_im * bi
    bbim_ref[...] = f_re * bi + f_im * br


def _ssm_prep(lam_re, lam_im, log_dt, b_re, b_im):
    rep = lambda a: jnp.repeat(a.astype(F32), SSM_GROUP, axis=0)
    lre, lim = rep(lam_re), rep(lam_im)
    ldt = rep(jnp.broadcast_to(log_dt.astype(F32)[:, None], (SSM_GROUPS, SSM_STATE)))
    tr = lambda b: b.astype(F32).transpose(0, 2, 1).reshape(SSM_WIDTH, SSM_STATE)
    shp = jax.ShapeDtypeStruct((SSM_WIDTH, SSM_STATE), F32)
    a_re, a_im, bb_re, bb_im = pl.pallas_call(
        _ssm_prep_kernel, out_shape=(shp, shp, shp, shp), name="ssm_prep",
    )(lre, lim, ldt, tr(b_re), tr(b_im))
    eye = jnp.eye(SSM_GROUPS, dtype=F32)

    def blockdiag_in(bb):
        bb4 = bb.reshape(SSM_GROUPS, SSM_GROUP, 1, SSM_STATE) * eye.reshape(SSM_GROUPS, 1, SSM_GROUPS, 1)
        return bb4.reshape(SSM_WIDTH, SSM_FLAT).astype(BF16)

    a_re = a_re[::SSM_GROUP].reshape(1, SSM_FLAT)
    a_im = a_im[::SSM_GROUP].reshape(1, SSM_FLAT)
    return a_re, a_im, blockdiag_in(bb_re), blockdiag_in(bb_im)


def _blockdiag_out(c):
    eye = jnp.eye(SSM_GROUPS, dtype=F32)
    c4 = c.astype(F32).transpose(0, 2, 1).reshape(SSM_GROUPS, SSM_STATE, 1, SSM_GROUP)
    return (c4 * eye.reshape(SSM_GROUPS, 1, SSM_GROUPS, 1)).reshape(SSM_FLAT, SSM_WIDTH).astype(BF16)


def _inproj_kernel(x_ref, shift_ref, scale_ref, g1_ref, w_ref, qg_ref, kg_ref, hm_ref,
                   u_ref, q_ref, k_ref, kb_ref, v_ref, vb_ref, sga_ref, sgb_ref):
    x = x_ref[...]
    r = lax.rsqrt(jnp.mean(x * x, axis=-1, keepdims=True) + RMS_EPS)
    h = (x * r) * g1_ref[...]
    h = h * (1.0 + scale_ref[...]) + shift_ref[...]
    hb = h.astype(BF16)
    hm = hm_ref[...]

    def head_norm(t, g):
        ms = _dot((t * t).astype(BF16), hm)
        return (t * lax.rsqrt(ms + RMS_EPS)) * g

    u_ref[...] = _dot(hb, w_ref[:, 0:512])
    q = head_norm(_dot(hb, w_ref[:, 512:1024]), qg_ref[...])
    q_ref[...] = (q * LOGIT_SCALE_LOG2).astype(BF16)
    k = head_norm(_dot(hb, w_ref[:, 1024:1536]), kg_ref[...])
    k_ref[...] = k.T
    kb_ref[...] = k.astype(BF16)
    v = _dot(hb, w_ref[:, 1536:2048])
    v_ref[...] = v.T
    vb_ref[...] = v.astype(BF16)
    sga_ref[...] = jax.nn.sigmoid(_dot(hb, w_ref[:, 2048:3072])).astype(BF16)
    sgb_ref[...] = jax.nn.sigmoid(_dot(hb, w_ref[:, 3072:4096])).astype(BF16)


def _head_mean_matrix():
    idx = jnp.arange(SB_WIDTH) // SB_HEAD_DIM
    return ((idx[:, None] == idx[None, :]).astype(F32) / SB_HEAD_DIM).astype(BF16)


def _mod_spec(per_token, tm, rows_per_mod, chunk):
    if per_token:
        return pl.BlockSpec((tm, D_MODEL), lambda i: (i, chunk))
    return pl.BlockSpec((None, 1, D_MODEL), lambda i: (i // rows_per_mod, 0, chunk))


def _in_proj(x, mod, per_token, tm, seq_len, norm1_g, w_in_b, q_g, k_g):
    n = x.shape[0]
    tiles_per_mod = (n // mod.shape[0]) // tm if not per_token else 1
    tiles_per_seq = seq_len // tm
    row = lambda i: (i, 0)
    const = lambda i: (0, 0)
    f32o = lambda w: jax.ShapeDtypeStruct((n, w), F32)
    b16o = lambda w: jax.ShapeDtypeStruct((n, w), BF16)
    kvo = jax.ShapeDtypeStruct((n // seq_len, SB_WIDTH, seq_len), F32)
    kv_blk = pl.BlockSpec((None, SB_WIDTH, tm), lambda i: (i // tiles_per_seq, 0, i % tiles_per_seq))
    blk = lambda w: pl.BlockSpec((tm, w), row)
    return pl.pallas_call(
        _inproj_kernel,
        out_shape=(f32o(512), b16o(512), kvo, b16o(512), kvo, b16o(512), b16o(1024), b16o(1024)),
        grid=(n // tm,),
        in_specs=[blk(D_MODEL),
                  _mod_spec(per_token, tm, tiles_per_mod, 0),
                  _mod_spec(per_token, tm, tiles_per_mod, 1),
                  pl.BlockSpec((1, D_MODEL), const),
                  pl.BlockSpec(w_in_b.shape, const),
                  pl.BlockSpec((1, SB_WIDTH), const),
                  pl.BlockSpec((1, SB_WIDTH), const),
                  pl.BlockSpec((SB_WIDTH, SB_WIDTH), const)],
        out_specs=(blk(512), blk(512), kv_blk, blk(512), kv_blk, blk(512), blk(1024), blk(1024)),
        compiler_params=_cparams(("parallel",)),
        name="in_proj",
    )(x, mod, mod, norm1_g.reshape(1, D_MODEL), w_in_b,
      jnp.tile(q_g, SB_HEADS).reshape(1, SB_WIDTH), jnp.tile(k_g, SB_HEADS).reshape(1, SB_WIDTH),
      _head_mean_matrix())


def _ssm_drive(ub, bb_ref):
    half_c, half_s = SSM_WIDTH // 2, SSM_FLAT // 2
    lo = _dot(ub[:, :half_c], bb_ref[:half_c, :half_s])
    hi = _dot(ub[:, half_c:], bb_ref[half_c:, half_s:])
    return lo, hi


def _ssm_readout(xre, xim, ccre_ref, ccim_ref, d_ref, u):
    half_c, half_s = SSM_WIDTH // 2, SSM_FLAT // 2
    xr = xre.astype(BF16)
    xi = xim.astype(BF16)
    y_lo = _dot(xr[:, :half_s], ccre_ref[:half_s, :half_c]) + _dot(xi[:, :half_s], ccim_ref[:half_s, :half_c])
    y_hi = _dot(xr[:, half_s:], ccre_ref[half_s:, half_c:]) + _dot(xi[:, half_s:], ccim_ref[half_s:, half_c:])
    y = jnp.concatenate([y_lo, y_hi], axis=1) + d_ref[...] * u
    return _gelu_tanh(y)


def _ssm_prompt_kernel(u_ref, bbre_ref, bbim_ref, are_ref, aim_ref, ccre_ref, ccim_ref, d_ref,
                       g_ref, sre_ref, sim_ref, ubuf, gbuf, xre, xim, cre, cim, *, lane_chunk):
    batch, tt, _ = u_ref.shape
    rows = tt * batch
    half_s = SSM_FLAT // 2

    @pl.when(pl.program_id(0) == 0)
    def _():
        cre[...] = jnp.zeros_like(cre)
        cim[...] = jnp.zeros_like(cim)

    n_lane_tiles = SSM_WIDTH // 128
    for b in range(batch):
        for c in range(n_lane_tiles):
            ubuf[c, pl.ds(b, tt, stride=batch), :] = u_ref[b, :, c * 128:(c + 1) * 128]
    u = jnp.concatenate([ubuf[c] for c in range(n_lane_tiles)], axis=1)
    ub = u.astype(BF16)
    lo, hi = _ssm_drive(ub, bbre_ref)
    xre[:, :half_s] = lo
    xre[:, half_s:] = hi
    lo, hi = _ssm_drive(ub, bbim_ref)
    xim[:, :half_s] = lo
    xim[:, half_s:] = hi

    first_step = lax.broadcasted_iota(jnp.int32, (8, lane_chunk), 0) < batch
    for c0 in range(0, SSM_FLAT, lane_chunk):
        cols = pl.ds(c0, lane_chunk)
        ar = jnp.broadcast_to(are_ref[:, cols], (8, lane_chunk))
        ai = jnp.broadcast_to(aim_ref[:, cols], (8, lane_chunk))

        def step(i, carry, cols=cols, ar=ar, ai=ai):
            c_r, c_i = carry
            r0 = pl.multiple_of(i * 8, 8)
            br = xre[pl.ds(r0, 8), cols]
            bi = xim[pl.ds(r0, 8), cols]
            p_r = pltpu.roll(c_r, batch, 0)
            p_i = pltpu.roll(c_i, batch, 0)
            y_r = ar * p_r - ai * p_i + br
            y_i = ar * p_i + ai * p_r + bi
            q_r = pltpu.roll(y_r, batch, 0)
            q_i = pltpu.roll(y_i, batch, 0)
            z_r = ar * q_r - ai * q_i + br
            z_i = ar * q_i + ai * q_r + bi
            xre[pl.ds(r0, 8), cols] = jnp.where(first_step, y_r, z_r)
            xim[pl.ds(r0, 8), cols] = jnp.where(first_step, y_i, z_i)
            return z_r, z_i

        c_r, c_i = lax.fori_loop(0, rows // 8, step, (cre[:, cols], cim[:, cols]))
        cre[:, cols] = c_r
        cim[:, cols] = c_i

    g = _ssm_readout(xre[...], xim[...], ccre_ref, ccim_ref, d_ref, u)
    for c in range(n_lane_tiles):
        gbuf[c] = g[:, c * 128:(c + 1) * 128]
    for b in range(batch):
        g_ref[b] = jnp.concatenate([gbuf[c, pl.ds(b, tt, stride=batch), :] for c in range(n_lane_tiles)],
                                   axis=1).astype(BF16)
    sre_ref[...] = cre[...]
    sim_ref[...] = cim[...]


def _ssm_prompt(u, ssm, tt=128, lane_chunk=512):
    a_re, a_im, bb_re, bb_im, cc_re, cc_im, d_skip = ssm
    batch, t_len, _ = u.shape
    assert 2 * batch == 8, "one sublane tile must hold exactly two time steps"
    rows = tt * batch
    const = lambda j: (0, 0)
    full = lambda a: pl.BlockSpec(a.shape, const)
    seq_blk = pl.BlockSpec((batch, tt, SSM_WIDTH), lambda j: (0, j, 0))
    g, s_re, s_im = pl.pallas_call(
        functools.partial(_ssm_prompt_kernel, lane_chunk=lane_chunk),
        out_shape=(jax.ShapeDtypeStruct((batch, t_len, SSM_WIDTH), BF16),
                   jax.ShapeDtypeStruct((8, SSM_FLAT), F32),
                   jax.ShapeDtypeStruct((8, SSM_FLAT), F32)),
        grid=(t_len // tt,),
        in_specs=[seq_blk,
                  full(bb_re), full(bb_im), full(a_re), full(a_im), full(cc_re), full(cc_im), full(d_skip)],
        out_specs=(seq_blk, pl.BlockSpec((8, SSM_FLAT), const), pl.BlockSpec((8, SSM_FLAT), const)),
        scratch_shapes=[pltpu.VMEM((SSM_WIDTH // 128, rows, 128), F32),
                        pltpu.VMEM((SSM_WIDTH // 128, rows, 128), F32),
                        pltpu.VMEM((rows, SSM_FLAT), F32), pltpu.VMEM((rows, SSM_FLAT), F32),
                        pltpu.VMEM((8, SSM_FLAT), F32), pltpu.VMEM((8, SSM_FLAT), F32)],
        compiler_params=_cparams(("arbitrary",)),
        name="ssm_prompt",
    )(u, bb_re, bb_im, a_re, a_im, cc_re, cc_im, d_skip)
    return g, s_re[batch:], s_im[batch:]


def _ssm_sample_kernel(u_ref, s0re_ref, s0im_ref, bbre_ref, bbim_ref, are_ref, aim_ref,
                       ccre_ref, ccim_ref, d_ref, g_ref, sre_ref, sim_ref):
    u = u_ref[...]
    ub = u.astype(BF16)
    ar, ai = are_ref[...], aim_ref[...]
    s_r, s_i = s0re_ref[...], s0im_ref[...]
    lo, hi = _ssm_drive(ub, bbre_ref)
    x_r = ar * s_r - ai * s_i + jnp.concatenate([lo, hi], axis=1)
    lo, hi = _ssm_drive(ub, bbim_ref)
    x_i = ar * s_i + ai * s_r + jnp.concatenate([lo, hi], axis=1)
    sre_ref[...] = x_r
    sim_ref[...] = x_i
    g_ref[...] = _ssm_readout(x_r, x_i, ccre_ref, ccim_ref, d_ref, u).astype(BF16)


def _ssm_sample(u, s_re, s_im, ssm):
    a_re, a_im, bb_re, bb_im, cc_re, cc_im, d_skip = ssm
    n = u.shape[0]
    return pl.pallas_call(
        _ssm_sample_kernel,
        out_shape=(jax.ShapeDtypeStruct((n, SSM_WIDTH), BF16),
                   jax.ShapeDtypeStruct((n, SSM_FLAT), F32),
                   jax.ShapeDtypeStruct((n, SSM_FLAT), F32)),
        compiler_params=pltpu.CompilerParams(vmem_limit_bytes=VMEM_LIMIT),
        name="ssm_sample",
    )(u, s_re, s_im, bb_re, bb_im, a_re, a_im, cc_re, cc_im, d_skip)


def _log_sigmoid_pair(z2):
    sign = jnp.uint32(0x80000000)
    neg_abs = lax.bitcast_convert_type(lax.bitcast_convert_type(z2, jnp.uint32) | sign, F32)
    soft = jnp.log2(1.0 + jnp.exp2(neg_abs))
    lsig = jnp.minimum(z2, 0.0) - soft
    return lsig, lsig - z2


def _attn_prompt_kernel(bias_ref, q_ref, k_ref, v_ref, tri_ref, o_ref, acc_ref, run_ref, *, tq, tk):
    qi = pl.program_id(1)
    nh = HEADS_PER_STEP
    ngrp = SB_HEADS // nh
    lane_head = lax.broadcasted_iota(jnp.int32, (1, HEAD_LANES), 1) // SB_HEAD_DIM
    head_mask = [(lane_head == h).astype(F32).astype(BF16) for h in range(nh)]
    qs, bias = [], []
    for g in range(ngrp):
        q = q_ref[:, g * HEAD_LANES:(g + 1) * HEAD_LANES]
        qs.append(jnp.concatenate([q * head_mask[h] for h in range(nh)], axis=0))
        bias.append([bias_ref[g * nh + h] for h in range(nh)])
    tri = tri_ref[...]
    acc_ref[...] = jnp.zeros_like(acc_ref)
    run_ref[...] = jnp.zeros_like(run_ref)

    def block(js, masked):
        if masked:
            k0 = pl.multiple_of(js[0] * tk, tk)
            kpos = k0 + lax.broadcasted_iota(jnp.int32, (nh * tq, tk), 1)
            qpos = qi * tq + (lax.broadcasted_iota(jnp.int32, (nh * tq, tk), 0) & (tq - 1))
            valid = kpos < qpos
        for g in range(ngrp):
            lanes = slice(g * HEAD_LANES, (g + 1) * HEAD_LANES)
            run = run_ref[g]
            pv = None
            for j in js:
                k0 = pl.multiple_of(j * tk, tk)
                zs = _dot_t(qs[g], k_ref[pl.ds(k0, tk), lanes])
                lsig_parts, stay_parts = [], []
                for h in range(nh):
                    lsig, stay = _log_sigmoid_pair(zs[h * tq:(h + 1) * tq] + bias[g][h])
                    lsig_parts.append(lsig)
                    stay_parts.append(stay)
                lsig = jnp.concatenate(lsig_parts, axis=0)
                stay = jnp.concatenate(stay_parts, axis=0)
                if masked:
                    stay = jnp.where(valid, stay, 0.0)
                after = _dot(stay.astype(BF16), tri)
                w = jnp.exp2(lsig + after + run)
                if masked:
                    w = jnp.where(valid, w, 0.0)
                part = _dot(w.astype(BF16), v_ref[pl.ds(k0, tk), lanes])
                pv = part if pv is None else pv + part
                run = run + (after[:, 0:1] + stay[:, 0:1])
            acc_ref[g] += pv
            run_ref[g] = run

    j_diag = (qi * tq) // tk
    block((j_diag,), True)

    @pl.when(j_diag % 2 == 1)
    def _():
        block((j_diag - 1,), False)

    def body(jj, carry):
        j = 2 * (j_diag // 2 - jj) - 1
        block((j, j - 1), False)
        return carry

    lax.fori_loop(0, j_diag // 2, body, 0)
    for g in range(ngrp):
        acc = acc_ref[g]
        out = jnp.zeros((tq, HEAD_LANES), F32)
        for h in range(nh):
            out = jnp.where(lane_head == h, acc[h * tq:(h + 1) * tq], out)
        o_ref[:, g * HEAD_LANES:(g + 1) * HEAD_LANES] = out.astype(BF16)


def _strict_upper(tk):
    idx = jnp.arange(tk)
    return (idx[:, None] > idx[None, :]).astype(BF16)


def _attn_prompt(q, kb, vb, sb_bias, batch, t_len, tq=256, tk=256):
    assert tq & (tq - 1) == 0 and tq <= tk, "one masked key block must cover the diagonal"
    n = q.shape[0]
    nq = t_len // tq
    ngrp = SB_HEADS // HEADS_PER_STEP
    return pl.pallas_call(
        functools.partial(_attn_prompt_kernel, tq=tq, tk=tk),
        out_shape=jax.ShapeDtypeStruct((n, SB_WIDTH), BF16),
        grid=(batch, nq),
        in_specs=[pl.BlockSpec(memory_space=pltpu.SMEM),
                  pl.BlockSpec((tq, SB_WIDTH), lambda b, i: (b * nq + i, 0)),
                  pl.BlockSpec((t_len, SB_WIDTH), lambda b, i: (b, 0)),
                  pl.BlockSpec((t_len, SB_WIDTH), lambda b, i: (b, 0)),
                  pl.BlockSpec((tk, tk), lambda b, i: (0, 0))],
        out_specs=pl.BlockSpec((tq, SB_WIDTH), lambda b, i: (b * nq + i, 0)),
        scratch_shapes=[pltpu.VMEM((ngrp, HEADS_PER_STEP * tq, HEAD_LANES), F32),
                        pltpu.VMEM((ngrp, HEADS_PER_STEP * tq, 1), F32)],
        compiler_params=_cparams(("parallel", "arbitrary")),
        name="attn_prompt",
    )(sb_bias.astype(F32) * LOG2E, q, kb, vb, _strict_upper(tk))


def _attn_sample_kernel(pt_ref, q_ref, bias_ref, tri_ref, later_ref, *refs, n_pages):
    del pt_ref
    k_refs = refs[:n_pages]
    v_refs = refs[n_pages:2 * n_pages]
    o_ref = refs[2 * n_pages]
    hd = (SB_HEADS, SB_HEAD_DIM, PAGE_SIZE)
    q = q_ref[0].astype(F32)
    q_col = jnp.broadcast_to(q, (PAGE_SIZE, SB_WIDTH)).T
    q3 = q_col.reshape(hd)
    z = jnp.concatenate([jnp.sum(k_refs[p][0].reshape(hd) * q3, axis=1) for p in range(n_pages)], axis=0)
    lsig, stay = _log_sigmoid_pair(z + bias_ref[...])
    after = _dot(stay.astype(BF16), tri_ref[...])
    total = jnp.broadcast_to(after[:, 0:1] + stay[:, 0:1], stay.shape)
    t_hi, t_lo = _split_bf16(total)
    later = later_ref[...]
    run = _dot(later, t_hi) + _dot(later, t_lo)
    w = jnp.exp2(lsig + after + run)
    acc = [jnp.zeros((SB_HEAD_DIM, PAGE_SIZE), F32) for _ in range(SB_HEADS)]
    for p in range(n_pages):
        for h in range(SB_HEADS):
            r = p * SB_HEADS + h
            acc[h] = acc[h] + v_refs[p][0, h * SB_HEAD_DIM:(h + 1) * SB_HEAD_DIM, :] * w[r:r + 1, :]
    a_hi, a_lo = _split_bf16(jnp.concatenate(acc, axis=0))
    ones = jnp.ones((8, PAGE_SIZE), BF16)
    o_ref[0] = (_dot_t(ones, a_hi) + _dot_t(ones, a_lo))[0:1].astype(BF16)


def _attn_sample(q, cache_k, cache_v, page_table, sb_bias):
    n, n_pages = page_table.shape
    n_pool = cache_k.shape[0]
    ck = cache_k.transpose(0, 2, 3, 1).reshape(n_pool, SB_WIDTH, PAGE_SIZE)
    cv = cache_v.transpose(0, 2, 3, 1).reshape(n_pool, SB_WIDTH, PAGE_SIZE)
    page_spec = lambda p: pl.BlockSpec((1, SB_WIDTH, PAGE_SIZE), lambda i, pt, p=p: (pt[i * n_pages + p], 0, 0))
    tok_spec = pl.BlockSpec((1, 1, SB_WIDTH), lambda i, pt: (i, 0, 0))
    rows = n_pages * SB_HEADS
    const = lambda shape: pl.BlockSpec(shape, lambda i, pt: (0, 0))
    r = jnp.arange(rows)
    later = ((r[:, None] % SB_HEADS == r[None, :] % SB_HEADS)
             & (r[None, :] // SB_HEADS > r[:, None] // SB_HEADS)).astype(BF16)
    out = pl.pallas_call(
        functools.partial(_attn_sample_kernel, n_pages=n_pages),
        out_shape=jax.ShapeDtypeStruct((n, 1, SB_WIDTH), BF16),
        grid_spec=pltpu.PrefetchScalarGridSpec(
            num_scalar_prefetch=1,
            grid=(n,),
            in_specs=[tok_spec, const((rows, 1)), const((PAGE_SIZE, PAGE_SIZE)), const((rows, rows))]
                     + [page_spec(p) for p in range(n_pages)] * 2,
            out_specs=tok_spec),
        compiler_params=_cparams(("parallel",)),
        name="attn_sample",
    )(page_table.reshape(-1).astype(jnp.int32), q.reshape(n, 1, SB_WIDTH),
      jnp.tile(sb_bias.astype(F32) * LOG2E, n_pages).reshape(rows, 1), _strict_upper(PAGE_SIZE), later,
      *([ck] * n_pages), *([cv] * n_pages))
    return out.reshape(n, SB_WIDTH)


def _first_index_of_max(vals, row_id, n_rows):
    m = jnp.max(vals, axis=0, keepdims=True)
    first = jnp.min(jnp.where(vals == m, row_id, n_rows), axis=0, keepdims=True)
    return m, first


def _route(logits_t, rbias):
    tm = logits_t.shape[1]
    scores = jax.nn.sigmoid(logits_t)
    biased = scores + rbias
    row8 = lax.broadcasted_iota(jnp.int32, (GROUP_SIZE, tm), 0)
    neg = jnp.float32(-jnp.inf)
    grp_score = []
    for g in range(N_GROUPS):
        blk = biased[g * GROUP_SIZE:(g + 1) * GROUP_SIZE]
        m1, first = _first_index_of_max(blk, row8, GROUP_SIZE)
        m2 = jnp.max(jnp.where(row8 == first, neg, blk), axis=0, keepdims=True)
        grp_score.append(m1 + m2)
    masked = []
    for g in range(N_GROUPS):
        beaten_by = jnp.zeros((1, tm), jnp.int32)
        for o in range(N_GROUPS):
            if o == g:
                continue
            wins = (grp_score[o] > grp_score[g]) if o > g else (grp_score[o] >= grp_score[g])
            beaten_by = beaten_by + wins.astype(jnp.int32)
        keep = beaten_by < TOPK_GROUPS
        masked.append(jnp.where(keep, biased[g * GROUP_SIZE:(g + 1) * GROUP_SIZE], neg))
    cur = jnp.concatenate(masked, axis=0)
    row = lax.broadcasted_iota(jnp.int32, (N_EXPERTS, tm), 0)
    chosen = jnp.zeros((N_EXPERTS, tm), jnp.bool_)
    for _ in range(TOP_K):
        _, first = _first_index_of_max(cur, row, N_EXPERTS)
        pick = row == first
        chosen = jnp.logical_or(chosen, pick)
        cur = jnp.where(pick, neg, cur)
    sel = jnp.where(chosen, scores, 0.0)
    return sel / jnp.sum(sel, axis=0, keepdims=True) * ROUTED_SCALE


def _post_kernel(x_ref, g_ref, o_ref, sga_ref, sgb_ref, gate1_ref, shift2_ref, scale2_ref, g2_ref,
                 wga_ref, wgb_ref, wsb_ref, wout_ref, wr_ref, rb_ref,
                 x1_ref, h2_ref, gates_ref):
    g = g_ref[...]
    branch_a = _dot(g, wga_ref[...]) * jax.nn.sigmoid(_dot(g, wgb_ref[...]))
    branch_b = _dot(o_ref[...], wsb_ref[...])
    merged = sga_ref[...].astype(F32) * branch_a + sgb_ref[...].astype(F32) * branch_b
    x1 = x_ref[...] + gate1_ref[...] * _dot(merged.astype(BF16), wout_ref[...])
    x1_ref[...] = x1
    r = lax.rsqrt(jnp.mean(x1 * x1, axis=-1, keepdims=True) + RMS_EPS)
    h2 = (x1 * r) * g2_ref[...]
    h2 = h2 * (1.0 + scale2_ref[...]) + shift2_ref[...]
    h2_ref[...] = h2.astype(BF16)
    h_hi, h_lo = _split_bf16(h2)
    w_hi, w_lo = _split_bf16(wr_ref[...])
    logits_t = _dot_t(w_hi, h_hi) + _dot_t(w_hi, h_lo) + _dot_t(w_lo, h_hi)
    gates_ref[...] = _route(logits_t, rb_ref[...])


def _post_mixer(x, g, o, sga, sgb, mod, per_token, tm, p):
    n = x.shape[0]
    tiles_per_mod = (n // mod.shape[0]) // tm if not per_token else 1
    row = lambda i: (i, 0)
    const = lambda i: (0, 0)
    blk = lambda w: pl.BlockSpec((tm, w), row)
    full = lambda a: pl.BlockSpec(a.shape, const)
    ms = lambda c: _mod_spec(per_token, tm, tiles_per_mod, c)
    weights = (p["w_glu_a"], p["w_glu_b"], p["w_sb_out"], p["w_out"], p["w_router_t"], p["router_bias"])
    return pl.pallas_call(
        _post_kernel,
        out_shape=(jax.ShapeDtypeStruct((n, D_MODEL), F32),
                   jax.ShapeDtypeStruct((n, D_MODEL), BF16),
                   jax.ShapeDtypeStruct((N_EXPERTS, n), F32)),
        grid=(n // tm,),
        in_specs=[blk(D_MODEL), blk(512), blk(512), blk(1024), blk(1024), ms(2), ms(3), ms(4),
                  pl.BlockSpec((1, D_MODEL), const)] + [full(w) for w in weights],
        out_specs=(blk(D_MODEL), blk(D_MODEL), pl.BlockSpec((N_EXPERTS, tm), lambda i: (0, i))),
        compiler_params=_cparams(("parallel",)),
        name="post_mixer",
    )(x, g, o, sga, sgb, mod, mod, mod, p["norm2_g"], *weights)


def _moe_kernel(h_ref, gates_ref, x1_ref, gate2_ref, wgs_ref, wus_ref, wds_ref,
                wg_ref, wu_ref, wd_ref, y_ref, acc_ref):
    j = pl.program_id(1)
    h = h_ref[...]

    @pl.when(j == 0)
    def _():
        act = _silu(_dot(h, wgs_ref[...])) * _dot(h, wus_ref[...])
        acc_ref[...] = _dot(act.astype(BF16), wds_ref[...])

    gates = gates_ref[...]
    lane = lax.broadcasted_iota(jnp.int32, gates.shape, 1)
    scaled = []
    for s in range(EXPERTS_PER_STEP):
        act = _silu(_dot(h, wg_ref[s].astype(BF16))) * _dot(h, wu_ref[s].astype(BF16))
        gcol = jnp.sum(jnp.where(lane == j * EXPERTS_PER_STEP + s, gates, 0.0), axis=1, keepdims=True)
        scaled.append((act * gcol).astype(BF16))
    acc_ref[...] += _dot(jnp.concatenate(scaled, axis=1), wd_ref[...].astype(BF16))

    @pl.when(j == pl.num_programs(1) - 1)
    def _():
        y_ref[...] = x1_ref[...] + gate2_ref[...] * acc_ref[...]


def _moe(h2, gates, x1, mod, per_token, tm, p):
    n = h2.shape[0]
    tiles_per_mod = (n // mod.shape[0]) // tm if not per_token else 1
    row = lambda i, e: (i, 0)
    const = lambda i, e: (0, 0)
    exp = lambda i, e: (e, 0, 0)
    if per_token:
        gate2_spec = pl.BlockSpec((tm, D_MODEL), lambda i, e: (i, 5))
    else:
        gate2_spec = pl.BlockSpec((None, 1, D_MODEL), lambda i, e: (i // tiles_per_mod, 0, 5))
    wide = EXPERTS_PER_STEP * EXPERT_HIDDEN
    return pl.pallas_call(
        _moe_kernel,
        out_shape=jax.ShapeDtypeStruct((n, D_MODEL), F32),
        grid=(n // tm, N_EXPERTS // EXPERTS_PER_STEP),
        in_specs=[pl.BlockSpec((tm, D_MODEL), row),
                  pl.BlockSpec((tm, N_EXPERTS), row),
                  pl.BlockSpec((tm, D_MODEL), row),
                  gate2_spec,
                  pl.BlockSpec((D_MODEL, EXPERT_HIDDEN), const),
                  pl.BlockSpec((D_MODEL, EXPERT_HIDDEN), const),
                  pl.BlockSpec((EXPERT_HIDDEN, D_MODEL), const),
                  pl.BlockSpec((EXPERTS_PER_STEP, D_MODEL, EXPERT_HIDDEN), exp),
                  pl.BlockSpec((EXPERTS_PER_STEP, D_MODEL, EXPERT_HIDDEN), exp),
                  pl.BlockSpec((None, wide, D_MODEL), exp)],
        out_specs=pl.BlockSpec((tm, D_MODEL), row),
        scratch_shapes=[pltpu.VMEM((tm, D_MODEL), F32)],
        compiler_params=_cparams(("parallel", "arbitrary")),
        name="moe",
    )(h2, gates, x1, mod, p["w_gate_s"], p["w_up_s"], p["w_down_s"], p["w_gate_e"], p["w_up_e"], p["w_down_e"])


def _dispatch_kernel(gates_ref, h_ref, tri_ref, xs_ref, flag_ref):
    g = gates_ref[...]
    sel = g > 0.0
    rank = _dot(sel.astype(F32).astype(BF16), tri_ref[...])
    rank = jnp.where(sel, rank, -1.0)
    over = jnp.where(rank >= ROUTE_CAP, 1.0, 0.0)
    worst = jnp.max(jnp.max(over, axis=1, keepdims=True), axis=0, keepdims=True)
    flag_ref[...] = jnp.broadcast_to(worst, flag_ref.shape)
    slot = lax.broadcasted_iota(jnp.int32, (ROUTE_CAP, g.shape[1]), 0).astype(F32)
    h = h_ref[...]
    per_dot = 8
    for e0 in range(0, N_EXPERTS, per_dot):
        place = jnp.concatenate([(slot == rank[e:e + 1, :]).astype(F32).astype(BF16)
                                 for e in range(e0, e0 + per_dot)], axis=0)
        xs_ref[e0 * ROUTE_CAP:(e0 + per_dot) * ROUTE_CAP, :] = _dot(place, h).astype(BF16)


def _expert_kernel(x_ref, wg_ref, wu_ref, wd_ref, y_ref):
    nb, cap, d = x_ref.shape
    x = x_ref[...].reshape(nb * cap, d)
    n_part = 2 if nb % 2 == 0 else 1
    rows = nb * cap // n_part
    wg, wu, wd = wg_ref[...].astype(BF16), wu_ref[...].astype(BF16), wd_ref[...].astype(BF16)
    for r in range(n_part):
        xr = x[r * rows:(r + 1) * rows]
        act = _silu(_dot(xr, wg)) * _dot(xr, wu)
        y = _dot(act.astype(BF16), wd).astype(BF16)
        y_ref[r * (nb // n_part):(r + 1) * (nb // n_part)] = y.reshape(nb // n_part, cap, d)


def _combine_kernel(gates_ref, ys_ref, h_ref, x1_ref, gate2_ref, wgs_ref, wus_ref, wds_ref,
                    tri_ref, y_ref):
    g = gates_ref[...]
    sel = g > 0.0
    rank = _dot(sel.astype(F32).astype(BF16), tri_ref[...])
    rank = jnp.where(sel, rank, -1.0)
    slot = lax.broadcasted_iota(jnp.int32, (ROUTE_CAP, g.shape[1]), 0).astype(F32)
    weigh = jnp.concatenate([jnp.where(slot == rank[e:e + 1, :], g[e:e + 1, :], 0.0).astype(BF16)
                             for e in range(N_EXPERTS)], axis=0)
    routed = lax.dot_general(weigh, ys_ref[...], (((0,), (0,)), ((), ())), preferred_element_type=F32)
    h = h_ref[...]
    act = _silu(_dot(h, wgs_ref[...])) * _dot(h, wus_ref[...])
    shared = _dot(act.astype(BF16), wds_ref[...])
    y_ref[...] = x1_ref[...] + gate2_ref[...] * (shared + routed)


def _moe_routed(h2, gates_t, x1, mod, p):
    assert ROUTE_CAP & (ROUTE_CAP - 1) == 0
    n = h2.shape[0]
    bt = ROUTE_BLOCK
    nb = n // bt
    slots = N_EXPERTS * ROUTE_CAP
    tiles_per_mod = (n // mod.shape[0]) // bt
    later = _strict_upper(bt)
    const2 = lambda b: (0, 0)
    xs, flags = pl.pallas_call(
        _dispatch_kernel,
        out_shape=(jax.ShapeDtypeStruct((nb, slots, D_MODEL), BF16),
                   jax.ShapeDtypeStruct((nb, 8, 128), F32)),
        grid=(nb,),
        in_specs=[pl.BlockSpec((N_EXPERTS, bt), lambda b: (0, b)),
                  pl.BlockSpec((bt, D_MODEL), lambda b: (b, 0)),
                  pl.BlockSpec((bt, bt), const2)],
        out_specs=(pl.BlockSpec((None, slots, D_MODEL), lambda b: (b, 0, 0)),
                   pl.BlockSpec((None, 8, 128), lambda b: (b, 0, 0))),
        compiler_params=_cparams(("parallel",)),
        name="moe_dispatch",
    )(gates_t, h2, later.T)
    gates = gates_t.T

    n_half = 2
    xblk = pl.BlockSpec((nb // n_half, None, ROUTE_CAP, D_MODEL), lambda e, f: (f, e, 0, 0))
    ys = pl.pallas_call(
        _expert_kernel,
        out_shape=jax.ShapeDtypeStruct((nb, N_EXPERTS, ROUTE_CAP, D_MODEL), BF16),
        grid=(N_EXPERTS, n_half),
        in_specs=[xblk,
                  pl.BlockSpec((None, D_MODEL, EXPERT_HIDDEN), lambda e, f: (e, 0, 0)),
                  pl.BlockSpec((None, D_MODEL, EXPERT_HIDDEN), lambda e, f: (e, 0, 0)),
                  pl.BlockSpec((None, EXPERT_HIDDEN, D_MODEL), lambda e, f: (e, 0, 0))],
        out_specs=xblk,
        compiler_params=_cparams(("parallel", "arbitrary")),
        name="moe_experts",
    )(xs.reshape(nb, N_EXPERTS, ROUTE_CAP, D_MODEL), p["w_gate_e"], p["w_up_e"],
      p["w_down_e"].reshape(N_EXPERTS, EXPERT_HIDDEN, D_MODEL))
    blk = lambda w: pl.BlockSpec((bt, w), lambda b: (b, 0))
    full = lambda a: pl.BlockSpec(a.shape, const2)
    y = pl.pallas_call(
        _combine_kernel,
        out_shape=jax.ShapeDtypeStruct((n, D_MODEL), F32),
        grid=(nb,),
        in_specs=[pl.BlockSpec((N_EXPERTS, bt), lambda b: (0, b)),
                  pl.BlockSpec((None, slots, D_MODEL), lambda b: (b, 0, 0)),
                  blk(D_MODEL), blk(D_MODEL), _mod_spec(False, bt, tiles_per_mod, 5),
                  full(p["w_gate_s"]), full(p["w_up_s"]), full(p["w_down_s"]), pl.BlockSpec((bt, bt), const2)],
        out_specs=blk(D_MODEL),
        compiler_params=_cparams(("parallel",)),
        name="moe_combine",
    )(gates_t, ys.reshape(nb, slots, D_MODEL), h2, x1, mod, p["w_gate_s"], p["w_up_s"], p["w_down_s"],
      later.T)

    over = flags[:, 0, 0] > 0.0
    n_fix = jnp.sum(over.astype(jnp.int32))
    fix_first = jnp.argsort(jnp.logical_not(over), stable=True).astype(jnp.int32)
    case = (n_fix > 0).astype(jnp.int32) + (n_fix > MAX_FIX_BLOCKS).astype(jnp.int32)
    return lax.switch(case, [lambda: y,
                             lambda: _moe_fix(y, fix_first, n_fix.reshape(1), h2, gates, x1, mod, p),
                             lambda: _moe(h2, gates, x1, mod, False, 512, p)])


def _moe_fix_kernel(blocks_ref, n_fix_ref, h_ref, gates_ref, x1_ref, gate2_ref, wgs_ref, wus_ref, wds_ref,
                    wg_ref, wu_ref, wd_ref, y_in_ref, y_ref, acc_ref):
    del blocks_ref, y_in_ref

    @pl.when(pl.program_id(0) < n_fix_ref[0])
    def _():
        _moe_kernel(h_ref, gates_ref, x1_ref, gate2_ref, wgs_ref, wus_ref, wds_ref,
                    wg_ref, wu_ref, wd_ref, y_ref, acc_ref)


def _moe_fix(y, blocks, n_fix, h2, gates, x1, mod, p):
    n = h2.shape[0]
    bt = ROUTE_BLOCK
    nb = n // bt
    n_steps = N_EXPERTS // EXPERTS_PER_STEP
    tiles_per_mod = (n // mod.shape[0]) // bt
    wide = EXPERTS_PER_STEP * EXPERT_HIDDEN

    def tile(i, bl, nf):
        return bl[jnp.minimum(i, jnp.maximum(nf[0] - 1, 0))]

    row = lambda i, j, bl, nf: (tile(i, bl, nf), 0)
    const = lambda i, j, bl, nf: (0, 0)
    exp = lambda i, j, bl, nf: (jnp.where(i < nf[0], j, n_steps - 1), 0, 0)
    return pl.pallas_call(
        _moe_fix_kernel,
        out_shape=jax.ShapeDtypeStruct((n, D_MODEL), F32),
        grid_spec=pltpu.PrefetchScalarGridSpec(
            num_scalar_prefetch=2,
            grid=(min(nb, MAX_FIX_BLOCKS), n_steps),
            in_specs=[pl.BlockSpec((bt, D_MODEL), row),
                      pl.BlockSpec((bt, N_EXPERTS), row),
                      pl.BlockSpec((bt, D_MODEL), row),
                      pl.BlockSpec((None, 1, D_MODEL), lambda i, j, bl, nf: (tile(i, bl, nf) // tiles_per_mod, 0, 5)),
                      pl.BlockSpec((D_MODEL, EXPERT_HIDDEN), const),
                      pl.BlockSpec((D_MODEL, EXPERT_HIDDEN), const),
                      pl.BlockSpec((EXPERT_HIDDEN, D_MODEL), const),
                      pl.BlockSpec((EXPERTS_PER_STEP, D_MODEL, EXPERT_HIDDEN), exp),
                      pl.BlockSpec((EXPERTS_PER_STEP, D_MODEL, EXPERT_HIDDEN), exp),
                      pl.BlockSpec((None, wide, D_MODEL), exp),
                      pl.BlockSpec(memory_space=pl.ANY)],
            out_specs=pl.BlockSpec((bt, D_MODEL), row),
            scratch_shapes=[pltpu.VMEM((bt, D_MODEL), F32)]),
        input_output_aliases={12: 0},
        compiler_params=_cparams(("arbitrary", "arbitrary")),
        name="moe_fix",
    )(blocks, n_fix, h2, gates, x1, mod, p["w_gate_s"], p["w_up_s"], p["w_down_s"],
      p["w_gate_e"], p["w_up_e"], p["w_down_e"], y)


def _layer(x_prompt, x_sample, cache_k, cache_v, s0_re, s0_im, page_table, c_prompt, c_sample, p):
    batch, t_len, d = x_prompt.shape
    n_s = x_sample.shape[0]
    n_p = batch * t_len

    c_all = jnp.concatenate([c_prompt, c_sample], axis=0)
    pad = (-c_all.shape[0]) % 8
    c_all = jnp.pad(c_all, ((0, pad), (0, 0)))
    mod = _ada_mod(c_all, p["w_ada"], p["b_ada"])
    mod_p = mod[:batch].reshape(batch, 1, 6 * d)
    mod_s = mod[batch:batch + n_s]

    a_re, a_im, bb_re, bb_im = _ssm_prep(p["lam_re"], p["lam_im"], p["log_dt"], p["b_re"], p["b_im"])
    ssm = (a_re, a_im, bb_re, bb_im, _blockdiag_out(p["c_re"]), _blockdiag_out(-p["c_im"]),
           p["d_skip"].astype(F32).reshape(1, SSM_WIDTH))

    xp = x_prompt.reshape(n_p, d)
    xs = x_sample.reshape(n_s, d)
    proj_p = _in_proj(xp, mod_p, False, 512, t_len, p["norm1_g"], p["w_in"], p["q_norm_g"], p["k_norm_g"])
    proj_s = _in_proj(xs, mod_s, True, n_s, n_s, p["norm1_g"], p["w_in"], p["q_norm_g"], p["k_norm_g"])
    u_p, q_p, k_p, kb_p, v_p, vb_p, sga_p, sgb_p = proj_p
    u_s, q_s, k_s, _, v_s, _, sga_s, sgb_s = proj_s

    g_p, st_re_p, st_im_p = _ssm_prompt(u_p.reshape(batch, t_len, SSM_WIDTH), ssm)
    g_p = g_p.reshape(n_p, SSM_WIDTH)
    g_s, st_re_s, st_im_s = _ssm_sample(u_s, s0_re.reshape(n_s, SSM_FLAT), s0_im.reshape(n_s, SSM_FLAT), ssm)

    o_p = _attn_prompt(q_p, kb_p, vb_p, p["sb_bias"], batch, t_len)
    o_s = _attn_sample(q_s, cache_k, cache_v, page_table, p["sb_bias"])

    x1_p, h2_p, gates_p = _post_mixer(xp, g_p, o_p, sga_p, sgb_p, mod_p, False, 512, p)
    x1_s, h2_s, gates_s = _post_mixer(xs, g_s, o_s, sga_s, sgb_s, mod_s, True, n_s, p)

    y_p = _moe_routed(h2_p, gates_p, x1_p, mod_p, p)
    y_s = _moe(h2_s, gates_s.T, x1_s, mod_s, True, n_s, p)

    hd = (SB_HEADS, SB_HEAD_DIM)
    gs = (SSM_GROUPS, SSM_STATE)

    def kv_out(a_t, n_seq, seq_len):
        return a_t.reshape(n_seq, *hd, seq_len).transpose(0, 3, 1, 2)

    return (y_p.reshape(batch, t_len, d), y_s.reshape(n_s, 1, d),
            kv_out(k_p, batch, t_len), kv_out(v_p, batch, t_len),
            st_re_p.reshape(batch, *gs), st_im_p.reshape(batch, *gs),
            kv_out(k_s, 1, n_s).reshape(n_s, 1, *hd), kv_out(v_s, 1, n_s).reshape(n_s, 1, *hd),
            st_re_s.reshape(n_s, *gs), st_im_s.reshape(n_s, *gs))


def kernel(x_prompt, x_sample, cache_k, cache_v, state_ssm_re, state_ssm_im, page_table, c_prompt, c_sample,
           w_ada, b_ada, norm1_g, norm2_g, w_in, q_norm_g, k_norm_g, sb_bias, ssm_lambda_re, ssm_lambda_im,
           ssm_log_dt, ssm_b_re, ssm_b_im, ssm_c_re, ssm_c_im, ssm_d, w_glu_a, w_glu_b, w_sb_out, w_out,
           w_router, router_bias, w_gate_e, w_up_e, w_down_e, w_gate_s, w_up_s, w_down_s):
    depth = w_in.shape[0]
    assert depth == 1, "single-layer step"
    l = 0
    p = dict(
        w_ada=w_ada[l], b_ada=b_ada[l],
        norm1_g=norm1_g[l], norm2_g=norm2_g[l].reshape(1, D_MODEL),
        w_in=w_in[l].astype(BF16), q_norm_g=q_norm_g[l], k_norm_g=k_norm_g[l], sb_bias=sb_bias[l],
        lam_re=ssm_lambda_re[l], lam_im=ssm_lambda_im[l], log_dt=ssm_log_dt[l],
        b_re=ssm_b_re[l], b_im=ssm_b_im[l], c_re=ssm_c_re[l], c_im=ssm_c_im[l], d_skip=ssm_d[l],
        w_glu_a=w_glu_a[l].astype(BF16), w_glu_b=w_glu_b[l].astype(BF16),
        w_sb_out=w_sb_out[l].astype(BF16), w_out=w_out[l].astype(BF16),
        w_router_t=w_router[l].T, router_bias=router_bias[l].reshape(N_EXPERTS, 1),
        w_gate_e=w_gate_e[l], w_up_e=w_up_e[l],
        w_down_e=w_down_e[l].reshape(N_EXPERTS // EXPERTS_PER_STEP, EXPERTS_PER_STEP * EXPERT_HIDDEN, D_MODEL),
        w_gate_s=w_gate_s[l].astype(BF16), w_up_s=w_up_s[l].astype(BF16), w_down_s=w_down_s[l].astype(BF16),
    )
    outs = _layer(x_prompt, x_sample, cache_k[l], cache_v[l], state_ssm_re[l], state_ssm_im[l],
                  page_table, c_prompt, c_sample, p)
    y_p, y_s = outs[0], outs[1]
    return (y_p, y_s) + tuple(o[None] for o in outs[2:])
```

```python
import functools
import math

import jax
import jax.numpy as jnp
from jax import lax
from jax.experimental import pallas as pl
from jax.experimental.pallas import tpu as pltpu

F32 = jnp.float32
BF16 = jnp.bfloat16

D_MODEL = 1024
SSM_WIDTH = 512
SSM_GROUP = 16
SSM_GROUPS = 32
SSM_STATE = 64
SSM_FLAT = SSM_GROUPS * SSM_STATE
EIG_CLIP = -1e-4
SB_HEADS = 8
SB_HEAD_DIM = 64
SB_WIDTH = 512
PAGE_SIZE = 128
N_EXPERTS = 64
TOP_K = 8
N_GROUPS = 8
TOPK_GROUPS = 4
GROUP_SIZE = N_EXPERTS // N_GROUPS
EXPERT_HIDDEN = 256
EXPERTS_PER_STEP = 4
ROUTE_BLOCK = 256
ROUTE_CAP = 64
MAX_FIX_PAIRS = 16
ROUTED_SCALE = 2.5
RMS_EPS = 1e-6

LOG2E = math.log2(math.e)
LOGIT_SCALE_LOG2 = SB_HEAD_DIM ** -0.5 * LOG2E
HEADS_PER_STEP = 4
HEAD_LANES = HEADS_PER_STEP * SB_HEAD_DIM
VMEM_LIMIT = 48 * 1024 * 1024


def _cparams(sem):
    return pltpu.CompilerParams(dimension_semantics=sem, vmem_limit_bytes=VMEM_LIMIT)


def _dot(a, b):
    return jnp.dot(a, b, preferred_element_type=F32)


def _dot_t(a, b):
    return lax.dot_general(a, b, (((1,), (1,)), ((), ())), preferred_element_type=F32)


def _split_bf16(x):
    hi = x.astype(BF16)
    lo = (x - hi.astype(F32)).astype(BF16)
    return hi, lo


def _silu(x):
    return x * jax.nn.sigmoid(x)


def _gelu_tanh(x):
    c = math.sqrt(2.0 / math.pi)
    return 0.5 * x * (1.0 + jnp.tanh(c * (x + 0.044715 * (x * x * x))))


def _ada_kernel(c_ref, w_ref, b_ref, o_ref):
    s = _silu(c_ref[...])
    s_hi, s_lo = _split_bf16(s)
    w_hi, w_lo = _split_bf16(w_ref[...])
    o_ref[...] = _dot(s_hi, w_hi) + _dot(s_hi, w_lo) + _dot(s_lo, w_hi) + b_ref[...]


def _ada_mod(c, w_ada, b_ada):
    rows, d = c.shape
    cols = w_ada.shape[1]
    tn = 1024
    return pl.pallas_call(
        _ada_kernel,
        out_shape=jax.ShapeDtypeStruct((rows, cols), F32),
        grid=(cols // tn,),
        in_specs=[pl.BlockSpec((rows, d), lambda j: (0, 0)),
                  pl.BlockSpec((d, tn), lambda j: (0, j)),
                  pl.BlockSpec((1, tn), lambda j: (0, j))],
        out_specs=pl.BlockSpec((rows, tn), lambda j: (0, j)),
        compiler_params=_cparams(("parallel",)),
        name="ada_mod",
    )(c, w_ada, b_ada.reshape(1, cols))


def _ssm_prep_kernel(lre_ref, lim_ref, ldt_ref, bre_ref, bim_ref,
                     are_ref, aim_ref, bbre_ref, bbim_ref):
    dt = jnp.exp(ldt_ref[...])
    lre = jnp.minimum(lre_ref[...], EIG_CLIP)
    lim = lim_ref[...]
    mag = jnp.exp(lre * dt)
    a_re = mag * jnp.cos(lim * dt)
    a_im = mag * jnp.sin(lim * dt)
    den = lre * lre + lim * lim
    f_re = ((a_re - 1.0) * lre + a_im * lim) / den
    f_im = (a_im * lre - (a_re - 1.0) * lim) / den
    br = bre_ref[...]
    bi = bim_ref[...]
    are_ref[...] = a_re
    aim_ref[...] = a_im
    bbre_ref[...] = f_re * br - f_im * bi
    bbim_ref[...] = f_re * bi + f_im * br


def _ssm_prep(lam_re, lam_im, log_dt, b_re, b_im):
    rep = lambda a: jnp.repeat(a.astype(F32), SSM_GROUP, axis=0)
    lre, lim = rep(lam_re), rep(lam_im)
    ldt = rep(jnp.broadcast_to(log_dt.astype(F32)[:, None], (SSM_GROUPS, SSM_STATE)))
    tr = lambda b: b.astype(F32).transpose(0, 2, 1).reshape(SSM_WIDTH, SSM_STATE)
    shp = jax.ShapeDtypeStruct((SSM_WIDTH, SSM_STATE), F32)
    a_re, a_im, bb_re, bb_im = pl.pallas_call(
        _ssm_prep_kernel, out_shape=(shp, shp, shp, shp), name="ssm_prep",
    )(lre, lim, ldt, tr(b_re), tr(b_im))
    eye = jnp.eye(SSM_GROUPS, dtype=F32)

    def blockdiag_in(bb):
        bb4 = bb.reshape(SSM_GROUPS, SSM_GROUP, 1, SSM_STATE) * eye.reshape(SSM_GROUPS, 1, SSM_GROUPS, 1)
        return bb4.reshape(SSM_WIDTH, SSM_FLAT).astype(BF16)

    a_re = a_re[::SSM_GROUP].reshape(1, SSM_FLAT)
    a_im = a_im[::SSM_GROUP].reshape(1, SSM_FLAT)
    return a_re, a_im, blockdiag_in(bb_re), blockdiag_in(bb_im)


def _blockdiag_out(c):
    eye = jnp.eye(SSM_GROUPS, dtype=F32)
    c4 = c.astype(F32).transpose(0, 2, 1).reshape(SSM_GROUPS, SSM_STATE, 1, SSM_GROUP)
    return (c4 * eye.reshape(SSM_GROUPS, 1, SSM_GROUPS, 1)).reshape(SSM_FLAT, SSM_WIDTH).astype(BF16)


def _inproj_kernel(x_ref, shift_ref, scale_ref, g1_ref, w_ref, qg_ref, kg_ref, hm_ref,
                   u_ref, q_ref, k_ref, kb_ref, v_ref, vb_ref, sga_ref, sgb_ref):
    x = x_ref[...]
    r = lax.rsqrt(jnp.mean(x * x, axis=-1, keepdims=True) + RMS_EPS)
    h = (x * r) * g1_ref[...]
    h = h * (1.0 + scale_ref[...]) + shift_ref[...]
    hb = h.astype(BF16)
    hm = hm_ref[...]

    def head_norm(t, g):
        ms = _dot((t * t).astype(BF16), hm)
        return (t * lax.rsqrt(ms + RMS_EPS)) * g

    u_ref[...] = _dot(hb, w_ref[:, 0:512])
    q = head_norm(_dot(hb, w_ref[:, 512:1024]), qg_ref[...])
    q_ref[...] = (q * LOGIT_SCALE_LOG2).astype(BF16)
    k = head_norm(_dot(hb, w_ref[:, 1024:1536]), kg_ref[...])
    k_ref[...] = k.T
    kb_ref[...] = k.astype(BF16)
    v = _dot(hb, w_ref[:, 1536:2048])
    v_ref[...] = v.T
    vb_ref[...] = v.astype(BF16)
    sga_ref[...] = jax.nn.sigmoid(_dot(hb, w_ref[:, 2048:3072])).astype(BF16)
    sgb_ref[...] = jax.nn.sigmoid(_dot(hb, w_ref[:, 3072:4096])).astype(BF16)


def _head_mean_matrix():
    idx = jnp.arange(SB_WIDTH) // SB_HEAD_DIM
    return ((idx[:, None] == idx[None, :]).astype(F32) / SB_HEAD_DIM).astype(BF16)


def _mod_spec(per_token, tm, rows_per_mod, chunk):
    if per_token:
        return pl.BlockSpec((tm, D_MODEL), lambda i: (i, chunk))
    return pl.BlockSpec((None, 1, D_MODEL), lambda i: (i // rows_per_mod, 0, chunk))


def _in_proj(x, mod, per_token, tm, seq_len, norm1_g, w_in_b, q_g, k_g):
    n = x.shape[0]
    tiles_per_mod = (n // mod.shape[0]) // tm if not per_token else 1
    tiles_per_seq = seq_len // tm
    row = lambda i: (i, 0)
    const = lambda i: (0, 0)
    f32o = lambda w: jax.ShapeDtypeStruct((n, w), F32)
    b16o = lambda w: jax.ShapeDtypeStruct((n, w), BF16)
    kvo = jax.ShapeDtypeStruct((n // seq_len, SB_WIDTH, seq_len), F32)
    kv_blk = pl.BlockSpec((None, SB_WIDTH, tm), lambda i: (i // tiles_per_seq, 0, i % tiles_per_seq))
    blk = lambda w: pl.BlockSpec((tm, w), row)
    return pl.pallas_call(
        _inproj_kernel,
        out_shape=(f32o(512), b16o(512), kvo, b16o(512), kvo, b16o(512), b16o(1024), b16o(1024)),
        grid=(n // tm,),
        in_specs=[blk(D_MODEL),
                  _mod_spec(per_token, tm, tiles_per_mod, 0),
                  _mod_spec(per_token, tm, tiles_per_mod, 1),
                  pl.BlockSpec((1, D_MODEL), const),
                  pl.BlockSpec(w_in_b.shape, const),
                  pl.BlockSpec((1, SB_WIDTH), const),
                  pl.BlockSpec((1, SB_WIDTH), const),
                  pl.BlockSpec((SB_WIDTH, SB_WIDTH), const)],
        out_specs=(blk(512), blk(512), kv_blk, blk(512), kv_blk, blk(512), blk(1024), blk(1024)),
        compiler_params=_cparams(("parallel",)),
        name="in_proj",
    )(x, mod, mod, norm1_g.reshape(1, D_MODEL), w_in_b,
      jnp.tile(q_g, SB_HEADS).reshape(1, SB_WIDTH), jnp.tile(k_g, SB_HEADS).reshape(1, SB_WIDTH),
      _head_mean_matrix())


def _ssm_drive(ub, bb_ref):
    half_c, half_s = SSM_WIDTH // 2, SSM_FLAT // 2
    lo = _dot(ub[:, :half_c], bb_ref[:half_c, :half_s])
    hi = _dot(ub[:, half_c:], bb_ref[half_c:, half_s:])
    return lo, hi


def _ssm_readout(xre, xim, ccre_ref, ccim_ref, d_ref, u):
    half_c, half_s = SSM_WIDTH // 2, SSM_FLAT // 2
    xr = xre.astype(BF16)
    xi = xim.astype(BF16)
    y_lo = _dot(xr[:, :half_s], ccre_ref[:half_s, :half_c]) + _dot(xi[:, :half_s], ccim_ref[:half_s, :half_c])
    y_hi = _dot(xr[:, half_s:], ccre_ref[half_s:, half_c:]) + _dot(xi[:, half_s:], ccim_ref[half_s:, half_c:])
    y = jnp.concatenate([y_lo, y_hi], axis=1) + d_ref[...] * u
    return _gelu_tanh(y)


def _ssm_prompt_kernel(u_ref, bbre_ref, bbim_ref, are_ref, aim_ref, ccre_ref, ccim_ref, d_ref,
                       g_ref, sre_ref, sim_ref, ubuf, gbuf, xre, xim, cre, cim, *, lane_chunk):
    batch, tt, _ = u_ref.shape
    rows = tt * batch
    half_s = SSM_FLAT // 2

    @pl.when(pl.program_id(0) == 0)
    def _():
        cre[...] = jnp.zeros_like(cre)
        cim[...] = jnp.zeros_like(cim)

    n_lane_tiles = SSM_WIDTH // 128
    for b in range(batch):
        for c in range(n_lane_tiles):
            ubuf[c, pl.ds(b, tt, stride=batch), :] = u_ref[b, :, c * 128:(c + 1) * 128]
    u = jnp.concatenate([ubuf[c] for c in range(n_lane_tiles)], axis=1)
    ub = u.astype(BF16)
    lo, hi = _ssm_drive(ub, bbre_ref)
    xre[:, :half_s] = lo
    xre[:, half_s:] = hi
    lo, hi = _ssm_drive(ub, bbim_ref)
    xim[:, :half_s] = lo
    xim[:, half_s:] = hi

    first_step = lax.broadcasted_iota(jnp.int32, (8, lane_chunk), 0) < batch
    for c0 in range(0, SSM_FLAT, lane_chunk):
        cols = pl.ds(c0, lane_chunk)
        ar = jnp.broadcast_to(are_ref[:, cols], (8, lane_chunk))
        ai = jnp.broadcast_to(aim_ref[:, cols], (8, lane_chunk))

        def step(i, carry, cols=cols, ar=ar, ai=ai):
            c_r, c_i = carry
            r0 = pl.multiple_of(i * 8, 8)
            br = xre[pl.ds(r0, 8), cols]
            bi = xim[pl.ds(r0, 8), cols]
            p_r = pltpu.roll(c_r, batch, 0)
            p_i = pltpu.roll(c_i, batch, 0)
            y_r = ar * p_r - ai * p_i + br
            y_i = ar * p_i + ai * p_r + bi
            q_r = pltpu.roll(y_r, batch, 0)
            q_i = pltpu.roll(y_i, batch, 0)
            z_r = ar * q_r - ai * q_i + br
            z_i = ar * q_i + ai * q_r + bi
            xre[pl.ds(r0, 8), cols] = jnp.where(first_step, y_r, z_r)
            xim[pl.ds(r0, 8), cols] = jnp.where(first_step, y_i, z_i)
            return z_r, z_i

        c_r, c_i = lax.fori_loop(0, rows // 8, step, (cre[:, cols], cim[:, cols]))
        cre[:, cols] = c_r
        cim[:, cols] = c_i

    g = _ssm_readout(xre[...], xim[...], ccre_ref, ccim_ref, d_ref, u)
    for c in range(n_lane_tiles):
        gbuf[c] = g[:, c * 128:(c + 1) * 128]
    for b in range(batch):
        g_ref[b] = jnp.concatenate([gbuf[c, pl.ds(b, tt, stride=batch), :] for c in range(n_lane_tiles)],
                                   axis=1).astype(BF16)
    sre_ref[...] = cre[...]
    sim_ref[...] = cim[...]


def _ssm_prompt(u, ssm, tt=128, lane_chunk=512):
    a_re, a_im, bb_re, bb_im, cc_re, cc_im, d_skip = ssm
    batch, t_len, _ = u.shape
    assert 2 * batch == 8, "one sublane tile must hold exactly two time steps"
    rows = tt * batch
    const = lambda j: (0, 0)
    full = lambda a: pl.BlockSpec(a.shape, const)
    seq_blk = pl.BlockSpec((batch, tt, SSM_WIDTH), lambda j: (0, j, 0))
    g, s_re, s_im = pl.pallas_call(
        functools.partial(_ssm_prompt_kernel, lane_chunk=lane_chunk),
        out_shape=(jax.ShapeDtypeStruct((batch, t_len, SSM_WIDTH), BF16),
                   jax.ShapeDtypeStruct((8, SSM_FLAT), F32),
                   jax.ShapeDtypeStruct((8, SSM_FLAT), F32)),
        grid=(t_len // tt,),
        in_specs=[seq_blk,
                  full(bb_re), full(bb_im), full(a_re), full(a_im), full(cc_re), full(cc_im), full(d_skip)],
        out_specs=(seq_blk, pl.BlockSpec((8, SSM_FLAT), const), pl.BlockSpec((8, SSM_FLAT), const)),
        scratch_shapes=[pltpu.VMEM((SSM_WIDTH // 128, rows, 128), F32),
                        pltpu.VMEM((SSM_WIDTH // 128, rows, 128), F32),
                        pltpu.VMEM((rows, SSM_FLAT), F32), pltpu.VMEM((rows, SSM_FLAT), F32),
                        pltpu.VMEM((8, SSM_FLAT), F32), pltpu.VMEM((8, SSM_FLAT), F32)],
        compiler_params=_cparams(("arbitrary",)),
        name="ssm_prompt",
    )(u, bb_re, bb_im, a_re, a_im, cc_re, cc_im, d_skip)
    return g, s_re[batch:], s_im[batch:]


def _ssm_sample_kernel(u_ref, s0re_ref, s0im_ref, bbre_ref, bbim_ref, are_ref, aim_ref,
                       ccre_ref, ccim_ref, d_ref, g_ref, sre_ref, sim_ref):
    u = u_ref[...]
    ub = u.astype(BF16)
    ar, ai = are_ref[...], aim_ref[...]
    s_r, s_i = s0re_ref[...], s0im_ref[...]
    lo, hi = _ssm_drive(ub, bbre_ref)
    x_r = ar * s_r - ai * s_i + jnp.concatenate([lo, hi], axis=1)
    lo, hi = _ssm_drive(ub, bbim_ref)
    x_i = ar * s_i + ai * s_r + jnp.concatenate([lo, hi], axis=1)
    sre_ref[...] = x_r
    sim_ref[...] = x_i
    g_ref[...] = _ssm_readout(x_r, x_i, ccre_ref, ccim_ref, d_ref, u).astype(BF16)


def _ssm_sample(u, s_re, s_im, ssm):
    a_re, a_im, bb_re, bb_im, cc_re, cc_im, d_skip = ssm
    n = u.shape[0]
    return pl.pallas_call(
        _ssm_sample_kernel,
        out_shape=(jax.ShapeDtypeStruct((n, SSM_WIDTH), BF16),
                   jax.ShapeDtypeStruct((n, SSM_FLAT), F32),
                   jax.ShapeDtypeStruct((n, SSM_FLAT), F32)),
        compiler_params=pltpu.CompilerParams(vmem_limit_bytes=VMEM_LIMIT),
        name="ssm_sample",
    )(u, s_re, s_im, bb_re, bb_im, a_re, a_im, cc_re, cc_im, d_skip)


def _log_sigmoid_pair(z2):
    sign = jnp.uint32(0x80000000)
    neg_abs = lax.bitcast_convert_type(lax.bitcast_convert_type(z2, jnp.uint32) | sign, F32)
    soft = jnp.log2(1.0 + jnp.exp2(neg_abs))
    lsig = jnp.minimum(z2, 0.0) - soft
    return lsig, lsig - z2


def _attn_prompt_kernel(bias_ref, q_ref, k_ref, v_ref, tri_ref, o_ref, acc_ref, run_ref, *, tq, tk):
    qi = pl.program_id(1)
    nh = HEADS_PER_STEP
    ngrp = SB_HEADS // nh
    lane_head = lax.broadcasted_iota(jnp.int32, (1, HEAD_LANES), 1) // SB_HEAD_DIM
    head_mask = [(lane_head == h).astype(F32).astype(BF16) for h in range(nh)]
    qs, bias = [], []
    for g in range(ngrp):
        q = q_ref[:, g * HEAD_LANES:(g + 1) * HEAD_LANES]
        qs.append(jnp.concatenate([q * head_mask[h] for h in range(nh)], axis=0))
        bias.append([bias_ref[g * nh + h] for h in range(nh)])
    tri = tri_ref[...]
    acc_ref[...] = jnp.zeros_like(acc_ref)
    run_ref[...] = jnp.zeros_like(run_ref)

    def block(js, masked):
        if masked:
            k0 = pl.multiple_of(js[0] * tk, tk)
            kpos = k0 + lax.broadcasted_iota(jnp.int32, (nh * tq, tk), 1)
            qpos = qi * tq + (lax.broadcasted_iota(jnp.int32, (nh * tq, tk), 0) & (tq - 1))
            valid = kpos < qpos
        for g in range(ngrp):
            lanes = slice(g * HEAD_LANES, (g + 1) * HEAD_LANES)
            run = run_ref[g]
            pv = None
            for j in js:
                k0 = pl.multiple_of(j * tk, tk)
                zs = _dot_t(qs[g], k_ref[pl.ds(k0, tk), lanes])
                lsig_parts, stay_parts = [], []
                for h in range(nh):
                    lsig, stay = _log_sigmoid_pair(zs[h * tq:(h + 1) * tq] + bias[g][h])
                    lsig_parts.append(lsig)
                    stay_parts.append(stay)
                lsig = jnp.concatenate(lsig_parts, axis=0)
                stay = jnp.concatenate(stay_parts, axis=0)
                if masked:
                    stay = jnp.where(valid, stay, 0.0)
                after = _dot(stay.astype(BF16), tri)
                w = jnp.exp2(lsig + after + run)
                if masked:
                    w = jnp.where(valid, w, 0.0)
                part = _dot(w.astype(BF16), v_ref[pl.ds(k0, tk), lanes])
                pv = part if pv is None else pv + part
                run = run + (after[:, 0:1] + stay[:, 0:1])
            acc_ref[g] += pv
            run_ref[g] = run

    j_diag = (qi * tq) // tk
    block((j_diag,), True)

    @pl.when(j_diag % 2 == 1)
    def _():
        block((j_diag - 1,), False)

    def body(jj, carry):
        j = 2 * (j_diag // 2 - jj) - 1
        block((j, j - 1), False)
        return carry

    lax.fori_loop(0, j_diag // 2, body, 0)
    for g in range(ngrp):
        acc = acc_ref[g]
        out = jnp.zeros((tq, HEAD_LANES), F32)
        for h in range(nh):
            out = jnp.where(lane_head == h, acc[h * tq:(h + 1) * tq], out)
        o_ref[:, g * HEAD_LANES:(g + 1) * HEAD_LANES] = out.astype(BF16)


def _strict_upper(tk):
    idx = jnp.arange(tk)
    return (idx[:, None] > idx[None, :]).astype(BF16)


def _attn_prompt(q, kb, vb, sb_bias, batch, t_len, tq=256, tk=256):
    assert tq & (tq - 1) == 0 and tq <= tk, "one masked key block must cover the diagonal"
    n = q.shape[0]
    nq = t_len // tq
    ngrp = SB_HEADS // HEADS_PER_STEP
    return pl.pallas_call(
        functools.partial(_attn_prompt_kernel, tq=tq, tk=tk),
        out_shape=jax.ShapeDtypeStruct((n, SB_WIDTH), BF16),
        grid=(batch, nq),
        in_specs=[pl.BlockSpec(memory_space=pltpu.SMEM),
                  pl.BlockSpec((tq, SB_WIDTH), lambda b, i: (b * nq + i, 0)),
                  pl.BlockSpec((t_len, SB_WIDTH), lambda b, i: (b, 0)),
                  pl.BlockSpec((t_len, SB_WIDTH), lambda b, i: (b, 0)),
                  pl.BlockSpec((tk, tk), lambda b, i: (0, 0))],
        out_specs=pl.BlockSpec((tq, SB_WIDTH), lambda b, i: (b * nq + i, 0)),
        scratch_shapes=[pltpu.VMEM((ngrp, HEADS_PER_STEP * tq, HEAD_LANES), F32),
                        pltpu.VMEM((ngrp, HEADS_PER_STEP * tq, 1), F32)],
        compiler_params=_cparams(("parallel", "arbitrary")),
        name="attn_prompt",
    )(sb_bias.astype(F32) * LOG2E, q, kb, vb, _strict_upper(tk))


def _attn_sample_kernel(pt_ref, q_ref, bias_ref, tri_ref, later_ref, *refs, n_pages):
    del pt_ref
    k_refs = refs[:n_pages]
    v_refs = refs[n_pages:2 * n_pages]
    o_ref = refs[2 * n_pages]
    hd = (SB_HEADS, SB_HEAD_DIM, PAGE_SIZE)
    q = q_ref[0].astype(F32)
    q_col = jnp.broadcast_to(q, (PAGE_SIZE, SB_WIDTH)).T
    q3 = q_col.reshape(hd)
    z = jnp.concatenate([jnp.sum(k_refs[p][0].reshape(hd) * q3, axis=1) for p in range(n_pages)], axis=0)
    lsig, stay = _log_sigmoid_pair(z + bias_ref[...])
    after = _dot(stay.astype(BF16), tri_ref[...])
    total = jnp.broadcast_to(after[:, 0:1] + stay[:, 0:1], stay.shape)
    t_hi, t_lo = _split_bf16(total)
    later = later_ref[...]
    run = _dot(later, t_hi) + _dot(later, t_lo)
    w = jnp.exp2(lsig + after + run)
    acc = [jnp.zeros((SB_HEAD_DIM, PAGE_SIZE), F32) for _ in range(SB_HEADS)]
    for p in range(n_pages):
        for h in range(SB_HEADS):
            r = p * SB_HEADS + h
            acc[h] = acc[h] + v_refs[p][0, h * SB_HEAD_DIM:(h + 1) * SB_HEAD_DIM, :] * w[r:r + 1, :]
    a_hi, a_lo = _split_bf16(jnp.concatenate(acc, axis=0))
    ones = jnp.ones((8, PAGE_SIZE), BF16)
    o_ref[0] = (_dot_t(ones, a_hi) + _dot_t(ones, a_lo))[0:1].astype(BF16)


def _attn_sample(q, cache_k, cache_v, page_table, sb_bias):
    n, n_pages = page_table.shape
    n_pool = cache_k.shape[0]
    ck = cache_k.transpose(0, 2, 3, 1).reshape(n_pool, SB_WIDTH, PAGE_SIZE)
    cv = cache_v.transpose(0, 2, 3, 1).reshape(n_pool, SB_WIDTH, PAGE_SIZE)
    page_spec = lambda p: pl.BlockSpec((1, SB_WIDTH, PAGE_SIZE), lambda i, pt, p=p: (pt[i * n_pages + p], 0, 0))
    tok_spec = pl.BlockSpec((1, 1, SB_WIDTH), lambda i, pt: (i, 0, 0))
    rows = n_pages * SB_HEADS
    const = lambda shape: pl.BlockSpec(shape, lambda i, pt: (0, 0))
    r = jnp.arange(rows)
    later = ((r[:, None] % SB_HEADS == r[None, :] % SB_HEADS)
             & (r[None, :] // SB_HEADS > r[:, None] // SB_HEADS)).astype(BF16)
    out = pl.pallas_call(
        functools.partial(_attn_sample_kernel, n_pages=n_pages),
        out_shape=jax.ShapeDtypeStruct((n, 1, SB_WIDTH), BF16),
        grid_spec=pltpu.PrefetchScalarGridSpec(
            num_scalar_prefetch=1,
            grid=(n,),
            in_specs=[tok_spec, const((rows, 1)), const((PAGE_SIZE, PAGE_SIZE)), const((rows, rows))]
                     + [page_spec(p) for p in range(n_pages)] * 2,
            out_specs=tok_spec),
        compiler_params=_cparams(("parallel",)),
        name="attn_sample",
    )(page_table.reshape(-1).astype(jnp.int32), q.reshape(n, 1, SB_WIDTH),
      jnp.tile(sb_bias.astype(F32) * LOG2E, n_pages).reshape(rows, 1), _strict_upper(PAGE_SIZE), later,
      *([ck] * n_pages), *([cv] * n_pages))
    return out.reshape(n, SB_WIDTH)


def _first_index_of_max(vals, row_id, n_rows):
    m = jnp.max(vals, axis=0, keepdims=True)
    first = jnp.min(jnp.where(vals == m, row_id, n_rows), axis=0, keepdims=True)
    return m, first


def _route(logits_t, rbias):
    tm = logits_t.shape[1]
    scores = jax.nn.sigmoid(logits_t)
    biased = scores + rbias
    row8 = lax.broadcasted_iota(jnp.int32, (GROUP_SIZE, tm), 0)
    neg = jnp.float32(-jnp.inf)
    grp_score = []
    for g in range(N_GROUPS):
        blk = biased[g * GROUP_SIZE:(g + 1) * GROUP_SIZE]
        m1, first = _first_index_of_max(blk, row8, GROUP_SIZE)
        m2 = jnp.max(jnp.where(row8 == first, neg, blk), axis=0, keepdims=True)
        grp_score.append(m1 + m2)
    masked = []
    for g in range(N_GROUPS):
        beaten_by = jnp.zeros((1, tm), jnp.int32)
        for o in range(N_GROUPS):
            if o == g:
                continue
            wins = (grp_score[o] > grp_score[g]) if o > g else (grp_score[o] >= grp_score[g])
            beaten_by = beaten_by + wins.astype(jnp.int32)
        keep = beaten_by < TOPK_GROUPS
        masked.append(jnp.where(keep, biased[g * GROUP_SIZE:(g + 1) * GROUP_SIZE], neg))
    cur = jnp.concatenate(masked, axis=0)
    row = lax.broadcasted_iota(jnp.int32, (N_EXPERTS, tm), 0)
    chosen = jnp.zeros((N_EXPERTS, tm), jnp.bool_)
    for _ in range(TOP_K):
        _, first = _first_index_of_max(cur, row, N_EXPERTS)
        pick = row == first
        chosen = jnp.logical_or(chosen, pick)
        cur = jnp.where(pick, neg, cur)
    sel = jnp.where(chosen, scores, 0.0)
    return sel / jnp.sum(sel, axis=0, keepdims=True) * ROUTED_SCALE


def _post_kernel(x_ref, g_ref, o_ref, sga_ref, sgb_ref, gate1_ref, shift2_ref, scale2_ref, g2_ref,
                 wga_ref, wgb_ref, wsb_ref, wout_ref, wr_ref, rb_ref,
                 x1_ref, h2_ref, gates_ref):
    g = g_ref[...]
    branch_a = _dot(g, wga_ref[...]) * jax.nn.sigmoid(_dot(g, wgb_ref[...]))
    branch_b = _dot(o_ref[...], wsb_ref[...])
    merged = sga_ref[...].astype(F32) * branch_a + sgb_ref[...].astype(F32) * branch_b
    x1 = x_ref[...] + gate1_ref[...] * _dot(merged.astype(BF16), wout_ref[...])
    x1_ref[...] = x1
    r = lax.rsqrt(jnp.mean(x1 * x1, axis=-1, keepdims=True) + RMS_EPS)
    h2 = (x1 * r) * g2_ref[...]
    h2 = h2 * (1.0 + scale2_ref[...]) + shift2_ref[...]
    h2_ref[...] = h2.astype(BF16)
    h_hi, h_lo = _split_bf16(h2)
    w_hi, w_lo = _split_bf16(wr_ref[...])
    logits_t = _dot_t(w_hi, h_hi) + _dot_t(w_hi, h_lo) + _dot_t(w_lo, h_hi)
    gates_ref[...] = _route(logits_t, rb_ref[...])


def _post_mixer(x, g, o, sga, sgb, mod, per_token, tm, p):
    n = x.shape[0]
    tiles_per_mod = (n // mod.shape[0]) // tm if not per_token else 1
    row = lambda i: (i, 0)
    const = lambda i: (0, 0)
    blk = lambda w: pl.BlockSpec((tm, w), row)
    full = lambda a: pl.BlockSpec(a.shape, const)
    ms = lambda c: _mod_spec(per_token, tm, tiles_per_mod, c)
    weights = (p["w_glu_a"], p["w_glu_b"], p["w_sb_out"], p["w_out"], p["w_router_t"], p["router_bias"])
    return pl.pallas_call(
        _post_kernel,
        out_shape=(jax.ShapeDtypeStruct((n, D_MODEL), F32),
                   jax.ShapeDtypeStruct((n, D_MODEL), BF16),
                   jax.ShapeDtypeStruct((N_EXPERTS, n), F32)),
        grid=(n // tm,),
        in_specs=[blk(D_MODEL), blk(512), blk(512), blk(1024), blk(1024), ms(2), ms(3), ms(4),
                  pl.BlockSpec((1, D_MODEL), const)] + [full(w) for w in weights],
        out_specs=(blk(D_MODEL), blk(D_MODEL), pl.BlockSpec((N_EXPERTS, tm), lambda i: (0, i))),
        compiler_params=_cparams(("parallel",)),
        name="post_mixer",
    )(x, g, o, sga, sgb, mod, mod, mod, p["norm2_g"], *weights)


def _moe_kernel(h_ref, gates_ref, x1_ref, gate2_ref, wgs_ref, wus_ref, wds_ref,
                wg_ref, wu_ref, wd_ref, y_ref, acc_ref):
    j = pl.program_id(1)
    h = h_ref[...]

    @pl.when(j == 0)
    def _():
        act = _silu(_dot(h, wgs_ref[...])) * _dot(h, wus_ref[...])
        acc_ref[...] = _dot(act.astype(BF16), wds_ref[...])

    gates = gates_ref[...]
    lane = lax.broadcasted_iota(jnp.int32, gates.shape, 1)
    scaled = []
    for s in range(EXPERTS_PER_STEP):
        act = _silu(_dot(h, wg_ref[s].astype(BF16))) * _dot(h, wu_ref[s].astype(BF16))
        gcol = jnp.sum(jnp.where(lane == j * EXPERTS_PER_STEP + s, gates, 0.0), axis=1, keepdims=True)
        scaled.append((act * gcol).astype(BF16))
    acc_ref[...] += _dot(jnp.concatenate(scaled, axis=1), wd_ref[...].astype(BF16))

    @pl.when(j == pl.num_programs(1) - 1)
    def _():
        y_ref[...] = x1_ref[...] + gate2_ref[...] * acc_ref[...]


def _moe(h2, gates, x1, mod, per_token, tm, p):
    n = h2.shape[0]
    tiles_per_mod = (n // mod.shape[0]) // tm if not per_token else 1
    row = lambda i, e: (i, 0)
    const = lambda i, e: (0, 0)
    exp = lambda i, e: (e, 0, 0)
    if per_token:
        gate2_spec = pl.BlockSpec((tm, D_MODEL), lambda i, e: (i, 5))
    else:
        gate2_spec = pl.BlockSpec((None, 1, D_MODEL), lambda i, e: (i // tiles_per_mod, 0, 5))
    wide = EXPERTS_PER_STEP * EXPERT_HIDDEN
    return pl.pallas_call(
        _moe_kernel,
        out_shape=jax.ShapeDtypeStruct((n, D_MODEL), F32),
        grid=(n // tm, N_EXPERTS // EXPERTS_PER_STEP),
        in_specs=[pl.BlockSpec((tm, D_MODEL), row),
                  pl.BlockSpec((tm, N_EXPERTS), row),
                  pl.BlockSpec((tm, D_MODEL), row),
                  gate2_spec,
                  pl.BlockSpec((D_MODEL, EXPERT_HIDDEN), const),
                  pl.BlockSpec((D_MODEL, EXPERT_HIDDEN), const),
                  pl.BlockSpec((EXPERT_HIDDEN, D_MODEL), const),
                  pl.BlockSpec((EXPERTS_PER_STEP, D_MODEL, EXPERT_HIDDEN), exp),
                  pl.BlockSpec((EXPERTS_PER_STEP, D_MODEL, EXPERT_HIDDEN), exp),
                  pl.BlockSpec((None, wide, D_MODEL), exp)],
        out_specs=pl.BlockSpec((tm, D_MODEL), row),
        scratch_shapes=[pltpu.VMEM((tm, D_MODEL), F32)],
        compiler_params=_cparams(("parallel", "arbitrary")),
        name="moe",
    )(h2, gates, x1, mod, p["w_gate_s"], p["w_up_s"], p["w_down_s"], p["w_gate_e"], p["w_up_e"], p["w_down_e"])


def _dispatch_kernel(gates_ref, h_ref, tri_ref, xs_ref, flag_ref):
    g = gates_ref[...]
    sel = g > 0.0
    rank = _dot(sel.astype(F32).astype(BF16), tri_ref[...])
    rank = jnp.where(sel, rank, -1.0)
    over = jnp.where(rank >= ROUTE_CAP, 1.0, 0.0)
    flag_ref[...] = jnp.broadcast_to(jnp.max(over, axis=1, keepdims=True), flag_ref.shape)
    slot = lax.broadcasted_iota(jnp.int32, (ROUTE_CAP, g.shape[1]), 0).astype(F32)
    h = h_ref[...]
    per_dot = 8
    for e0 in range(0, N_EXPERTS, per_dot):
        place = jnp.concatenate([(slot == rank[e:e + 1, :]).astype(F32).astype(BF16)
                                 for e in range(e0, e0 + per_dot)], axis=0)
        xs_ref[e0 * ROUTE_CAP:(e0 + per_dot) * ROUTE_CAP, :] = _dot(place, h).astype(BF16)


def _expert_kernel(x_ref, wg_ref, wu_ref, wd_ref, y_ref):
    nb, cap, d = x_ref.shape
    x = x_ref[...].reshape(nb * cap, d)
    n_part = 2 if nb % 2 == 0 else 1
    rows = nb * cap // n_part
    wg, wu, wd = wg_ref[...].astype(BF16), wu_ref[...].astype(BF16), wd_ref[...].astype(BF16)
    for r in range(n_part):
        xr = x[r * rows:(r + 1) * rows]
        act = _silu(_dot(xr, wg)) * _dot(xr, wu)
        y = _dot(act.astype(BF16), wd).astype(BF16)
        y_ref[r * (nb // n_part):(r + 1) * (nb // n_part)] = y.reshape(nb // n_part, cap, d)


def _combine_kernel(gates_ref, ys_ref, h_ref, x1_ref, gate2_ref, wgs_ref, wus_ref, wds_ref,
                    tri_ref, y_ref):
    g = gates_ref[...]
    sel = g > 0.0
    rank = _dot(sel.astype(F32).astype(BF16), tri_ref[...])
    rank = jnp.where(sel, rank, -1.0)
    slot = lax.broadcasted_iota(jnp.int32, (ROUTE_CAP, g.shape[1]), 0).astype(F32)
    weigh = jnp.concatenate([jnp.where(slot == rank[e:e + 1, :], g[e:e + 1, :], 0.0).astype(BF16)
                             for e in range(N_EXPERTS)], axis=0)
    routed = lax.dot_general(weigh, ys_ref[...], (((0,), (0,)), ((), ())), preferred_element_type=F32)
    h = h_ref[...]
    act = _silu(_dot(h, wgs_ref[...])) * _dot(h, wus_ref[...])
    shared = _dot(act.astype(BF16), wds_ref[...])
    y_ref[...] = x1_ref[...] + gate2_ref[...] * (shared + routed)


def _moe_routed(h2, gates_t, x1, mod, p):
    assert ROUTE_CAP & (ROUTE_CAP - 1) == 0
    n = h2.shape[0]
    bt = ROUTE_BLOCK
    nb = n // bt
    slots = N_EXPERTS * ROUTE_CAP
    tiles_per_mod = (n // mod.shape[0]) // bt
    later = _strict_upper(bt)
    const2 = lambda b: (0, 0)
    xs, flags = pl.pallas_call(
        _dispatch_kernel,
        out_shape=(jax.ShapeDtypeStruct((nb, slots, D_MODEL), BF16),
                   jax.ShapeDtypeStruct((nb, N_EXPERTS, 128), F32)),
        grid=(nb,),
        in_specs=[pl.BlockSpec((N_EXPERTS, bt), lambda b: (0, b)),
                  pl.BlockSpec((bt, D_MODEL), lambda b: (b, 0)),
                  pl.BlockSpec((bt, bt), const2)],
        out_specs=(pl.BlockSpec((None, slots, D_MODEL), lambda b: (b, 0, 0)),
                   pl.BlockSpec((None, N_EXPERTS, 128), lambda b: (b, 0, 0))),
        compiler_params=_cparams(("parallel",)),
        name="moe_dispatch",
    )(gates_t, h2, later.T)
    gates = gates_t.T

    n_half = 2
    xblk = pl.BlockSpec((nb // n_half, None, ROUTE_CAP, D_MODEL), lambda e, f: (f, e, 0, 0))
    ys = pl.pallas_call(
        _expert_kernel,
        out_shape=jax.ShapeDtypeStruct((nb, N_EXPERTS, ROUTE_CAP, D_MODEL), BF16),
        grid=(N_EXPERTS, n_half),
        in_specs=[xblk,
                  pl.BlockSpec((None, D_MODEL, EXPERT_HIDDEN), lambda e, f: (e, 0, 0)),
                  pl.BlockSpec((None, D_MODEL, EXPERT_HIDDEN), lambda e, f: (e, 0, 0)),
                  pl.BlockSpec((None, EXPERT_HIDDEN, D_MODEL), lambda e, f: (e, 0, 0))],
        out_specs=xblk,
        compiler_params=_cparams(("parallel", "arbitrary")),
        name="moe_experts",
    )(xs.reshape(nb, N_EXPERTS, ROUTE_CAP, D_MODEL), p["w_gate_e"], p["w_up_e"],
      p["w_down_e"].reshape(N_EXPERTS, EXPERT_HIDDEN, D_MODEL))
    blk = lambda w: pl.BlockSpec((bt, w), lambda b: (b, 0))
    full = lambda a: pl.BlockSpec(a.shape, const2)
    y = pl.pallas_call(
        _combine_kernel,
        out_shape=jax.ShapeDtypeStruct((n, D_MODEL), F32),
        grid=(nb,),
        in_specs=[pl.BlockSpec((N_EXPERTS, bt), lambda b: (0, b)),
                  pl.BlockSpec((None, slots, D_MODEL), lambda b: (b, 0, 0)),
                  blk(D_MODEL), blk(D_MODEL), _mod_spec(False, bt, tiles_per_mod, 5),
                  full(p["w_gate_s"]), full(p["w_up_s"]), full(p["w_down_s"]), pl.BlockSpec((bt, bt), const2)],
        out_specs=blk(D_MODEL),
        compiler_params=_cparams(("parallel",)),
        name="moe_combine",
    )(gates_t, ys.reshape(nb, slots, D_MODEL), h2, x1, mod, p["w_gate_s"], p["w_up_s"], p["w_down_s"],
      later.T)

    over = (flags[:, :, 0] > 0.0).reshape(-1)
    n_over = jnp.sum(over.astype(jnp.int32))
    pairs = jnp.argsort(jnp.logical_not(over), stable=True)[:MAX_FIX_PAIRS].astype(jnp.int32)
    case = (n_over > 0).astype(jnp.int32) + (n_over > MAX_FIX_PAIRS).astype(jnp.int32)
    return lax.switch(case, [lambda: y,
                             lambda: _moe_fix(y, pairs // N_EXPERTS, pairs % N_EXPERTS, n_over.reshape(1),
                                              h2, gates, mod, later, p),
                             lambda: _moe(h2, gates, x1, mod, False, 512, p)])


def _moe_fix_kernel(block_ref, expert_ref, n_ref, h_ref, gates_ref, gate2_ref, wg_ref, wu_ref, wd_ref,
                    tri_ref, y_in_ref, y_ref):
    i = pl.program_id(0)
    active = i < n_ref[0]
    new_block = jnp.logical_or(i == 0, block_ref[i] != block_ref[jnp.maximum(i - 1, 0)])

    @pl.when(jnp.logical_and(active, new_block))
    def _():
        y_ref[...] = y_in_ref[...]

    @pl.when(active)
    def _():
        g = gates_ref[...]
        sel = g > 0.0
        rank = _dot(tri_ref[...], sel.astype(F32).astype(BF16))
        lane = lax.broadcasted_iota(jnp.int32, g.shape, 1)
        missed = jnp.logical_and(jnp.logical_and(sel, rank >= ROUTE_CAP), lane == expert_ref[i])
        gcol = jnp.sum(jnp.where(missed, g, 0.0), axis=1, keepdims=True)
        h = h_ref[...]
        act = _silu(_dot(h, wg_ref[...].astype(BF16))) * _dot(h, wu_ref[...].astype(BF16))
        y_ref[...] += gate2_ref[...] * _dot((act * gcol).astype(BF16), wd_ref[...].astype(BF16))


def _moe_fix(y, pair_block, pair_expert, n_pairs, h2, gates, mod, later, p):
    n = h2.shape[0]
    bt = ROUTE_BLOCK
    tiles_per_mod = (n // mod.shape[0]) // bt
    last = lambda i, nf: jnp.minimum(i, jnp.maximum(nf[0] - 1, 0))
    row = lambda i, pb, pe, nf: (pb[last(i, nf)], 0)
    exp = lambda i, pb, pe, nf: (pe[last(i, nf)], 0, 0)
    return pl.pallas_call(
        _moe_fix_kernel,
        out_shape=jax.ShapeDtypeStruct((n, D_MODEL), F32),
        grid_spec=pltpu.PrefetchScalarGridSpec(
            num_scalar_prefetch=3,
            grid=(MAX_FIX_PAIRS,),
            in_specs=[pl.BlockSpec((bt, D_MODEL), row),
                      pl.BlockSpec((bt, N_EXPERTS), row),
                      pl.BlockSpec((None, 1, D_MODEL), lambda i, pb, pe, nf: (pb[last(i, nf)] // tiles_per_mod, 0, 5)),
                      pl.BlockSpec((None, D_MODEL, EXPERT_HIDDEN), exp),
                      pl.BlockSpec((None, D_MODEL, EXPERT_HIDDEN), exp),
                      pl.BlockSpec((None, EXPERT_HIDDEN, D_MODEL), exp),
                      pl.BlockSpec((bt, bt), lambda i, pb, pe, nf: (0, 0)),
                      pl.BlockSpec((bt, D_MODEL), row)],
            out_specs=pl.BlockSpec((bt, D_MODEL), row)),
        input_output_aliases={10: 0},
        compiler_params=_cparams(("arbitrary",)),
        name="moe_fix",
    )(pair_block, pair_expert, n_pairs, h2, gates, mod, p["w_gate_e"], p["w_up_e"],
      p["w_down_e"].reshape(N_EXPERTS, EXPERT_HIDDEN, D_MODEL), later, y)


def _layer(x_prompt, x_sample, cache_k, cache_v, s0_re, s0_im, page_table, c_prompt, c_sample, p):
    batch, t_len, d = x_prompt.shape
    n_s = x_sample.shape[0]
    n_p = batch * t_len

    c_all = jnp.concatenate([c_prompt, c_sample], axis=0)
    pad = (-c_all.shape[0]) % 8
    c_all = jnp.pad(c_all, ((0, pad), (0, 0)))
    mod = _ada_mod(c_all, p["w_ada"], p["b_ada"])
    mod_p = mod[:batch].reshape(batch, 1, 6 * d)
    mod_s = mod[batch:batch + n_s]

    a_re, a_im, bb_re, bb_im = _ssm_prep(p["lam_re"], p["lam_im"], p["log_dt"], p["b_re"], p["b_im"])
    ssm = (a_re, a_im, bb_re, bb_im, _blockdiag_out(p["c_re"]), _blockdiag_out(-p["c_im"]),
           p["d_skip"].astype(F32).reshape(1, SSM_WIDTH))

    xp = x_prompt.reshape(n_p, d)
    xs = x_sample.reshape(n_s, d)
    proj_p = _in_proj(xp, mod_p, False, 512, t_len, p["norm1_g"], p["w_in"], p["q_norm_g"], p["k_norm_g"])
    proj_s = _in_proj(xs, mod_s, True, n_s, n_s, p["norm1_g"], p["w_in"], p["q_norm_g"], p["k_norm_g"])
    u_p, q_p, k_p, kb_p, v_p, vb_p, sga_p, sgb_p = proj_p
    u_s, q_s, k_s, _, v_s, _, sga_s, sgb_s = proj_s

    g_p, st_re_p, st_im_p = _ssm_prompt(u_p.reshape(batch, t_len, SSM_WIDTH), ssm)
    g_p = g_p.reshape(n_p, SSM_WIDTH)
    g_s, st_re_s, st_im_s = _ssm_sample(u_s, s0_re.reshape(n_s, SSM_FLAT), s0_im.reshape(n_s, SSM_FLAT), ssm)

    o_p = _attn_prompt(q_p, kb_p, vb_p, p["sb_bias"], batch, t_len)
    o_s = _attn_sample(q_s, cache_k, cache_v, page_table, p["sb_bias"])

    x1_p, h2_p, gates_p = _post_mixer(xp, g_p, o_p, sga_p, sgb_p, mod_p, False, 512, p)
    x1_s, h2_s, gates_s = _post_mixer(xs, g_s, o_s, sga_s, sgb_s, mod_s, True, n_s, p)

    y_p = _moe_routed(h2_p, gates_p, x1_p, mod_p, p)
    y_s = _moe(h2_s, gates_s.T, x1_s, mod_s, True, n_s, p)

    hd = (SB_HEADS, SB_HEAD_DIM)
    gs = (SSM_GROUPS, SSM_STATE)

    def kv_out(a_t, n_seq, seq_len):
        return a_t.reshape(n_seq, *hd, seq_len).transpose(0, 3, 1, 2)

    return (y_p.reshape(batch, t_len, d), y_s.reshape(n_s, 1, d),
            kv_out(k_p, batch, t_len), kv_out(v_p, batch, t_len),
            st_re_p.reshape(batch, *gs), st_im_p.reshape(batch, *gs),
            kv_out(k_s, 1, n_s).reshape(n_s, 1, *hd), kv_out(v_s, 1, n_s).reshape(n_s, 1, *hd),
            st_re_s.reshape(n_s, *gs), st_im_s.reshape(n_s, *gs))


def kernel(x_prompt, x_sample, cache_k, cache_v, state_ssm_re, state_ssm_im, page_table, c_prompt, c_sample,
           w_ada, b_ada, norm1_g, norm2_g, w_in, q_norm_g, k_norm_g, sb_bias, ssm_lambda_re, ssm_lambda_im,
           ssm_log_dt, ssm_b_re, ssm_b_im, ssm_c_re, ssm_c_im, ssm_d, w_glu_a, w_glu_b, w_sb_out, w_out,
           w_router, router_bias, w_gate_e, w_up_e, w_down_e, w_gate_s, w_up_s, w_down_s):
    depth = w_in.shape[0]
    assert depth == 1, "single-layer step"
    l = 0
    p = dict(
        w_ada=w_ada[l], b_ada=b_ada[l],
        norm1_g=norm1_g[l], norm2_g=norm2_g[l].reshape(1, D_MODEL),
        w_in=w_in[l].astype(BF16), q_norm_g=q_norm_g[l], k_norm_g=k_norm_g[l], sb_bias=sb_bias[l],
        lam_re=ssm_lambda_re[l], lam_im=ssm_lambda_im[l], log_dt=ssm_log_dt[l],
        b_re=ssm_b_re[l], b_im=ssm_b_im[l], c_re=ssm_c_re[l], c_im=ssm_c_im[l], d_skip=ssm_d[l],
        w_glu_a=w_glu_a[l].astype(BF16), w_glu_b=w_glu_b[l].astype(BF16),
        w_sb_out=w_sb_out[l].astype(BF16), w_out=w_out[l].astype(BF16),
        w_router_t=w_router[l].T, router_bias=router_bias[l].reshape(N_EXPERTS, 1),
        w_gate_e=w_gate_e[l], w_up_e=w_up_e[l],
        w_down_e=w_down_e[l].reshape(N_EXPERTS // EXPERTS_PER_STEP, EXPERTS_PER_STEP * EXPERT_HIDDEN, D_MODEL),
        w_gate_s=w_gate_s[l].astype(BF16), w_up_s=w_up_s[l].astype(BF16), w_down_s=w_down_s[l].astype(BF16),
    )
    outs = _layer(x_prompt, x_sample, cache_k[l], cache_v[l], state_ssm_re[l], state_ssm_im[l],
                  page_table, c_prompt, c_sample, p)
    y_p, y_s = outs[0], outs[1]
    return (y_p, y_s) + tuple(o[None] for o in outs[2:])
```

```python
import functools
import math

import jax
import jax.numpy as jnp
from jax import lax
from jax.experimental import pallas as pl
from jax.experimental.pallas import tpu as pltpu

F32 = jnp.float32
BF16 = jnp.bfloat16

D_MODEL = 1024
SSM_WIDTH = 512
SSM_GROUP = 16
SSM_GROUPS = 32
SSM_STATE = 64
SSM_FLAT = SSM_GROUPS * SSM_STATE
EIG_CLIP = -1e-4
SB_HEADS = 8
SB_HEAD_DIM = 64
SB_WIDTH = 512
PAGE_SIZE = 128
N_EXPERTS = 64
TOP_K = 8
N_GROUPS = 8
TOPK_GROUPS = 4
GROUP_SIZE = N_EXPERTS // N_GROUPS
EXPERT_HIDDEN = 256
EXPERTS_PER_STEP = 4
ROUTE_BLOCK = 256
ROUTE_CAP = 64
MAX_FIX_PAIRS = 16
ROUTED_SCALE = 2.5
RMS_EPS = 1e-6

LOG2E = math.log2(math.e)
LOGIT_SCALE_LOG2 = SB_HEAD_DIM ** -0.5 * LOG2E
HEADS_PER_STEP = 4
HEAD_LANES = HEADS_PER_STEP * SB_HEAD_DIM
VMEM_LIMIT = 48 * 1024 * 1024


def _cparams(sem):
    return pltpu.CompilerParams(dimension_semantics=sem, vmem_limit_bytes=VMEM_LIMIT)


def _dot(a, b):
    return jnp.dot(a, b, preferred_element_type=F32)


def _dot_t(a, b):
    return lax.dot_general(a, b, (((1,), (1,)), ((), ())), preferred_element_type=F32)


def _split_bf16(x):
    hi = x.astype(BF16)
    lo = (x - hi.astype(F32)).astype(BF16)
    return hi, lo


def _silu(x):
    return x * jax.nn.sigmoid(x)


def _gelu_tanh(x):
    c = math.sqrt(2.0 / math.pi)
    return 0.5 * x * (1.0 + jnp.tanh(c * (x + 0.044715 * (x * x * x))))


def _ada_kernel(c_ref, w_ref, b_ref, o_ref):
    s = _silu(c_ref[...])
    s_hi, s_lo = _split_bf16(s)
    w_hi, w_lo = _split_bf16(w_ref[...])
    o_ref[...] = _dot(s_hi, w_hi) + _dot(s_hi, w_lo) + _dot(s_lo, w_hi) + b_ref[...]


def _ada_mod(c, w_ada, b_ada):
    rows, d = c.shape
    cols = w_ada.shape[1]
    tn = 1024
    return pl.pallas_call(
        _ada_kernel,
        out_shape=jax.ShapeDtypeStruct((rows, cols), F32),
        grid=(cols // tn,),
        in_specs=[pl.BlockSpec((rows, d), lambda j: (0, 0)),
                  pl.BlockSpec((d, tn), lambda j: (0, j)),
                  pl.BlockSpec((1, tn), lambda j: (0, j))],
        out_specs=pl.BlockSpec((rows, tn), lambda j: (0, j)),
        compiler_params=_cparams(("parallel",)),
        name="ada_mod",
    )(c, w_ada, b_ada.reshape(1, cols))


def _ssm_prep_kernel(lre_ref, lim_ref, ldt_ref, bre_ref, bim_ref,
                     are_ref, aim_ref, bbre_ref, bbim_ref):
    dt = jnp.exp(ldt_ref[...])
    lre = jnp.minimum(lre_ref[...], EIG_CLIP)
    lim = lim_ref[...]
    mag = jnp.exp(lre * dt)
    a_re = mag * jnp.cos(lim * dt)
    a_im = mag * jnp.sin(lim * dt)
    den = lre * lre + lim * lim
    f_re = ((a_re - 1.0) * lre + a_im * lim) / den
    f_im = (a_im * lre - (a_re - 1.0) * lim) / den
    br = bre_ref[...]
    bi = bim_ref[...]
    are_ref[...] = a_re
    aim_ref[...] = a_im
    bbre_ref[...] = f_re * br - f_im * bi
    bbim_ref[...] = f_re * bi + f_im * br


def _ssm_prep(lam_re, lam_im, log_dt, b_re, b_im):
    rep = lambda a: jnp.repeat(a.astype(F32), SSM_GROUP, axis=0)
    lre, lim = rep(lam_re), rep(lam_im)
    ldt = rep(jnp.broadcast_to(log_dt.astype(F32)[:, None], (SSM_GROUPS, SSM_STATE)))
    tr = lambda b: b.astype(F32).transpose(0, 2, 1).reshape(SSM_WIDTH, SSM_STATE)
    shp = jax.ShapeDtypeStruct((SSM_WIDTH, SSM_STATE), F32)
    a_re, a_im, bb_re, bb_im = pl.pallas_call(
        _ssm_prep_kernel, out_shape=(shp, shp, shp, shp), name="ssm_prep",
    )(lre, lim, ldt, tr(b_re), tr(b_im))
    eye = jnp.eye(SSM_GROUPS, dtype=F32)

    def blockdiag_in(bb):
        bb4 = bb.reshape(SSM_GROUPS, SSM_GROUP, 1, SSM_STATE) * eye.reshape(SSM_GROUPS, 1, SSM_GROUPS, 1)
        return bb4.reshape(SSM_WIDTH, SSM_FLAT).astype(BF16)

    a_re = a_re[::SSM_GROUP].reshape(1, SSM_FLAT)
    a_im = a_im[::SSM_GROUP].reshape(1, SSM_FLAT)
    return a_re, a_im, blockdiag_in(bb_re), blockdiag_in(bb_im)


def _blockdiag_out(c):
    eye = jnp.eye(SSM_GROUPS, dtype=F32)
    c4 = c.astype(F32).transpose(0, 2, 1).reshape(SSM_GROUPS, SSM_STATE, 1, SSM_GROUP)
    return (c4 * eye.reshape(SSM_GROUPS, 1, SSM_GROUPS, 1)).reshape(SSM_FLAT, SSM_WIDTH).astype(BF16)


def _inproj_kernel(x_ref, shift_ref, scale_ref, g1_ref, w_ref, qg_ref, kg_ref, hm_ref,
                   u_ref, q_ref, k_ref, kb_ref, v_ref, vb_ref, sga_ref, sgb_ref):
    x = x_ref[...]
    r = lax.rsqrt(jnp.mean(x * x, axis=-1, keepdims=True) + RMS_EPS)
    h = (x * r) * g1_ref[...]
    h = h * (1.0 + scale_ref[...]) + shift_ref[...]
    hb = h.astype(BF16)
    hm = hm_ref[...]

    def head_norm(t, g):
        ms = _dot((t * t).astype(BF16), hm)
        return (t * lax.rsqrt(ms + RMS_EPS)) * g

    u_ref[...] = _dot(hb, w_ref[:, 0:512])
    q = head_norm(_dot(hb, w_ref[:, 512:1024]), qg_ref[...])
    q_ref[...] = (q * LOGIT_SCALE_LOG2).astype(BF16)
    k = head_norm(_dot(hb, w_ref[:, 1024:1536]), kg_ref[...])
    k_ref[...] = k.T
    kb_ref[...] = k.astype(BF16)
    v = _dot(hb, w_ref[:, 1536:2048])
    v_ref[...] = v.T
    vb_ref[...] = v.astype(BF16)
    sga_ref[...] = jax.nn.sigmoid(_dot(hb, w_ref[:, 2048:3072])).astype(BF16)
    sgb_ref[...] = jax.nn.sigmoid(_dot(hb, w_ref[:, 3072:4096])).astype(BF16)


def _head_mean_matrix():
    idx = jnp.arange(SB_WIDTH) // SB_HEAD_DIM
    return ((idx[:, None] == idx[None, :]).astype(F32) / SB_HEAD_DIM).astype(BF16)


def _mod_spec(per_token, tm, rows_per_mod, chunk):
    if per_token:
        return pl.BlockSpec((tm, D_MODEL), lambda i: (i, chunk))
    return pl.BlockSpec((None, 1, D_MODEL), lambda i: (i // rows_per_mod, 0, chunk))


def _in_proj(x, mod, per_token, tm, seq_len, norm1_g, w_in_b, q_g, k_g):
    n = x.shape[0]
    tiles_per_mod = (n // mod.shape[0]) // tm if not per_token else 1
    tiles_per_seq = seq_len // tm
    row = lambda i: (i, 0)
    const = lambda i: (0, 0)
    f32o = lambda w: jax.ShapeDtypeStruct((n, w), F32)
    b16o = lambda w: jax.ShapeDtypeStruct((n, w), BF16)
    kvo = jax.ShapeDtypeStruct((n // seq_len, SB_WIDTH, seq_len), F32)
    kv_blk = pl.BlockSpec((None, SB_WIDTH, tm), lambda i: (i // tiles_per_seq, 0, i % tiles_per_seq))
    blk = lambda w: pl.BlockSpec((tm, w), row)
    return pl.pallas_call(
        _inproj_kernel,
        out_shape=(f32o(512), b16o(512), kvo, b16o(512), kvo, b16o(512), b16o(1024), b16o(1024)),
        grid=(n // tm,),
        in_specs=[blk(D_MODEL),
                  _mod_spec(per_token, tm, tiles_per_mod, 0),
                  _mod_spec(per_token, tm, tiles_per_mod, 1),
                  pl.BlockSpec((1, D_MODEL), const),
                  pl.BlockSpec(w_in_b.shape, const),
                  pl.BlockSpec((1, SB_WIDTH), const),
                  pl.BlockSpec((1, SB_WIDTH), const),
                  pl.BlockSpec((SB_WIDTH, SB_WIDTH), const)],
        out_specs=(blk(512), blk(512), kv_blk, blk(512), kv_blk, blk(512), blk(1024), blk(1024)),
        compiler_params=_cparams(("parallel",)),
        name="in_proj",
    )(x, mod, mod, norm1_g.reshape(1, D_MODEL), w_in_b,
      jnp.tile(q_g, SB_HEADS).reshape(1, SB_WIDTH), jnp.tile(k_g, SB_HEADS).reshape(1, SB_WIDTH),
      _head_mean_matrix())


def _ssm_drive(ub, bb_ref):
    half_c, half_s = SSM_WIDTH // 2, SSM_FLAT // 2
    lo = _dot(ub[:, :half_c], bb_ref[:half_c, :half_s])
    hi = _dot(ub[:, half_c:], bb_ref[half_c:, half_s:])
    return lo, hi


def _ssm_readout(xre, xim, ccre_ref, ccim_ref, d_ref, u):
    half_c, half_s = SSM_WIDTH // 2, SSM_FLAT // 2
    xr = xre.astype(BF16)
    xi = xim.astype(BF16)
    y_lo = _dot(xr[:, :half_s], ccre_ref[:half_s, :half_c]) + _dot(xi[:, :half_s], ccim_ref[:half_s, :half_c])
    y_hi = _dot(xr[:, half_s:], ccre_ref[half_s:, half_c:]) + _dot(xi[:, half_s:], ccim_ref[half_s:, half_c:])
    y = jnp.concatenate([y_lo, y_hi], axis=1) + d_ref[...] * u
    return _gelu_tanh(y)


def _ssm_prompt_kernel(u_ref, bbre_ref, bbim_ref, are_ref, aim_ref, ccre_ref, ccim_ref, d_ref,
                       g_ref, sre_ref, sim_ref, ubuf, gbuf, xre, xim, cre, cim, *, lane_chunk):
    batch, tt, _ = u_ref.shape
    rows = tt * batch
    half_s = SSM_FLAT // 2

    @pl.when(pl.program_id(0) == 0)
    def _():
        cre[...] = jnp.zeros_like(cre)
        cim[...] = jnp.zeros_like(cim)

    n_lane_tiles = SSM_WIDTH // 128
    for b in range(batch):
        for c in range(n_lane_tiles):
            ubuf[c, pl.ds(b, tt, stride=batch), :] = u_ref[b, :, c * 128:(c + 1) * 128]
    u = jnp.concatenate([ubuf[c] for c in range(n_lane_tiles)], axis=1)
    ub = u.astype(BF16)
    lo, hi = _ssm_drive(ub, bbre_ref)
    xre[:, :half_s] = lo
    xre[:, half_s:] = hi
    lo, hi = _ssm_drive(ub, bbim_ref)
    xim[:, :half_s] = lo
    xim[:, half_s:] = hi

    first_step = lax.broadcasted_iota(jnp.int32, (8, lane_chunk), 0) < batch
    for c0 in range(0, SSM_FLAT, lane_chunk):
        cols = pl.ds(c0, lane_chunk)
        ar = jnp.broadcast_to(are_ref[:, cols], (8, lane_chunk))
        ai = jnp.broadcast_to(aim_ref[:, cols], (8, lane_chunk))

        def step(i, carry, cols=cols, ar=ar, ai=ai):
            c_r, c_i = carry
            r0 = pl.multiple_of(i * 8, 8)
            br = xre[pl.ds(r0, 8), cols]
            bi = xim[pl.ds(r0, 8), cols]
            p_r = pltpu.roll(c_r, batch, 0)
            p_i = pltpu.roll(c_i, batch, 0)
            y_r = ar * p_r - ai * p_i + br
            y_i = ar * p_i + ai * p_r + bi
            q_r = pltpu.roll(y_r, batch, 0)
            q_i = pltpu.roll(y_i, batch, 0)
            z_r = ar * q_r - ai * q_i + br
            z_i = ar * q_i + ai * q_r + bi
            xre[pl.ds(r0, 8), cols] = jnp.where(first_step, y_r, z_r)
            xim[pl.ds(r0, 8), cols] = jnp.where(first_step, y_i, z_i)
            return z_r, z_i

        c_r, c_i = lax.fori_loop(0, rows // 8, step, (cre[:, cols], cim[:, cols]))
        cre[:, cols] = c_r
        cim[:, cols] = c_i

    g = _ssm_readout(xre[...], xim[...], ccre_ref, ccim_ref, d_ref, u)
    for c in range(n_lane_tiles):
        gbuf[c] = g[:, c * 128:(c + 1) * 128]
    for b in range(batch):
        g_ref[b] = jnp.concatenate([gbuf[c, pl.ds(b, tt, stride=batch), :] for c in range(n_lane_tiles)],
                                   axis=1).astype(BF16)
    sre_ref[...] = cre[...]
    sim_ref[...] = cim[...]


def _ssm_prompt(u, ssm, tt=128, lane_chunk=512):
    a_re, a_im, bb_re, bb_im, cc_re, cc_im, d_skip = ssm
    batch, t_len, _ = u.shape
    assert 2 * batch == 8, "one sublane tile must hold exactly two time steps"
    rows = tt * batch
    const = lambda j: (0, 0)
    full = lambda a: pl.BlockSpec(a.shape, const)
    seq_blk = pl.BlockSpec((batch, tt, SSM_WIDTH), lambda j: (0, j, 0))
    g, s_re, s_im = pl.pallas_call(
        functools.partial(_ssm_prompt_kernel, lane_chunk=lane_chunk),
        out_shape=(jax.ShapeDtypeStruct((batch, t_len, SSM_WIDTH), BF16),
                   jax.ShapeDtypeStruct((8, SSM_FLAT), F32),
                   jax.ShapeDtypeStruct((8, SSM_FLAT), F32)),
        grid=(t_len // tt,),
        in_specs=[seq_blk,
                  full(bb_re), full(bb_im), full(a_re), full(a_im), full(cc_re), full(cc_im), full(d_skip)],
        out_specs=(seq_blk, pl.BlockSpec((8, SSM_FLAT), const), pl.BlockSpec((8, SSM_FLAT), const)),
        scratch_shapes=[pltpu.VMEM((SSM_WIDTH // 128, rows, 128), F32),
                        pltpu.VMEM((SSM_WIDTH // 128, rows, 128), F32),
                        pltpu.VMEM((rows, SSM_FLAT), F32), pltpu.VMEM((rows, SSM_FLAT), F32),
                        pltpu.VMEM((8, SSM_FLAT), F32), pltpu.VMEM((8, SSM_FLAT), F32)],
        compiler_params=_cparams(("arbitrary",)),
        name="ssm_prompt",
    )(u, bb_re, bb_im, a_re, a_im, cc_re, cc_im, d_skip)
    return g, s_re[batch:], s_im[batch:]


def _ssm_sample_kernel(u_ref, s0re_ref, s0im_ref, bbre_ref, bbim_ref, are_ref, aim_ref,
                       ccre_ref, ccim_ref, d_ref, g_ref, sre_ref, sim_ref):
    u = u_ref[...]
    ub = u.astype(BF16)
    ar, ai = are_ref[...], aim_ref[...]
    s_r, s_i = s0re_ref[...], s0im_ref[...]
    lo, hi = _ssm_drive(ub, bbre_ref)
    x_r = ar * s_r - ai * s_i + jnp.concatenate([lo, hi], axis=1)
    lo, hi = _ssm_drive(ub, bbim_ref)
    x_i = ar * s_i + ai * s_r + jnp.concatenate([lo, hi], axis=1)
    sre_ref[...] = x_r
    sim_ref[...] = x_i
    g_ref[...] = _ssm_readout(x_r, x_i, ccre_ref, ccim_ref, d_ref, u).astype(BF16)


def _ssm_sample(u, s_re, s_im, ssm):
    a_re, a_im, bb_re, bb_im, cc_re, cc_im, d_skip = ssm
    n = u.shape[0]
    return pl.pallas_call(
        _ssm_sample_kernel,
        out_shape=(jax.ShapeDtypeStruct((n, SSM_WIDTH), BF16),
                   jax.ShapeDtypeStruct((n, SSM_FLAT), F32),
                   jax.ShapeDtypeStruct((n, SSM_FLAT), F32)),
        compiler_params=pltpu.CompilerParams(vmem_limit_bytes=VMEM_LIMIT),
        name="ssm_sample",
    )(u, s_re, s_im, bb_re, bb_im, a_re, a_im, cc_re, cc_im, d_skip)


def _log_sigmoid_pair(z2):
    sign = jnp.uint32(0x80000000)
    neg_abs = lax.bitcast_convert_type(lax.bitcast_convert_type(z2, jnp.uint32) | sign, F32)
    soft = jnp.log2(1.0 + jnp.exp2(neg_abs))
    lsig = jnp.minimum(z2, 0.0) - soft
    return lsig, lsig - z2


def _attn_prompt_kernel(bias_ref, q_ref, k_ref, v_ref, tri_ref, o_ref, acc_ref, run_ref, *, tq, tk):
    qi = pl.program_id(1)
    nh = HEADS_PER_STEP
    ngrp = SB_HEADS // nh
    lane_head = lax.broadcasted_iota(jnp.int32, (1, HEAD_LANES), 1) // SB_HEAD_DIM
    head_mask = [(lane_head == h).astype(F32).astype(BF16) for h in range(nh)]
    qs, bias = [], []
    for g in range(ngrp):
        q = q_ref[:, g * HEAD_LANES:(g + 1) * HEAD_LANES]
        qs.append(jnp.concatenate([q * head_mask[h] for h in range(nh)], axis=0))
        bias.append([bias_ref[g * nh + h] for h in range(nh)])
    tri = tri_ref[...]
    run_ref[...] = jnp.zeros_like(run_ref)

    def block(js, masked):
        if masked:
            k0 = pl.multiple_of(js[0] * tk, tk)
            kpos = k0 + lax.broadcasted_iota(jnp.int32, (nh * tq, tk), 1)
            qpos = qi * tq + (lax.broadcasted_iota(jnp.int32, (nh * tq, tk), 0) & (tq - 1))
            valid = kpos < qpos
        for g in range(ngrp):
            lanes = slice(g * HEAD_LANES, (g + 1) * HEAD_LANES)
            run = run_ref[g]
            pv = None
            for j in js:
                k0 = pl.multiple_of(j * tk, tk)
                zs = _dot_t(qs[g], k_ref[pl.ds(k0, tk), lanes])
                lsig_parts, stay_parts = [], []
                for h in range(nh):
                    lsig, stay = _log_sigmoid_pair(zs[h * tq:(h + 1) * tq] + bias[g][h])
                    lsig_parts.append(lsig)
                    stay_parts.append(stay)
                lsig = jnp.concatenate(lsig_parts, axis=0)
                stay = jnp.concatenate(stay_parts, axis=0)
                if masked:
                    stay = jnp.where(valid, stay, 0.0)
                after = _dot(stay.astype(BF16), tri)
                w = jnp.exp2(lsig + after + run)
                if masked:
                    w = jnp.where(valid, w, 0.0)
                part = _dot(w.astype(BF16), v_ref[pl.ds(k0, tk), lanes])
                pv = part if pv is None else pv + part
                run = run + (after[:, 0:1] + stay[:, 0:1])
            if masked:
                acc_ref[g] = pv
            else:
                acc_ref[g] += pv
            run_ref[g] = run

    j_diag = (qi * tq) // tk
    block((j_diag,), True)

    @pl.when(j_diag % 2 == 1)
    def _():
        block((j_diag - 1,), False)

    def body(jj, carry):
        j = 2 * (j_diag // 2 - jj) - 1
        block((j, j - 1), False)
        return carry

    lax.fori_loop(0, j_diag // 2, body, 0)
    for g in range(ngrp):
        acc = acc_ref[g]
        out = jnp.zeros((tq, HEAD_LANES), F32)
        for h in range(nh):
            out = jnp.where(lane_head == h, acc[h * tq:(h + 1) * tq], out)
        o_ref[:, g * HEAD_LANES:(g + 1) * HEAD_LANES] = out.astype(BF16)


def _strict_upper(tk):
    idx = jnp.arange(tk)
    return (idx[:, None] > idx[None, :]).astype(BF16)


def _attn_prompt(q, kb, vb, sb_bias, batch, t_len, tq=256, tk=256):
    assert tq & (tq - 1) == 0 and tq <= tk, "one masked key block must cover the diagonal"
    n = q.shape[0]
    nq = t_len // tq
    ngrp = SB_HEADS // HEADS_PER_STEP
    return pl.pallas_call(
        functools.partial(_attn_prompt_kernel, tq=tq, tk=tk),
        out_shape=jax.ShapeDtypeStruct((n, SB_WIDTH), BF16),
        grid=(batch, nq),
        in_specs=[pl.BlockSpec(memory_space=pltpu.SMEM),
                  pl.BlockSpec((tq, SB_WIDTH), lambda b, i: (b * nq + i, 0)),
                  pl.BlockSpec((t_len, SB_WIDTH), lambda b, i: (b, 0)),
                  pl.BlockSpec((t_len, SB_WIDTH), lambda b, i: (b, 0)),
                  pl.BlockSpec((tk, tk), lambda b, i: (0, 0))],
        out_specs=pl.BlockSpec((tq, SB_WIDTH), lambda b, i: (b * nq + i, 0)),
        scratch_shapes=[pltpu.VMEM((ngrp, HEADS_PER_STEP * tq, HEAD_LANES), F32),
                        pltpu.VMEM((ngrp, HEADS_PER_STEP * tq, 1), F32)],
        compiler_params=_cparams(("parallel", "arbitrary")),
        name="attn_prompt",
    )(sb_bias.astype(F32) * LOG2E, q, kb, vb, _strict_upper(tk))


def _attn_sample_kernel(pt_ref, q_ref, bias_ref, tri_ref, later_ref, *refs, n_pages):
    del pt_ref
    k_refs = refs[:n_pages]
    v_refs = refs[n_pages:2 * n_pages]
    o_ref = refs[2 * n_pages]
    hd = (SB_HEADS, SB_HEAD_DIM, PAGE_SIZE)
    q = q_ref[0].astype(F32)
    q_col = jnp.broadcast_to(q, (PAGE_SIZE, SB_WIDTH)).T
    q3 = q_col.reshape(hd)
    z = jnp.concatenate([jnp.sum(k_refs[p][0].reshape(hd) * q3, axis=1) for p in range(n_pages)], axis=0)
    lsig, stay = _log_sigmoid_pair(z + bias_ref[...])
    after = _dot(stay.astype(BF16), tri_ref[...])
    total = jnp.broadcast_to(after[:, 0:1] + stay[:, 0:1], stay.shape)
    t_hi, t_lo = _split_bf16(total)
    later = later_ref[...]
    run = _dot(later, t_hi) + _dot(later, t_lo)
    w = jnp.exp2(lsig + after + run)
    acc = [jnp.zeros((SB_HEAD_DIM, PAGE_SIZE), F32) for _ in range(SB_HEADS)]
    for p in range(n_pages):
        for h in range(SB_HEADS):
            r = p * SB_HEADS + h
            acc[h] = acc[h] + v_refs[p][0, h * SB_HEAD_DIM:(h + 1) * SB_HEAD_DIM, :] * w[r:r + 1, :]
    a_hi, a_lo = _split_bf16(jnp.concatenate(acc, axis=0))
    ones = jnp.ones((8, PAGE_SIZE), BF16)
    o_ref[0] = (_dot_t(ones, a_hi) + _dot_t(ones, a_lo))[0:1].astype(BF16)


def _attn_sample(q, cache_k, cache_v, page_table, sb_bias):
    n, n_pages = page_table.shape
    n_pool = cache_k.shape[0]
    ck = cache_k.transpose(0, 2, 3, 1).reshape(n_pool, SB_WIDTH, PAGE_SIZE)
    cv = cache_v.transpose(0, 2, 3, 1).reshape(n_pool, SB_WIDTH, PAGE_SIZE)
    page_spec = lambda p: pl.BlockSpec((1, SB_WIDTH, PAGE_SIZE), lambda i, pt, p=p: (pt[i * n_pages + p], 0, 0))
    tok_spec = pl.BlockSpec((1, 1, SB_WIDTH), lambda i, pt: (i, 0, 0))
    rows = n_pages * SB_HEADS
    const = lambda shape: pl.BlockSpec(shape, lambda i, pt: (0, 0))
    r = jnp.arange(rows)
    later = ((r[:, None] % SB_HEADS == r[None, :] % SB_HEADS)
             & (r[None, :] // SB_HEADS > r[:, None] // SB_HEADS)).astype(BF16)
    out = pl.pallas_call(
        functools.partial(_attn_sample_kernel, n_pages=n_pages),
        out_shape=jax.ShapeDtypeStruct((n, 1, SB_WIDTH), BF16),
        grid_spec=pltpu.PrefetchScalarGridSpec(
            num_scalar_prefetch=1,
            grid=(n,),
            in_specs=[tok_spec, const((rows, 1)), const((PAGE_SIZE, PAGE_SIZE)), const((rows, rows))]
                     + [page_spec(p) for p in range(n_pages)] * 2,
            out_specs=tok_spec),
        compiler_params=_cparams(("parallel",)),
        name="attn_sample",
    )(page_table.reshape(-1).astype(jnp.int32), q.reshape(n, 1, SB_WIDTH),
      jnp.tile(sb_bias.astype(F32) * LOG2E, n_pages).reshape(rows, 1), _strict_upper(PAGE_SIZE), later,
      *([ck] * n_pages), *([cv] * n_pages))
    return out.reshape(n, SB_WIDTH)


def _first_index_of_max(vals, row_id, n_rows):
    m = jnp.max(vals, axis=0, keepdims=True)
    first = jnp.min(jnp.where(vals == m, row_id, n_rows), axis=0, keepdims=True)
    return m, first


def _route(logits_t, rbias):
    tm = logits_t.shape[1]
    scores = jax.nn.sigmoid(logits_t)
    biased = scores + rbias
    row8 = lax.broadcasted_iota(jnp.int32, (GROUP_SIZE, tm), 0)
    neg = jnp.float32(-jnp.inf)
    grp_score = []
    for g in range(N_GROUPS):
        blk = biased[g * GROUP_SIZE:(g + 1) * GROUP_SIZE]
        m1, first = _first_index_of_max(blk, row8, GROUP_SIZE)
        m2 = jnp.max(jnp.where(row8 == first, neg, blk), axis=0, keepdims=True)
        grp_score.append(m1 + m2)
    masked = []
    for g in range(N_GROUPS):
        beaten_by = jnp.zeros((1, tm), jnp.int32)
        for o in range(N_GROUPS):
            if o == g:
                continue
            wins = (grp_score[o] > grp_score[g]) if o > g else (grp_score[o] >= grp_score[g])
            beaten_by = beaten_by + wins.astype(jnp.int32)
        keep = beaten_by < TOPK_GROUPS
        masked.append(jnp.where(keep, biased[g * GROUP_SIZE:(g + 1) * GROUP_SIZE], neg))
    cur = jnp.concatenate(masked, axis=0)
    row = lax.broadcasted_iota(jnp.int32, (N_EXPERTS, tm), 0)
    chosen = jnp.zeros((N_EXPERTS, tm), jnp.bool_)
    for _ in range(TOP_K):
        _, first = _first_index_of_max(cur, row, N_EXPERTS)
        pick = row == first
        chosen = jnp.logical_or(chosen, pick)
        cur = jnp.where(pick, neg, cur)
    sel = jnp.where(chosen, scores, 0.0)
    return sel / jnp.sum(sel, axis=0, keepdims=True) * ROUTED_SCALE


def _post_kernel(x_ref, g_ref, o_ref, sga_ref, sgb_ref, gate1_ref, shift2_ref, scale2_ref, g2_ref,
                 wga_ref, wgb_ref, wsb_ref, wout_ref, wr_ref, rb_ref,
                 x1_ref, h2_ref, gates_ref):
    g = g_ref[...]
    branch_a = _dot(g, wga_ref[...]) * jax.nn.sigmoid(_dot(g, wgb_ref[...]))
    branch_b = _dot(o_ref[...], wsb_ref[...])
    merged = sga_ref[...].astype(F32) * branch_a + sgb_ref[...].astype(F32) * branch_b
    x1 = x_ref[...] + gate1_ref[...] * _dot(merged.astype(BF16), wout_ref[...])
    x1_ref[...] = x1
    r = lax.rsqrt(jnp.mean(x1 * x1, axis=-1, keepdims=True) + RMS_EPS)
    h2 = (x1 * r) * g2_ref[...]
    h2 = h2 * (1.0 + scale2_ref[...]) + shift2_ref[...]
    h2_ref[...] = h2.astype(BF16)
    h_hi, h_lo = _split_bf16(h2)
    w_hi, w_lo = _split_bf16(wr_ref[...])
    logits_t = _dot_t(w_hi, h_hi) + _dot_t(w_hi, h_lo) + _dot_t(w_lo, h_hi)
    gates_ref[...] = _route(logits_t, rb_ref[...])


def _post_mixer(x, g, o, sga, sgb, mod, per_token, tm, p):
    n = x.shape[0]
    tiles_per_mod = (n // mod.shape[0]) // tm if not per_token else 1
    row = lambda i: (i, 0)
    const = lambda i: (0, 0)
    blk = lambda w: pl.BlockSpec((tm, w), row)
    full = lambda a: pl.BlockSpec(a.shape, const)
    ms = lambda c: _mod_spec(per_token, tm, tiles_per_mod, c)
    weights = (p["w_glu_a"], p["w_glu_b"], p["w_sb_out"], p["w_out"], p["w_router_t"], p["router_bias"])
    return pl.pallas_call(
        _post_kernel,
        out_shape=(jax.ShapeDtypeStruct((n, D_MODEL), F32),
                   jax.ShapeDtypeStruct((n, D_MODEL), BF16),
                   jax.ShapeDtypeStruct((N_EXPERTS, n), F32)),
        grid=(n // tm,),
        in_specs=[blk(D_MODEL), blk(512), blk(512), blk(1024), blk(1024), ms(2), ms(3), ms(4),
                  pl.BlockSpec((1, D_MODEL), const)] + [full(w) for w in weights],
        out_specs=(blk(D_MODEL), blk(D_MODEL), pl.BlockSpec((N_EXPERTS, tm), lambda i: (0, i))),
        compiler_params=_cparams(("parallel",)),
        name="post_mixer",
    )(x, g, o, sga, sgb, mod, mod, mod, p["norm2_g"], *weights)


def _moe_kernel(h_ref, gates_ref, x1_ref, gate2_ref, wgs_ref, wus_ref, wds_ref,
                wg_ref, wu_ref, wd_ref, y_ref, acc_ref):
    j = pl.program_id(1)
    h = h_ref[...]

    @pl.when(j == 0)
    def _():
        act = _silu(_dot(h, wgs_ref[...])) * _dot(h, wus_ref[...])
        acc_ref[...] = _dot(act.astype(BF16), wds_ref[...])

    gates = gates_ref[...]
    lane = lax.broadcasted_iota(jnp.int32, gates.shape, 1)
    scaled = []
    for s in range(EXPERTS_PER_STEP):
        act = _silu(_dot(h, wg_ref[s].astype(BF16))) * _dot(h, wu_ref[s].astype(BF16))
        gcol = jnp.sum(jnp.where(lane == j * EXPERTS_PER_STEP + s, gates, 0.0), axis=1, keepdims=True)
        scaled.append((act * gcol).astype(BF16))
    acc_ref[...] += _dot(jnp.concatenate(scaled, axis=1), wd_ref[...].astype(BF16))

    @pl.when(j == pl.num_programs(1) - 1)
    def _():
        y_ref[...] = x1_ref[...] + gate2_ref[...] * acc_ref[...]


def _moe(h2, gates, x1, mod, per_token, tm, p):
    n = h2.shape[0]
    tiles_per_mod = (n // mod.shape[0]) // tm if not per_token else 1
    row = lambda i, e: (i, 0)
    const = lambda i, e: (0, 0)
    exp = lambda i, e: (e, 0, 0)
    if per_token:
        gate2_spec = pl.BlockSpec((tm, D_MODEL), lambda i, e: (i, 5))
    else:
        gate2_spec = pl.BlockSpec((None, 1, D_MODEL), lambda i, e: (i // tiles_per_mod, 0, 5))
    wide = EXPERTS_PER_STEP * EXPERT_HIDDEN
    return pl.pallas_call(
        _moe_kernel,
        out_shape=jax.ShapeDtypeStruct((n, D_MODEL), F32),
        grid=(n // tm, N_EXPERTS // EXPERTS_PER_STEP),
        in_specs=[pl.BlockSpec((tm, D_MODEL), row),
                  pl.BlockSpec((tm, N_EXPERTS), row),
                  pl.BlockSpec((tm, D_MODEL), row),
                  gate2_spec,
                  pl.BlockSpec((D_MODEL, EXPERT_HIDDEN), const),
                  pl.BlockSpec((D_MODEL, EXPERT_HIDDEN), const),
                  pl.BlockSpec((EXPERT_HIDDEN, D_MODEL), const),
                  pl.BlockSpec((EXPERTS_PER_STEP, D_MODEL, EXPERT_HIDDEN), exp),
                  pl.BlockSpec((EXPERTS_PER_STEP, D_MODEL, EXPERT_HIDDEN), exp),
                  pl.BlockSpec((None, wide, D_MODEL), exp)],
        out_specs=pl.BlockSpec((tm, D_MODEL), row),
        scratch_shapes=[pltpu.VMEM((tm, D_MODEL), F32)],
        compiler_params=_cparams(("parallel", "arbitrary")),
        name="moe",
    )(h2, gates, x1, mod, p["w_gate_s"], p["w_up_s"], p["w_down_s"], p["w_gate_e"], p["w_up_e"], p["w_down_e"])


def _dispatch_kernel(gates_ref, h_ref, tri_ref, xs_ref, flag_ref):
    g = gates_ref[...]
    sel = g > 0.0
    rank = _dot(sel.astype(F32).astype(BF16), tri_ref[...])
    rank = jnp.where(sel, rank, -1.0)
    over = jnp.where(rank >= ROUTE_CAP, 1.0, 0.0)
    flag_ref[...] = jnp.broadcast_to(jnp.max(over, axis=1, keepdims=True), flag_ref.shape)
    slot = lax.broadcasted_iota(jnp.int32, (ROUTE_CAP, g.shape[1]), 0).astype(F32)
    h = h_ref[...]
    per_dot = 8
    for e0 in range(0, N_EXPERTS, per_dot):
        place = jnp.concatenate([(slot == rank[e:e + 1, :]).astype(F32).astype(BF16)
                                 for e in range(e0, e0 + per_dot)], axis=0)
        xs_ref[e0 * ROUTE_CAP:(e0 + per_dot) * ROUTE_CAP, :] = _dot(place, h).astype(BF16)


def _expert_kernel(x_ref, wg_ref, wu_ref, wd_ref, y_ref):
    nb, cap, d = x_ref.shape
    x = x_ref[...].reshape(nb * cap, d)
    n_part = 2 if nb % 2 == 0 else 1
    rows = nb * cap // n_part
    wg, wu, wd = wg_ref[...].astype(BF16), wu_ref[...].astype(BF16), wd_ref[...].astype(BF16)
    for r in range(n_part):
        xr = x[r * rows:(r + 1) * rows]
        act = _silu(_dot(xr, wg)) * _dot(xr, wu)
        y = _dot(act.astype(BF16), wd).astype(BF16)
        y_ref[r * (nb // n_part):(r + 1) * (nb // n_part)] = y.reshape(nb // n_part, cap, d)


def _combine_kernel(gates_ref, ys_ref, h_ref, x1_ref, gate2_ref, wgs_ref, wus_ref, wds_ref,
                    tri_ref, y_ref):
    g = gates_ref[...]
    sel = g > 0.0
    rank = _dot(sel.astype(F32).astype(BF16), tri_ref[...])
    rank = jnp.where(sel, rank, -1.0)
    slot = lax.broadcasted_iota(jnp.int32, (ROUTE_CAP, g.shape[1]), 0).astype(F32)
    weigh = jnp.concatenate([jnp.where(slot == rank[e:e + 1, :], g[e:e + 1, :], 0.0).astype(BF16)
                             for e in range(N_EXPERTS)], axis=0)
    routed = lax.dot_general(weigh, ys_ref[...], (((0,), (0,)), ((), ())), preferred_element_type=F32)
    h = h_ref[...]
    act = _silu(_dot(h, wgs_ref[...])) * _dot(h, wus_ref[...])
    shared = _dot(act.astype(BF16), wds_ref[...])
    y_ref[...] = x1_ref[...] + gate2_ref[...] * (shared + routed)


def _moe_routed(h2, gates_t, x1, mod, p):
    n = h2.shape[0]
    bt = ROUTE_BLOCK
    nb = n // bt
    slots = N_EXPERTS * ROUTE_CAP
    tiles_per_mod = (n // mod.shape[0]) // bt
    later = _strict_upper(bt)
    const2 = lambda b: (0, 0)
    xs, flags = pl.pallas_call(
        _dispatch_kernel,
        out_shape=(jax.ShapeDtypeStruct((nb, slots, D_MODEL), BF16),
                   jax.ShapeDtypeStruct((nb, N_EXPERTS, 128), F32)),
        grid=(nb,),
        in_specs=[pl.BlockSpec((N_EXPERTS, bt), lambda b: (0, b)),
                  pl.BlockSpec((bt, D_MODEL), lambda b: (b, 0)),
                  pl.BlockSpec((bt, bt), const2)],
        out_specs=(pl.BlockSpec((None, slots, D_MODEL), lambda b: (b, 0, 0)),
                   pl.BlockSpec((None, N_EXPERTS, 128), lambda b: (b, 0, 0))),
        compiler_params=_cparams(("parallel",)),
        name="moe_dispatch",
    )(gates_t, h2, later.T)
    gates = gates_t.T

    n_half = 2
    xblk = pl.BlockSpec((nb // n_half, None, ROUTE_CAP, D_MODEL), lambda e, f: (f, e, 0, 0))
    ys = pl.pallas_call(
        _expert_kernel,
        out_shape=jax.ShapeDtypeStruct((nb, N_EXPERTS, ROUTE_CAP, D_MODEL), BF16),
        grid=(N_EXPERTS, n_half),
        in_specs=[xblk,
                  pl.BlockSpec((None, D_MODEL, EXPERT_HIDDEN), lambda e, f: (e, 0, 0)),
                  pl.BlockSpec((None, D_MODEL, EXPERT_HIDDEN), lambda e, f: (e, 0, 0)),
                  pl.BlockSpec((None, EXPERT_HIDDEN, D_MODEL), lambda e, f: (e, 0, 0))],
        out_specs=xblk,
        compiler_params=_cparams(("parallel", "arbitrary")),
        name="moe_experts",
    )(xs.reshape(nb, N_EXPERTS, ROUTE_CAP, D_MODEL), p["w_gate_e"], p["w_up_e"],
      p["w_down_e"].reshape(N_EXPERTS, EXPERT_HIDDEN, D_MODEL))
    blk = lambda w: pl.BlockSpec((bt, w), lambda b: (b, 0))
    full = lambda a: pl.BlockSpec(a.shape, const2)
    y = pl.pallas_call(
        _combine_kernel,
        out_shape=jax.ShapeDtypeStruct((n, D_MODEL), F32),
        grid=(nb,),
        in_specs=[pl.BlockSpec((N_EXPERTS, bt), lambda b: (0, b)),
                  pl.BlockSpec((None, slots, D_MODEL), lambda b: (b, 0, 0)),
                  blk(D_MODEL), blk(D_MODEL), _mod_spec(False, bt, tiles_per_mod, 5),
                  full(p["w_gate_s"]), full(p["w_up_s"]), full(p["w_down_s"]), pl.BlockSpec((bt, bt), const2)],
        out_specs=blk(D_MODEL),
        compiler_params=_cparams(("parallel",)),
        name="moe_combine",
    )(gates_t, ys.reshape(nb, slots, D_MODEL), h2, x1, mod, p["w_gate_s"], p["w_up_s"], p["w_down_s"],
      later.T)

    over = (flags[:, :, 0] > 0.0).reshape(-1)
    n_over = jnp.sum(over.astype(jnp.int32))
    pairs = jnp.argsort(jnp.logical_not(over), stable=True)[:MAX_FIX_PAIRS].astype(jnp.int32)
    case = (n_over > 0).astype(jnp.int32) + (n_over > MAX_FIX_PAIRS).astype(jnp.int32)
    return lax.switch(case, [lambda: y,
                             lambda: _moe_fix(y, pairs // N_EXPERTS, pairs % N_EXPERTS, n_over.reshape(1),
                                              h2, gates, mod, later, p),
                             lambda: _moe(h2, gates, x1, mod, False, 512, p)])


def _moe_fix_kernel(block_ref, expert_ref, n_ref, h_ref, gates_ref, gate2_ref, wg_ref, wu_ref, wd_ref,
                    tri_ref, y_in_ref, y_ref):
    i = pl.program_id(0)
    active = i < n_ref[0]
    new_block = jnp.logical_or(i == 0, block_ref[i] != block_ref[jnp.maximum(i - 1, 0)])

    @pl.when(jnp.logical_and(active, new_block))
    def _():
        y_ref[...] = y_in_ref[...]

    @pl.when(active)
    def _():
        g = gates_ref[...]
        sel = g > 0.0
        rank = _dot(tri_ref[...], sel.astype(F32).astype(BF16))
        lane = lax.broadcasted_iota(jnp.int32, g.shape, 1)
        missed = jnp.logical_and(jnp.logical_and(sel, rank >= ROUTE_CAP), lane == expert_ref[i])
        gcol = jnp.sum(jnp.where(missed, g, 0.0), axis=1, keepdims=True)
        h = h_ref[...]
        act = _silu(_dot(h, wg_ref[...].astype(BF16))) * _dot(h, wu_ref[...].astype(BF16))
        y_ref[...] += gate2_ref[...] * _dot((act * gcol).astype(BF16), wd_ref[...].astype(BF16))


def _moe_fix(y, pair_block, pair_expert, n_pairs, h2, gates, mod, later, p):
    n = h2.shape[0]
    bt = ROUTE_BLOCK
    tiles_per_mod = (n // mod.shape[0]) // bt
    last = lambda i, nf: jnp.minimum(i, jnp.maximum(nf[0] - 1, 0))
    row = lambda i, pb, pe, nf: (pb[last(i, nf)], 0)
    exp = lambda i, pb, pe, nf: (pe[last(i, nf)], 0, 0)
    return pl.pallas_call(
        _moe_fix_kernel,
        out_shape=jax.ShapeDtypeStruct((n, D_MODEL), F32),
        grid_spec=pltpu.PrefetchScalarGridSpec(
            num_scalar_prefetch=3,
            grid=(MAX_FIX_PAIRS,),
            in_specs=[pl.BlockSpec((bt, D_MODEL), row),
                      pl.BlockSpec((bt, N_EXPERTS), row),
                      pl.BlockSpec((None, 1, D_MODEL), lambda i, pb, pe, nf: (pb[last(i, nf)] // tiles_per_mod, 0, 5)),
                      pl.BlockSpec((None, D_MODEL, EXPERT_HIDDEN), exp),
                      pl.BlockSpec((None, D_MODEL, EXPERT_HIDDEN), exp),
                      pl.BlockSpec((None, EXPERT_HIDDEN, D_MODEL), exp),
                      pl.BlockSpec((bt, bt), lambda i, pb, pe, nf: (0, 0)),
                      pl.BlockSpec((bt, D_MODEL), row)],
            out_specs=pl.BlockSpec((bt, D_MODEL), row)),
        input_output_aliases={10: 0},
        compiler_params=_cparams(("arbitrary",)),
        name="moe_fix",
    )(pair_block, pair_expert, n_pairs, h2, gates, mod, p["w_gate_e"], p["w_up_e"],
      p["w_down_e"].reshape(N_EXPERTS, EXPERT_HIDDEN, D_MODEL), later, y)


def _layer(x_prompt, x_sample, cache_k, cache_v, s0_re, s0_im, page_table, c_prompt, c_sample, p):
    batch, t_len, d = x_prompt.shape
    n_s = x_sample.shape[0]
    n_p = batch * t_len

    c_all = jnp.concatenate([c_prompt, c_sample], axis=0)
    pad = (-c_all.shape[0]) % 8
    c_all = jnp.pad(c_all, ((0, pad), (0, 0)))
    mod = _ada_mod(c_all, p["w_ada"], p["b_ada"])
    mod_p = mod[:batch].reshape(batch, 1, 6 * d)
    mod_s = mod[batch:batch + n_s]

    a_re, a_im, bb_re, bb_im = _ssm_prep(p["lam_re"], p["lam_im"], p["log_dt"], p["b_re"], p["b_im"])
    ssm = (a_re, a_im, bb_re, bb_im, _blockdiag_out(p["c_re"]), _blockdiag_out(-p["c_im"]),
           p["d_skip"].astype(F32).reshape(1, SSM_WIDTH))

    xp = x_prompt.reshape(n_p, d)
    xs = x_sample.reshape(n_s, d)
    proj_p = _in_proj(xp, mod_p, False, 512, t_len, p["norm1_g"], p["w_in"], p["q_norm_g"], p["k_norm_g"])
    proj_s = _in_proj(xs, mod_s, True, n_s, n_s, p["norm1_g"], p["w_in"], p["q_norm_g"], p["k_norm_g"])
    u_p, q_p, k_p, kb_p, v_p, vb_p, sga_p, sgb_p = proj_p
    u_s, q_s, k_s, _, v_s, _, sga_s, sgb_s = proj_s

    g_p, st_re_p, st_im_p = _ssm_prompt(u_p.reshape(batch, t_len, SSM_WIDTH), ssm)
    g_p = g_p.reshape(n_p, SSM_WIDTH)
    g_s, st_re_s, st_im_s = _ssm_sample(u_s, s0_re.reshape(n_s, SSM_FLAT), s0_im.reshape(n_s, SSM_FLAT), ssm)

    o_p = _attn_prompt(q_p, kb_p, vb_p, p["sb_bias"], batch, t_len)
    o_s = _attn_sample(q_s, cache_k, cache_v, page_table, p["sb_bias"])

    x1_p, h2_p, gates_p = _post_mixer(xp, g_p, o_p, sga_p, sgb_p, mod_p, False, 512, p)
    x1_s, h2_s, gates_s = _post_mixer(xs, g_s, o_s, sga_s, sgb_s, mod_s, True, n_s, p)

    y_p = _moe_routed(h2_p, gates_p, x1_p, mod_p, p)
    y_s = _moe(h2_s, gates_s.T, x1_s, mod_s, True, n_s, p)

    hd = (SB_HEADS, SB_HEAD_DIM)
    gs = (SSM_GROUPS, SSM_STATE)

    def kv_out(a_t, n_seq, seq_len):
        return a_t.reshape(n_seq, *hd, seq_len).transpose(0, 3, 1, 2)

    return (y_p.reshape(batch, t_len, d), y_s.reshape(n_s, 1, d),
            kv_out(k_p, batch, t_len), kv_out(v_p, batch, t_len),
            st_re_p.reshape(batch, *gs), st_im_p.reshape(batch, *gs),
            kv_out(k_s, 1, n_s).reshape(n_s, 1, *hd), kv_out(v_s, 1, n_s).reshape(n_s, 1, *hd),
            st_re_s.reshape(n_s, *gs), st_im_s.reshape(n_s, *gs))


def kernel(x_prompt, x_sample, cache_k, cache_v, state_ssm_re, state_ssm_im, page_table, c_prompt, c_sample,
           w_ada, b_ada, norm1_g, norm2_g, w_in, q_norm_g, k_norm_g, sb_bias, ssm_lambda_re, ssm_lambda_im,
           ssm_log_dt, ssm_b_re, ssm_b_im, ssm_c_re, ssm_c_im, ssm_d, w_glu_a, w_glu_b, w_sb_out, w_out,
           w_router, router_bias, w_gate_e, w_up_e, w_down_e, w_gate_s, w_up_s, w_down_s):
    depth = w_in.shape[0]
    assert depth == 1, "single-layer step"
    l = 0
    p = dict(
        w_ada=w_ada[l], b_ada=b_ada[l],
        norm1_g=norm1_g[l], norm2_g=norm2_g[l].reshape(1, D_MODEL),
        w_in=w_in[l].astype(BF16), q_norm_g=q_norm_g[l], k_norm_g=k_norm_g[l], sb_bias=sb_bias[l],
        lam_re=ssm_lambda_re[l], lam_im=ssm_lambda_im[l], log_dt=ssm_log_dt[l],
        b_re=ssm_b_re[l], b_im=ssm_b_im[l], c_re=ssm_c_re[l], c_im=ssm_c_im[l], d_skip=ssm_d[l],
        w_glu_a=w_glu_a[l].astype(BF16), w_glu_b=w_glu_b[l].astype(BF16),
        w_sb_out=w_sb_out[l].astype(BF16), w_out=w_out[l].astype(BF16),
        w_router_t=w_router[l].T, router_bias=router_bias[l].reshape(N_EXPERTS, 1),
        w_gate_e=w_gate_e[l], w_up_e=w_up_e[l],
        w_down_e=w_down_e[l].reshape(N_EXPERTS // EXPERTS_PER_STEP, EXPERTS_PER_STEP * EXPERT_HIDDEN, D_MODEL),
        w_gate_s=w_gate_s[l].astype(BF16), w_up_s=w_up_s[l].astype(BF16), w_down_s=w_down_s[l].astype(BF16),
    )
    outs = _layer(x_prompt, x_sample, cache_k[l], cache_v[l], state_ssm_re[l], state_ssm_im[l],
                  page_table, c_prompt, c_sample, p)
    y_p, y_s = outs[0], outs[1]
    return (y_p, y_s) + tuple(o[None] for o in outs[2:])
```

```python
import functools
import math

import jax
import jax.numpy as jnp
from jax import lax
from jax.experimental import pallas as pl
from jax.experimental.pallas import tpu as pltpu

F32 = jnp.float32
BF16 = jnp.bfloat16

D_MODEL = 1024
SSM_WIDTH = 512
SSM_GROUP = 16
SSM_GROUPS = 32
SSM_STATE = 64
SSM_FLAT = SSM_GROUPS * SSM_STATE
EIG_CLIP = -1e-4
SB_HEADS = 8
SB_HEAD_DIM = 64
SB_WIDTH = 512
PAGE_SIZE = 128
N_EXPERTS = 64
TOP_K = 8
N_GROUPS = 8
TOPK_GROUPS = 4
GROUP_SIZE = N_EXPERTS // N_GROUPS
EXPERT_HIDDEN = 256
EXPERTS_PER_STEP = 4
ROUTE_BLOCK = 256
ROUTE_CAP = 64
MAX_FIX_PAIRS = 16
ROUTED_SCALE = 2.5
RMS_EPS = 1e-6

LOG2E = math.log2(math.e)
LOGIT_SCALE_LOG2 = SB_HEAD_DIM ** -0.5 * LOG2E
HEADS_PER_STEP = 4
HEAD_LANES = HEADS_PER_STEP * SB_HEAD_DIM
VMEM_LIMIT = 48 * 1024 * 1024


def _cparams(sem):
    return pltpu.CompilerParams(dimension_semantics=sem, vmem_limit_bytes=VMEM_LIMIT)


def _dot(a, b):
    return jnp.dot(a, b, preferred_element_type=F32)


def _dot_t(a, b):
    return lax.dot_general(a, b, (((1,), (1,)), ((), ())), preferred_element_type=F32)


def _split_bf16(x):
    hi = x.astype(BF16)
    lo = (x - hi.astype(F32)).astype(BF16)
    return hi, lo


def _silu(x):
    return x * jax.nn.sigmoid(x)


def _gelu_tanh(x):
    c = math.sqrt(2.0 / math.pi)
    return 0.5 * x * (1.0 + jnp.tanh(c * (x + 0.044715 * (x * x * x))))


def _ada_kernel(c_ref, w_ref, b_ref, o_ref):
    s = _silu(c_ref[...])
    s_hi, s_lo = _split_bf16(s)
    w_hi, w_lo = _split_bf16(w_ref[...])
    o_ref[...] = _dot(s_hi, w_hi) + _dot(s_hi, w_lo) + _dot(s_lo, w_hi) + b_ref[...]


def _ada_mod(c, w_ada, b_ada):
    rows, d = c.shape
    cols = w_ada.shape[1]
    tn = 1024
    return pl.pallas_call(
        _ada_kernel,
        out_shape=jax.ShapeDtypeStruct((rows, cols), F32),
        grid=(cols // tn,),
        in_specs=[pl.BlockSpec((rows, d), lambda j: (0, 0)),
                  pl.BlockSpec((d, tn), lambda j: (0, j)),
                  pl.BlockSpec((1, tn), lambda j: (0, j))],
        out_specs=pl.BlockSpec((rows, tn), lambda j: (0, j)),
        compiler_params=_cparams(("parallel",)),
        name="ada_mod",
    )(c, w_ada, b_ada.reshape(1, cols))


def _ssm_prep_kernel(lre_ref, lim_ref, ldt_ref, bre_ref, bim_ref,
                     are_ref, aim_ref, bbre_ref, bbim_ref):
    dt = jnp.exp(ldt_ref[...])
    lre = jnp.minimum(lre_ref[...], EIG_CLIP)
    lim = lim_ref[...]
    mag = jnp.exp(lre * dt)
    a_re = mag * jnp.cos(lim * dt)
    a_im = mag * jnp.sin(lim * dt)
    den = lre * lre + lim * lim
    f_re = ((a_re - 1.0) * lre + a_im * lim) / den
    f_im = (a_im * lre - (a_re - 1.0) * lim) / den
    br = bre_ref[...]
    bi = bim_ref[...]
    are_ref[...] = a_re
    aim_ref[...] = a_im
    bbre_ref[...] = f_re * br - f_im * bi
    bbim_ref[...] = f_re * bi + f_im * br


def _ssm_prep(lam_re, lam_im, log_dt, b_re, b_im):
    rep = lambda a: jnp.repeat(a.astype(F32), SSM_GROUP, axis=0)
    lre, lim = rep(lam_re), rep(lam_im)
    ldt = rep(jnp.broadcast_to(log_dt.astype(F32)[:, None], (SSM_GROUPS, SSM_STATE)))
    tr = lambda b: b.astype(F32).transpose(0, 2, 1).reshape(SSM_WIDTH, SSM_STATE)
    shp = jax.ShapeDtypeStruct((SSM_WIDTH, SSM_STATE), F32)
    a_re, a_im, bb_re, bb_im = pl.pallas_call(
        _ssm_prep_kernel, out_shape=(shp, shp, shp, shp), name="ssm_prep",
    )(lre, lim, ldt, tr(b_re), tr(b_im))
    eye = jnp.eye(SSM_GROUPS, dtype=F32)

    def blockdiag_in(bb):
        bb4 = bb.reshape(SSM_GROUPS, SSM_GROUP, 1, SSM_STATE) * eye.reshape(SSM_GROUPS, 1, SSM_GROUPS, 1)
        return bb4.reshape(SSM_WIDTH, SSM_FLAT).astype(BF16)

    a_re = a_re[::SSM_GROUP].reshape(1, SSM_FLAT)
    a_im = a_im[::SSM_GROUP].reshape(1, SSM_FLAT)
    return a_re, a_im, blockdiag_in(bb_re), blockdiag_in(bb_im)


def _blockdiag_out(c):
    eye = jnp.eye(SSM_GROUPS, dtype=F32)
    c4 = c.astype(F32).transpose(0, 2, 1).reshape(SSM_GROUPS, SSM_STATE, 1, SSM_GROUP)
    return (c4 * eye.reshape(SSM_GROUPS, 1, SSM_GROUPS, 1)).reshape(SSM_FLAT, SSM_WIDTH).astype(BF16)


def _inproj_kernel(x_ref, shift_ref, scale_ref, g1_ref, w_ref, qg_ref, kg_ref, hm_ref,
                   u_ref, q_ref, k_ref, kb_ref, v_ref, vb_ref, sga_ref, sgb_ref):
    x = x_ref[...]
    r = lax.rsqrt(jnp.mean(x * x, axis=-1, keepdims=True) + RMS_EPS)
    h = (x * r) * g1_ref[...]
    h = h * (1.0 + scale_ref[...]) + shift_ref[...]
    hb = h.astype(BF16)
    hm = hm_ref[...]

    def head_norm(t, g):
        ms = _dot((t * t).astype(BF16), hm)
        return (t * lax.rsqrt(ms + RMS_EPS)) * g

    u_ref[...] = _dot(hb, w_ref[:, 0:512])
    q = head_norm(_dot(hb, w_ref[:, 512:1024]), qg_ref[...])
    q_ref[...] = (q * LOGIT_SCALE_LOG2).astype(BF16)
    k = head_norm(_dot(hb, w_ref[:, 1024:1536]), kg_ref[...])
    k_ref[...] = k.T
    kb_ref[...] = k.astype(BF16)
    v = _dot(hb, w_ref[:, 1536:2048])
    v_ref[...] = v.T
    vb_ref[...] = v.astype(BF16)
    sga_ref[...] = jax.nn.sigmoid(_dot(hb, w_ref[:, 2048:3072])).astype(BF16)
    sgb_ref[...] = jax.nn.sigmoid(_dot(hb, w_ref[:, 3072:4096])).astype(BF16)


def _head_mean_matrix():
    idx = jnp.arange(SB_WIDTH) // SB_HEAD_DIM
    return ((idx[:, None] == idx[None, :]).astype(F32) / SB_HEAD_DIM).astype(BF16)


def _mod_spec(per_token, tm, rows_per_mod, chunk):
    if per_token:
        return pl.BlockSpec((tm, D_MODEL), lambda i: (i, chunk))
    return pl.BlockSpec((None, 1, D_MODEL), lambda i: (i // rows_per_mod, 0, chunk))


def _in_proj(x, mod, per_token, tm, seq_len, norm1_g, w_in_b, q_g, k_g):
    n = x.shape[0]
    tiles_per_mod = (n // mod.shape[0]) // tm if not per_token else 1
    tiles_per_seq = seq_len // tm
    row = lambda i: (i, 0)
    const = lambda i: (0, 0)
    f32o = lambda w: jax.ShapeDtypeStruct((n, w), F32)
    b16o = lambda w: jax.ShapeDtypeStruct((n, w), BF16)
    kvo = jax.ShapeDtypeStruct((n // seq_len, SB_WIDTH, seq_len), F32)
    kv_blk = pl.BlockSpec((None, SB_WIDTH, tm), lambda i: (i // tiles_per_seq, 0, i % tiles_per_seq))
    blk = lambda w: pl.BlockSpec((tm, w), row)
    return pl.pallas_call(
        _inproj_kernel,
        out_shape=(f32o(512), b16o(512), kvo, b16o(512), kvo, b16o(512), b16o(1024), b16o(1024)),
        grid=(n // tm,),
        in_specs=[blk(D_MODEL),
                  _mod_spec(per_token, tm, tiles_per_mod, 0),
                  _mod_spec(per_token, tm, tiles_per_mod, 1),
                  pl.BlockSpec((1, D_MODEL), const),
                  pl.BlockSpec(w_in_b.shape, const),
                  pl.BlockSpec((1, SB_WIDTH), const),
                  pl.BlockSpec((1, SB_WIDTH), const),
                  pl.BlockSpec((SB_WIDTH, SB_WIDTH), const)],
        out_specs=(blk(512), blk(512), kv_blk, blk(512), kv_blk, blk(512), blk(1024), blk(1024)),
        compiler_params=_cparams(("parallel",)),
        name="in_proj",
    )(x, mod, mod, norm1_g.reshape(1, D_MODEL), w_in_b,
      jnp.tile(q_g, SB_HEADS).reshape(1, SB_WIDTH), jnp.tile(k_g, SB_HEADS).reshape(1, SB_WIDTH),
      _head_mean_matrix())


def _ssm_drive(ub, bb_ref):
    half_c, half_s = SSM_WIDTH // 2, SSM_FLAT // 2
    lo = _dot(ub[:, :half_c], bb_ref[:half_c, :half_s])
    hi = _dot(ub[:, half_c:], bb_ref[half_c:, half_s:])
    return lo, hi


def _ssm_readout(xre, xim, ccre_ref, ccim_ref, d_ref, u):
    half_c, half_s = SSM_WIDTH // 2, SSM_FLAT // 2
    xr = xre.astype(BF16)
    xi = xim.astype(BF16)
    y_lo = _dot(xr[:, :half_s], ccre_ref[:half_s, :half_c]) + _dot(xi[:, :half_s], ccim_ref[:half_s, :half_c])
    y_hi = _dot(xr[:, half_s:], ccre_ref[half_s:, half_c:]) + _dot(xi[:, half_s:], ccim_ref[half_s:, half_c:])
    y = jnp.concatenate([y_lo, y_hi], axis=1) + d_ref[...] * u
    return _gelu_tanh(y)


def _ssm_prompt_kernel(u_ref, bbre_ref, bbim_ref, are_ref, aim_ref, ccre_ref, ccim_ref, d_ref,
                       g_ref, sre_ref, sim_ref, ubuf, gbuf, xre, xim, cre, cim, *, lane_chunk):
    batch, tt, _ = u_ref.shape
    rows = tt * batch
    half_s = SSM_FLAT // 2

    @pl.when(pl.program_id(0) == 0)
    def _():
        cre[...] = jnp.zeros_like(cre)
        cim[...] = jnp.zeros_like(cim)

    n_lane_tiles = SSM_WIDTH // 128
    for b in range(batch):
        for c in range(n_lane_tiles):
            ubuf[c, pl.ds(b, tt, stride=batch), :] = u_ref[b, :, c * 128:(c + 1) * 128]
    u = jnp.concatenate([ubuf[c] for c in range(n_lane_tiles)], axis=1)
    ub = u.astype(BF16)
    lo, hi = _ssm_drive(ub, bbre_ref)
    xre[:, :half_s] = lo
    xre[:, half_s:] = hi
    lo, hi = _ssm_drive(ub, bbim_ref)
    xim[:, :half_s] = lo
    xim[:, half_s:] = hi

    first_step = lax.broadcasted_iota(jnp.int32, (8, lane_chunk), 0) < batch
    for c0 in range(0, SSM_FLAT, lane_chunk):
        cols = pl.ds(c0, lane_chunk)
        ar = jnp.broadcast_to(are_ref[:, cols], (8, lane_chunk))
        ai = jnp.broadcast_to(aim_ref[:, cols], (8, lane_chunk))

        def step(i, carry, cols=cols, ar=ar, ai=ai):
            c_r, c_i = carry
            r0 = pl.multiple_of(i * 8, 8)
            br = xre[pl.ds(r0, 8), cols]
            bi = xim[pl.ds(r0, 8), cols]
            p_r = pltpu.roll(c_r, batch, 0)
            p_i = pltpu.roll(c_i, batch, 0)
            y_r = ar * p_r - ai * p_i + br
            y_i = ar * p_i + ai * p_r + bi
            q_r = pltpu.roll(y_r, batch, 0)
            q_i = pltpu.roll(y_i, batch, 0)
            z_r = ar * q_r - ai * q_i + br
            z_i = ar * q_i + ai * q_r + bi
            xre[pl.ds(r0, 8), cols] = jnp.where(first_step, y_r, z_r)
            xim[pl.ds(r0, 8), cols] = jnp.where(first_step, y_i, z_i)
            return z_r, z_i

        c_r, c_i = lax.fori_loop(0, rows // 8, step, (cre[:, cols], cim[:, cols]))
        cre[:, cols] = c_r
        cim[:, cols] = c_i

    g = _ssm_readout(xre[...], xim[...], ccre_ref, ccim_ref, d_ref, u)
    for c in range(n_lane_tiles):
        gbuf[c] = g[:, c * 128:(c + 1) * 128]
    for b in range(batch):
        g_ref[b] = jnp.concatenate([gbuf[c, pl.ds(b, tt, stride=batch), :] for c in range(n_lane_tiles)],
                                   axis=1).astype(BF16)
    sre_ref[...] = cre[...]
    sim_ref[...] = cim[...]


def _ssm_prompt(u, ssm, tt=128, lane_chunk=512):
    a_re, a_im, bb_re, bb_im, cc_re, cc_im, d_skip = ssm
    batch, t_len, _ = u.shape
    assert 2 * batch == 8, "one sublane tile must hold exactly two time steps"
    rows = tt * batch
    const = lambda j: (0, 0)
    full = lambda a: pl.BlockSpec(a.shape, const)
    seq_blk = pl.BlockSpec((batch, tt, SSM_WIDTH), lambda j: (0, j, 0))
    g, s_re, s_im = pl.pallas_call(
        functools.partial(_ssm_prompt_kernel, lane_chunk=lane_chunk),
        out_shape=(jax.ShapeDtypeStruct((batch, t_len, SSM_WIDTH), BF16),
                   jax.ShapeDtypeStruct((8, SSM_FLAT), F32),
                   jax.ShapeDtypeStruct((8, SSM_FLAT), F32)),
        grid=(t_len // tt,),
        in_specs=[seq_blk,
                  full(bb_re), full(bb_im), full(a_re), full(a_im), full(cc_re), full(cc_im), full(d_skip)],
        out_specs=(seq_blk, pl.BlockSpec((8, SSM_FLAT), const), pl.BlockSpec((8, SSM_FLAT), const)),
        scratch_shapes=[pltpu.VMEM((SSM_WIDTH // 128, rows, 128), F32),
                        pltpu.VMEM((SSM_WIDTH // 128, rows, 128), F32),
                        pltpu.VMEM((rows, SSM_FLAT), F32), pltpu.VMEM((rows, SSM_FLAT), F32),
                        pltpu.VMEM((8, SSM_FLAT), F32), pltpu.VMEM((8, SSM_FLAT), F32)],
        compiler_params=_cparams(("arbitrary",)),
        name="ssm_prompt",
    )(u, bb_re, bb_im, a_re, a_im, cc_re, cc_im, d_skip)
    return g, s_re[batch:], s_im[batch:]


def _ssm_sample_kernel(u_ref, s0re_ref, s0im_ref, bbre_ref, bbim_ref, are_ref, aim_ref,
                       ccre_ref, ccim_ref, d_ref, g_ref, sre_ref, sim_ref):
    u = u_ref[...]
    ub = u.astype(BF16)
    ar, ai = are_ref[...], aim_ref[...]
    s_r, s_i = s0re_ref[...], s0im_ref[...]
    lo, hi = _ssm_drive(ub, bbre_ref)
    x_r = ar * s_r - ai * s_i + jnp.concatenate([lo, hi], axis=1)
    lo, hi = _ssm_drive(ub, bbim_ref)
    x_i = ar * s_i + ai * s_r + jnp.concatenate([lo, hi], axis=1)
    sre_ref[...] = x_r
    sim_ref[...] = x_i
    g_ref[...] = _ssm_readout(x_r, x_i, ccre_ref, ccim_ref, d_ref, u).astype(BF16)


def _ssm_sample(u, s_re, s_im, ssm):
    a_re, a_im, bb_re, bb_im, cc_re, cc_im, d_skip = ssm
    n = u.shape[0]
    return pl.pallas_call(
        _ssm_sample_kernel,
        out_shape=(jax.ShapeDtypeStruct((n, SSM_WIDTH), BF16),
                   jax.ShapeDtypeStruct((n, SSM_FLAT), F32),
                   jax.ShapeDtypeStruct((n, SSM_FLAT), F32)),
        compiler_params=pltpu.CompilerParams(vmem_limit_bytes=VMEM_LIMIT),
        name="ssm_sample",
    )(u, s_re, s_im, bb_re, bb_im, a_re, a_im, cc_re, cc_im, d_skip)


def _log_sigmoid_pair(z2):
    sign = jnp.uint32(0x80000000)
    neg_abs = lax.bitcast_convert_type(lax.bitcast_convert_type(z2, jnp.uint32) | sign, F32)
    soft = jnp.log2(1.0 + jnp.exp2(neg_abs))
    lsig = jnp.minimum(z2, 0.0) - soft
    return lsig, lsig - z2


def _attn_prompt_kernel(bias_ref, q_ref, k_ref, v_ref, tri_ref, o_ref, acc_ref, run_ref, *, tq, tk):
    qi = pl.program_id(1)
    nh = HEADS_PER_STEP
    ngrp = SB_HEADS // nh
    lane_head = lax.broadcasted_iota(jnp.int32, (1, HEAD_LANES), 1) // SB_HEAD_DIM
    head_mask = [(lane_head == h).astype(F32).astype(BF16) for h in range(nh)]
    qs, bias = [], []
    for g in range(ngrp):
        q = q_ref[:, g * HEAD_LANES:(g + 1) * HEAD_LANES]
        qs.append(jnp.concatenate([q * head_mask[h] for h in range(nh)], axis=0))
        bias.append([bias_ref[g * nh + h] for h in range(nh)])
    tri = tri_ref[...]
    run_ref[...] = jnp.zeros_like(run_ref)

    def block(js, masked):
        if masked:
            k0 = pl.multiple_of(js[0] * tk, tk)
            kpos = k0 + lax.broadcasted_iota(jnp.int32, (nh * tq, tk), 1)
            qpos = qi * tq + (lax.broadcasted_iota(jnp.int32, (nh * tq, tk), 0) & (tq - 1))
            valid = kpos < qpos
        for g in range(ngrp):
            lanes = slice(g * HEAD_LANES, (g + 1) * HEAD_LANES)
            run = run_ref[g]
            pv = None
            for j in js:
                k0 = pl.multiple_of(j * tk, tk)
                zs = _dot_t(qs[g], k_ref[pl.ds(k0, tk), lanes])
                lsig_parts, stay_parts = [], []
                for h in range(nh):
                    lsig, stay = _log_sigmoid_pair(zs[h * tq:(h + 1) * tq] + bias[g][h])
                    lsig_parts.append(lsig)
                    stay_parts.append(stay)
                lsig = jnp.concatenate(lsig_parts, axis=0)
                stay = jnp.concatenate(stay_parts, axis=0)
                if masked:
                    stay = jnp.where(valid, stay, 0.0)
                after = _dot(stay.astype(BF16), tri)
                w = jnp.exp2(lsig + after + run)
                if masked:
                    w = jnp.where(valid, w, 0.0)
                part = _dot(w.astype(BF16), v_ref[pl.ds(k0, tk), lanes])
                pv = part if pv is None else pv + part
                run = run + (after[:, 0:1] + stay[:, 0:1])
            if masked:
                acc_ref[g] = pv
            else:
                acc_ref[g] += pv
            run_ref[g] = run

    j_diag = (qi * tq) // tk
    block((j_diag,), True)

    @pl.when(j_diag % 2 == 1)
    def _():
        block((j_diag - 1,), False)

    def body(jj, carry):
        j = 2 * (j_diag // 2 - jj) - 1
        block((j, j - 1), False)
        return carry

    lax.fori_loop(0, j_diag // 2, body, 0)
    for g in range(ngrp):
        acc = acc_ref[g]
        out = jnp.zeros((tq, HEAD_LANES), F32)
        for h in range(nh):
            out = jnp.where(lane_head == h, acc[h * tq:(h + 1) * tq], out)
        o_ref[:, g * HEAD_LANES:(g + 1) * HEAD_LANES] = out.astype(BF16)


def _strict_upper(tk):
    idx = jnp.arange(tk)
    return (idx[:, None] > idx[None, :]).astype(BF16)


def _attn_prompt(q, kb, vb, sb_bias, batch, t_len, tq=256, tk=256):
    assert tq & (tq - 1) == 0 and tq <= tk, "one masked key block must cover the diagonal"
    n = q.shape[0]
    nq = t_len // tq
    ngrp = SB_HEADS // HEADS_PER_STEP
    return pl.pallas_call(
        functools.partial(_attn_prompt_kernel, tq=tq, tk=tk),
        out_shape=jax.ShapeDtypeStruct((n, SB_WIDTH), BF16),
        grid=(batch, nq),
        in_specs=[pl.BlockSpec(memory_space=pltpu.SMEM),
                  pl.BlockSpec((tq, SB_WIDTH), lambda b, i: (b * nq + i, 0)),
                  pl.BlockSpec((t_len, SB_WIDTH), lambda b, i: (b, 0)),
                  pl.BlockSpec((t_len, SB_WIDTH), lambda b, i: (b, 0)),
                  pl.BlockSpec((tk, tk), lambda b, i: (0, 0))],
        out_specs=pl.BlockSpec((tq, SB_WIDTH), lambda b, i: (b * nq + i, 0)),
        scratch_shapes=[pltpu.VMEM((ngrp, HEADS_PER_STEP * tq, HEAD_LANES), F32),
                        pltpu.VMEM((ngrp, HEADS_PER_STEP * tq, 1), F32)],
        compiler_params=_cparams(("parallel", "arbitrary")),
        name="attn_prompt",
    )(sb_bias.astype(F32) * LOG2E, q, kb, vb, _strict_upper(tk))


def _attn_sample_kernel(pt_ref, q_ref, bias_ref, tri_ref, later_ref, *refs, n_pages):
    del pt_ref
    k_refs = refs[:n_pages]
    v_refs = refs[n_pages:2 * n_pages]
    o_ref = refs[2 * n_pages]
    hd = (SB_HEADS, SB_HEAD_DIM, PAGE_SIZE)
    q = q_ref[0].astype(F32)
    q_col = jnp.broadcast_to(q, (PAGE_SIZE, SB_WIDTH)).T
    q3 = q_col.reshape(hd)
    z = jnp.concatenate([jnp.sum(k_refs[p][0].reshape(hd) * q3, axis=1) for p in range(n_pages)], axis=0)
    lsig, stay = _log_sigmoid_pair(z + bias_ref[...])
    after = _dot(stay.astype(BF16), tri_ref[...])
    total = jnp.broadcast_to(after[:, 0:1] + stay[:, 0:1], stay.shape)
    t_hi, t_lo = _split_bf16(total)
    later = later_ref[...]
    run = _dot(later, t_hi) + _dot(later, t_lo)
    w = jnp.exp2(lsig + after + run)
    acc = [jnp.zeros((SB_HEAD_DIM, PAGE_SIZE), F32) for _ in range(SB_HEADS)]
    for p in range(n_pages):
        for h in range(SB_HEADS):
            r = p * SB_HEADS + h
            acc[h] = acc[h] + v_refs[p][0, h * SB_HEAD_DIM:(h + 1) * SB_HEAD_DIM, :] * w[r:r + 1, :]
    a_hi, a_lo = _split_bf16(jnp.concatenate(acc, axis=0))
    ones = jnp.ones((8, PAGE_SIZE), BF16)
    o_ref[0] = (_dot_t(ones, a_hi) + _dot_t(ones, a_lo))[0:1].astype(BF16)


def _attn_sample(q, cache_k, cache_v, page_table, sb_bias):
    n, n_pages = page_table.shape
    n_pool = cache_k.shape[0]
    ck = cache_k.transpose(0, 2, 3, 1).reshape(n_pool, SB_WIDTH, PAGE_SIZE)
    cv = cache_v.transpose(0, 2, 3, 1).reshape(n_pool, SB_WIDTH, PAGE_SIZE)
    page_spec = lambda p: pl.BlockSpec((1, SB_WIDTH, PAGE_SIZE), lambda i, pt, p=p: (pt[i * n_pages + p], 0, 0))
    tok_spec = pl.BlockSpec((1, 1, SB_WIDTH), lambda i, pt: (i, 0, 0))
    rows = n_pages * SB_HEADS
    const = lambda shape: pl.BlockSpec(shape, lambda i, pt: (0, 0))
    r = jnp.arange(rows)
    later = ((r[:, None] % SB_HEADS == r[None, :] % SB_HEADS)
             & (r[None, :] // SB_HEADS > r[:, None] // SB_HEADS)).astype(BF16)
    out = pl.pallas_call(
        functools.partial(_attn_sample_kernel, n_pages=n_pages),
        out_shape=jax.ShapeDtypeStruct((n, 1, SB_WIDTH), BF16),
        grid_spec=pltpu.PrefetchScalarGridSpec(
            num_scalar_prefetch=1,
            grid=(n,),
            in_specs=[tok_spec, const((rows, 1)), const((PAGE_SIZE, PAGE_SIZE)), const((rows, rows))]
                     + [page_spec(p) for p in range(n_pages)] * 2,
            out_specs=tok_spec),
        compiler_params=_cparams(("parallel",)),
        name="attn_sample",
    )(page_table.reshape(-1).astype(jnp.int32), q.reshape(n, 1, SB_WIDTH),
      jnp.tile(sb_bias.astype(F32) * LOG2E, n_pages).reshape(rows, 1), _strict_upper(PAGE_SIZE), later,
      *([ck] * n_pages), *([cv] * n_pages))
    return out.reshape(n, SB_WIDTH)


def _first_index_of_max(vals, row_id, n_rows):
    m = jnp.max(vals, axis=0, keepdims=True)
    first = jnp.min(jnp.where(vals == m, row_id, n_rows), axis=0, keepdims=True)
    return m, first


def _route(logits_t, rbias):
    tm = logits_t.shape[1]
    scores = jax.nn.sigmoid(logits_t)
    biased = scores + rbias
    row8 = lax.broadcasted_iota(jnp.int32, (GROUP_SIZE, tm), 0)
    neg = jnp.float32(-jnp.inf)
    grp_score = []
    for g in range(N_GROUPS):
        blk = biased[g * GROUP_SIZE:(g + 1) * GROUP_SIZE]
        m1, first = _first_index_of_max(blk, row8, GROUP_SIZE)
        m2 = jnp.max(jnp.where(row8 == first, neg, blk), axis=0, keepdims=True)
        grp_score.append(m1 + m2)
    masked = []
    for g in range(N_GROUPS):
        beaten_by = jnp.zeros((1, tm), jnp.int32)
        for o in range(N_GROUPS):
            if o == g:
                continue
            wins = (grp_score[o] > grp_score[g]) if o > g else (grp_score[o] >= grp_score[g])
            beaten_by = beaten_by + wins.astype(jnp.int32)
        keep = beaten_by < TOPK_GROUPS
        masked.append(jnp.where(keep, biased[g * GROUP_SIZE:(g + 1) * GROUP_SIZE], neg))
    cur = jnp.concatenate(masked, axis=0)
    row = lax.broadcasted_iota(jnp.int32, (N_EXPERTS, tm), 0)
    chosen = jnp.zeros((N_EXPERTS, tm), jnp.bool_)
    for _ in range(TOP_K):
        _, first = _first_index_of_max(cur, row, N_EXPERTS)
        pick = row == first
        chosen = jnp.logical_or(chosen, pick)
        cur = jnp.where(pick, neg, cur)
    sel = jnp.where(chosen, scores, 0.0)
    return sel / jnp.sum(sel, axis=0, keepdims=True) * ROUTED_SCALE


def _post_kernel(x_ref, g_ref, o_ref, sga_ref, sgb_ref, gate1_ref, shift2_ref, scale2_ref, g2_ref,
                 wga_ref, wgb_ref, wsb_ref, wout_ref, wr_ref, rb_ref,
                 x1_ref, h2_ref, gates_ref):
    g = g_ref[...]
    branch_a = _dot(g, wga_ref[...]) * jax.nn.sigmoid(_dot(g, wgb_ref[...]))
    branch_b = _dot(o_ref[...], wsb_ref[...])
    merged = sga_ref[...].astype(F32) * branch_a + sgb_ref[...].astype(F32) * branch_b
    x1 = x_ref[...] + gate1_ref[...] * _dot(merged.astype(BF16), wout_ref[...])
    x1_ref[...] = x1
    r = lax.rsqrt(jnp.mean(x1 * x1, axis=-1, keepdims=True) + RMS_EPS)
    h2 = (x1 * r) * g2_ref[...]
    h2 = h2 * (1.0 + scale2_ref[...]) + shift2_ref[...]
    h2_ref[...] = h2.astype(BF16)
    h_hi, h_lo = _split_bf16(h2)
    w_hi, w_lo = _split_bf16(wr_ref[...])
    logits_t = _dot_t(w_hi, h_hi) + _dot_t(w_hi, h_lo) + _dot_t(w_lo, h_hi)
    gates_ref[...] = _route(logits_t, rb_ref[...])


def _post_mixer(x, g, o, sga, sgb, mod, per_token, tm, p):
    n = x.shape[0]
    tiles_per_mod = (n // mod.shape[0]) // tm if not per_token else 1
    row = lambda i: (i, 0)
    const = lambda i: (0, 0)
    blk = lambda w: pl.BlockSpec((tm, w), row)
    full = lambda a: pl.BlockSpec(a.shape, const)
    ms = lambda c: _mod_spec(per_token, tm, tiles_per_mod, c)
    weights = (p["w_glu_a"], p["w_glu_b"], p["w_sb_out"], p["w_out"], p["w_router_t"], p["router_bias"])
    return pl.pallas_call(
        _post_kernel,
        out_shape=(jax.ShapeDtypeStruct((n, D_MODEL), F32),
                   jax.ShapeDtypeStruct((n, D_MODEL), BF16),
                   jax.ShapeDtypeStruct((N_EXPERTS, n), F32)),
        grid=(n // tm,),
        in_specs=[blk(D_MODEL), blk(512), blk(512), blk(1024), blk(1024), ms(2), ms(3), ms(4),
                  pl.BlockSpec((1, D_MODEL), const)] + [full(w) for w in weights],
        out_specs=(blk(D_MODEL), blk(D_MODEL), pl.BlockSpec((N_EXPERTS, tm), lambda i: (0, i))),
        compiler_params=_cparams(("parallel",)),
        name="post_mixer",
    )(x, g, o, sga, sgb, mod, mod, mod, p["norm2_g"], *weights)


def _moe_kernel(h_ref, gates_ref, x1_ref, gate2_ref, wgs_ref, wus_ref, wds_ref,
                wg_ref, wu_ref, wd_ref, y_ref, acc_ref):
    j = pl.program_id(1)
    h = h_ref[...]

    @pl.when(j == 0)
    def _():
        act = _silu(_dot(h, wgs_ref[...])) * _dot(h, wus_ref[...])
        acc_ref[...] = _dot(act.astype(BF16), wds_ref[...])

    gates = gates_ref[...]
    lane = lax.broadcasted_iota(jnp.int32, gates.shape, 1)
    scaled = []
    for s in range(EXPERTS_PER_STEP):
        act = _silu(_dot(h, wg_ref[s].astype(BF16))) * _dot(h, wu_ref[s].astype(BF16))
        gcol = jnp.sum(jnp.where(lane == j * EXPERTS_PER_STEP + s, gates, 0.0), axis=1, keepdims=True)
        scaled.append((act * gcol).astype(BF16))
    acc_ref[...] += _dot(jnp.concatenate(scaled, axis=1), wd_ref[...].astype(BF16))

    @pl.when(j == pl.num_programs(1) - 1)
    def _():
        y_ref[...] = x1_ref[...] + gate2_ref[...] * acc_ref[...]


def _moe(h2, gates, x1, mod, per_token, tm, p):
    n = h2.shape[0]
    tiles_per_mod = (n // mod.shape[0]) // tm if not per_token else 1
    row = lambda i, e: (i, 0)
    const = lambda i, e: (0, 0)
    exp = lambda i, e: (e, 0, 0)
    if per_token:
        gate2_spec = pl.BlockSpec((tm, D_MODEL), lambda i, e: (i, 5))
    else:
        gate2_spec = pl.BlockSpec((None, 1, D_MODEL), lambda i, e: (i // tiles_per_mod, 0, 5))
    wide = EXPERTS_PER_STEP * EXPERT_HIDDEN
    return pl.pallas_call(
        _moe_kernel,
        out_shape=jax.ShapeDtypeStruct((n, D_MODEL), F32),
        grid=(n // tm, N_EXPERTS // EXPERTS_PER_STEP),
        in_specs=[pl.BlockSpec((tm, D_MODEL), row),
                  pl.BlockSpec((tm, N_EXPERTS), row),
                  pl.BlockSpec((tm, D_MODEL), row),
                  gate2_spec,
                  pl.BlockSpec((D_MODEL, EXPERT_HIDDEN), const),
                  pl.BlockSpec((D_MODEL, EXPERT_HIDDEN), const),
                  pl.BlockSpec((EXPERT_HIDDEN, D_MODEL), const),
                  pl.BlockSpec((EXPERTS_PER_STEP, D_MODEL, EXPERT_HIDDEN), exp),
                  pl.BlockSpec((EXPERTS_PER_STEP, D_MODEL, EXPERT_HIDDEN), exp),
                  pl.BlockSpec((None, wide, D_MODEL), exp)],
        out_specs=pl.BlockSpec((tm, D_MODEL), row),
        scratch_shapes=[pltpu.VMEM((tm, D_MODEL), F32)],
        compiler_params=_cparams(("parallel", "arbitrary")),
        name="moe",
    )(h2, gates, x1, mod, p["w_gate_s"], p["w_up_s"], p["w_down_s"], p["w_gate_e"], p["w_up_e"], p["w_down_e"])


def _dispatch_kernel(gates_ref, h_ref, tri_ref, xs_ref, flag_ref):
    g = gates_ref[...]
    sel = g > 0.0
    rank = _dot(sel.astype(F32).astype(BF16), tri_ref[...])
    rank = jnp.where(sel, rank, -1.0)
    over = jnp.where(rank >= ROUTE_CAP, 1.0, 0.0)
    flag_ref[...] = jnp.broadcast_to(jnp.max(over, axis=1, keepdims=True), flag_ref.shape)
    slot = lax.broadcasted_iota(jnp.int32, (ROUTE_CAP, g.shape[1]), 0).astype(F32)
    h = h_ref[...]
    per_dot = 8
    for e0 in range(0, N_EXPERTS, per_dot):
        place = jnp.concatenate([(slot == rank[e:e + 1, :]).astype(F32).astype(BF16)
                                 for e in range(e0, e0 + per_dot)], axis=0)
        xs_ref[e0 * ROUTE_CAP:(e0 + per_dot) * ROUTE_CAP, :] = _dot(place, h).astype(BF16)


def _expert_kernel(x_ref, wg_ref, wu_ref, wd_ref, y_ref):
    nb, cap, d = x_ref.shape
    x = x_ref[...].reshape(nb * cap, d)
    n_part = 4 if nb % 4 == 0 else (2 if nb % 2 == 0 else 1)
    rows = nb * cap // n_part
    wg, wu, wd = wg_ref[...].astype(BF16), wu_ref[...].astype(BF16), wd_ref[...].astype(BF16)
    for r in range(n_part):
        xr = x[r * rows:(r + 1) * rows]
        act = _silu(_dot(xr, wg)) * _dot(xr, wu)
        y = _dot(act.astype(BF16), wd).astype(BF16)
        y_ref[r * (nb // n_part):(r + 1) * (nb // n_part)] = y.reshape(nb // n_part, cap, d)


def _combine_kernel(gates_ref, ys_ref, h_ref, x1_ref, gate2_ref, wgs_ref, wus_ref, wds_ref,
                    tri_ref, y_ref):
    g = gates_ref[...]
    sel = g > 0.0
    rank = _dot(sel.astype(F32).astype(BF16), tri_ref[...])
    rank = jnp.where(sel, rank, -1.0)
    slot = lax.broadcasted_iota(jnp.int32, (ROUTE_CAP, g.shape[1]), 0).astype(F32)
    weigh = jnp.concatenate([jnp.where(slot == rank[e:e + 1, :], g[e:e + 1, :], 0.0).astype(BF16)
                             for e in range(N_EXPERTS)], axis=0)
    routed = lax.dot_general(weigh, ys_ref[...], (((0,), (0,)), ((), ())), preferred_element_type=F32)
    h = h_ref[...]
    act = _silu(_dot(h, wgs_ref[...])) * _dot(h, wus_ref[...])
    shared = _dot(act.astype(BF16), wds_ref[...])
    y_ref[...] = x1_ref[...] + gate2_ref[...] * (shared + routed)


def _moe_routed(h2, gates_t, x1, mod, p):
    n = h2.shape[0]
    bt = ROUTE_BLOCK
    nb = n // bt
    slots = N_EXPERTS * ROUTE_CAP
    tiles_per_mod = (n // mod.shape[0]) // bt
    later = _strict_upper(bt)
    const2 = lambda b: (0, 0)
    xs, flags = pl.pallas_call(
        _dispatch_kernel,
        out_shape=(jax.ShapeDtypeStruct((nb, slots, D_MODEL), BF16),
                   jax.ShapeDtypeStruct((nb, N_EXPERTS, 128), F32)),
        grid=(nb,),
        in_specs=[pl.BlockSpec((N_EXPERTS, bt), lambda b: (0, b)),
                  pl.BlockSpec((bt, D_MODEL), lambda b: (b, 0)),
                  pl.BlockSpec((bt, bt), const2)],
        out_specs=(pl.BlockSpec((None, slots, D_MODEL), lambda b: (b, 0, 0)),
                   pl.BlockSpec((None, N_EXPERTS, 128), lambda b: (b, 0, 0))),
        compiler_params=_cparams(("parallel",)),
        name="moe_dispatch",
    )(gates_t, h2, later.T)
    gates = gates_t.T

    n_half = 1
    xblk = pl.BlockSpec((nb // n_half, None, ROUTE_CAP, D_MODEL), lambda e, f: (f, e, 0, 0))
    ys = pl.pallas_call(
        _expert_kernel,
        out_shape=jax.ShapeDtypeStruct((nb, N_EXPERTS, ROUTE_CAP, D_MODEL), BF16),
        grid=(N_EXPERTS, n_half),
        in_specs=[xblk,
                  pl.BlockSpec((None, D_MODEL, EXPERT_HIDDEN), lambda e, f: (e, 0, 0)),
                  pl.BlockSpec((None, D_MODEL, EXPERT_HIDDEN), lambda e, f: (e, 0, 0)),
                  pl.BlockSpec((None, EXPERT_HIDDEN, D_MODEL), lambda e, f: (e, 0, 0))],
        out_specs=xblk,
        compiler_params=_cparams(("parallel", "arbitrary")),
        name="moe_experts",
    )(xs.reshape(nb, N_EXPERTS, ROUTE_CAP, D_MODEL), p["w_gate_e"], p["w_up_e"],
      p["w_down_e"].reshape(N_EXPERTS, EXPERT_HIDDEN, D_MODEL))
    blk = lambda w: pl.BlockSpec((bt, w), lambda b: (b, 0))
    full = lambda a: pl.BlockSpec(a.shape, const2)
    y = pl.pallas_call(
        _combine_kernel,
        out_shape=jax.ShapeDtypeStruct((n, D_MODEL), F32),
        grid=(nb,),
        in_specs=[pl.BlockSpec((N_EXPERTS, bt), lambda b: (0, b)),
                  pl.BlockSpec((None, slots, D_MODEL), lambda b: (b, 0, 0)),
                  blk(D_MODEL), blk(D_MODEL), _mod_spec(False, bt, tiles_per_mod, 5),
                  full(p["w_gate_s"]), full(p["w_up_s"]), full(p["w_down_s"]), pl.BlockSpec((bt, bt), const2)],
        out_specs=blk(D_MODEL),
        compiler_params=_cparams(("parallel",)),
        name="moe_combine",
    )(gates_t, ys.reshape(nb, slots, D_MODEL), h2, x1, mod, p["w_gate_s"], p["w_up_s"], p["w_down_s"],
      later.T)

    over = (flags[:, :, 0] > 0.0).reshape(-1)
    n_over = jnp.sum(over.astype(jnp.int32))
    pairs = jnp.argsort(jnp.logical_not(over), stable=True)[:MAX_FIX_PAIRS].astype(jnp.int32)
    case = (n_over > 0).astype(jnp.int32) + (n_over > MAX_FIX_PAIRS).astype(jnp.int32)
    return lax.switch(case, [lambda: y,
                             lambda: _moe_fix(y, pairs // N_EXPERTS, pairs % N_EXPERTS, n_over.reshape(1),
                                              h2, gates, mod, later, p),
                             lambda: _moe(h2, gates, x1, mod, False, 512, p)])


def _moe_fix_kernel(block_ref, expert_ref, n_ref, h_ref, gates_ref, gate2_ref, wg_ref, wu_ref, wd_ref,
                    tri_ref, y_in_ref, y_ref):
    i = pl.program_id(0)
    active = i < n_ref[0]
    new_block = jnp.logical_or(i == 0, block_ref[i] != block_ref[jnp.maximum(i - 1, 0)])

    @pl.when(jnp.logical_and(active, new_block))
    def _():
        y_ref[...] = y_in_ref[...]

    @pl.when(active)
    def _():
        g = gates_ref[...]
        sel = g > 0.0
        rank = _dot(tri_ref[...], sel.astype(F32).astype(BF16))
        lane = lax.broadcasted_iota(jnp.int32, g.shape, 1)
        missed = jnp.logical_and(jnp.logical_and(sel, rank >= ROUTE_CAP), lane == expert_ref[i])
        gcol = jnp.sum(jnp.where(missed, g, 0.0), axis=1, keepdims=True)
        h = h_ref[...]
        act = _silu(_dot(h, wg_ref[...].astype(BF16))) * _dot(h, wu_ref[...].astype(BF16))
        y_ref[...] += gate2_ref[...] * _dot((act * gcol).astype(BF16), wd_ref[...].astype(BF16))


def _moe_fix(y, pair_block, pair_expert, n_pairs, h2, gates, mod, later, p):
    n = h2.shape[0]
    bt = ROUTE_BLOCK
    tiles_per_mod = (n // mod.shape[0]) // bt
    last = lambda i, nf: jnp.minimum(i, jnp.maximum(nf[0] - 1, 0))
    row = lambda i, pb, pe, nf: (pb[last(i, nf)], 0)
    exp = lambda i, pb, pe, nf: (pe[last(i, nf)], 0, 0)
    return pl.pallas_call(
        _moe_fix_kernel,
        out_shape=jax.ShapeDtypeStruct((n, D_MODEL), F32),
        grid_spec=pltpu.PrefetchScalarGridSpec(
            num_scalar_prefetch=3,
            grid=(MAX_FIX_PAIRS,),
            in_specs=[pl.BlockSpec((bt, D_MODEL), row),
                      pl.BlockSpec((bt, N_EXPERTS), row),
                      pl.BlockSpec((None, 1, D_MODEL), lambda i, pb, pe, nf: (pb[last(i, nf)] // tiles_per_mod, 0, 5)),
                      pl.BlockSpec((None, D_MODEL, EXPERT_HIDDEN), exp),
                      pl.BlockSpec((None, D_MODEL, EXPERT_HIDDEN), exp),
                      pl.BlockSpec((None, EXPERT_HIDDEN, D_MODEL), exp),
                      pl.BlockSpec((bt, bt), lambda i, pb, pe, nf: (0, 0)),
                      pl.BlockSpec((bt, D_MODEL), row)],
            out_specs=pl.BlockSpec((bt, D_MODEL), row)),
        input_output_aliases={10: 0},
        compiler_params=_cparams(("arbitrary",)),
        name="moe_fix",
    )(pair_block, pair_expert, n_pairs, h2, gates, mod, p["w_gate_e"], p["w_up_e"],
      p["w_down_e"].reshape(N_EXPERTS, EXPERT_HIDDEN, D_MODEL), later, y)


def _layer(x_prompt, x_sample, cache_k, cache_v, s0_re, s0_im, page_table, c_prompt, c_sample, p):
    batch, t_len, d = x_prompt.shape
    n_s = x_sample.shape[0]
    n_p = batch * t_len

    c_all = jnp.concatenate([c_prompt, c_sample], axis=0)
    pad = (-c_all.shape[0]) % 8
    c_all = jnp.pad(c_all, ((0, pad), (0, 0)))
    mod = _ada_mod(c_all, p["w_ada"], p["b_ada"])
    mod_p = mod[:batch].reshape(batch, 1, 6 * d)
    mod_s = mod[batch:batch + n_s]

    a_re, a_im, bb_re, bb_im = _ssm_prep(p["lam_re"], p["lam_im"], p["log_dt"], p["b_re"], p["b_im"])
    ssm = (a_re, a_im, bb_re, bb_im, _blockdiag_out(p["c_re"]), _blockdiag_out(-p["c_im"]),
           p["d_skip"].astype(F32).reshape(1, SSM_WIDTH))

    xp = x_prompt.reshape(n_p, d)
    xs = x_sample.reshape(n_s, d)
    proj_p = _in_proj(xp, mod_p, False, 512, t_len, p["norm1_g"], p["w_in"], p["q_norm_g"], p["k_norm_g"])
    proj_s = _in_proj(xs, mod_s, True, n_s, n_s, p["norm1_g"], p["w_in"], p["q_norm_g"], p["k_norm_g"])
    u_p, q_p, k_p, kb_p, v_p, vb_p, sga_p, sgb_p = proj_p
    u_s, q_s, k_s, _, v_s, _, sga_s, sgb_s = proj_s

    g_p, st_re_p, st_im_p = _ssm_prompt(u_p.reshape(batch, t_len, SSM_WIDTH), ssm)
    g_p = g_p.reshape(n_p, SSM_WIDTH)
    g_s, st_re_s, st_im_s = _ssm_sample(u_s, s0_re.reshape(n_s, SSM_FLAT), s0_im.reshape(n_s, SSM_FLAT), ssm)

    o_p = _attn_prompt(q_p, kb_p, vb_p, p["sb_bias"], batch, t_len)
    o_s = _attn_sample(q_s, cache_k, cache_v, page_table, p["sb_bias"])

    x1_p, h2_p, gates_p = _post_mixer(xp, g_p, o_p, sga_p, sgb_p, mod_p, False, 512, p)
    x1_s, h2_s, gates_s = _post_mixer(xs, g_s, o_s, sga_s, sgb_s, mod_s, True, n_s, p)

    y_p = _moe_routed(h2_p, gates_p, x1_p, mod_p, p)
    y_s = _moe(h2_s, gates_s.T, x1_s, mod_s, True, n_s, p)

    hd = (SB_HEADS, SB_HEAD_DIM)
    gs = (SSM_GROUPS, SSM_STATE)

    def kv_out(a_t, n_seq, seq_len):
        return a_t.reshape(n_seq, *hd, seq_len).transpose(0, 3, 1, 2)

    return (y_p.reshape(batch, t_len, d), y_s.reshape(n_s, 1, d),
            kv_out(k_p, batch, t_len), kv_out(v_p, batch, t_len),
            st_re_p.reshape(batch, *gs), st_im_p.reshape(batch, *gs),
            kv_out(k_s, 1, n_s).reshape(n_s, 1, *hd), kv_out(v_s, 1, n_s).reshape(n_s, 1, *hd),
            st_re_s.reshape(n_s, *gs), st_im_s.reshape(n_s, *gs))


def kernel(x_prompt, x_sample, cache_k, cache_v, state_ssm_re, state_ssm_im, page_table, c_prompt, c_sample,
           w_ada, b_ada, norm1_g, norm2_g, w_in, q_norm_g, k_norm_g, sb_bias, ssm_lambda_re, ssm_lambda_im,
           ssm_log_dt, ssm_b_re, ssm_b_im, ssm_c_re, ssm_c_im, ssm_d, w_glu_a, w_glu_b, w_sb_out, w_out,
           w_router, router_bias, w_gate_e, w_up_e, w_down_e, w_gate_s, w_up_s, w_down_s):
    depth = w_in.shape[0]
    assert depth == 1, "single-layer step"
    l = 0
    p = dict(
        w_ada=w_ada[l], b_ada=b_ada[l],
        norm1_g=norm1_g[l], norm2_g=norm2_g[l].reshape(1, D_MODEL),
        w_in=w_in[l].astype(BF16), q_norm_g=q_norm_g[l], k_norm_g=k_norm_g[l], sb_bias=sb_bias[l],
        lam_re=ssm_lambda_re[l], lam_im=ssm_lambda_im[l], log_dt=ssm_log_dt[l],
        b_re=ssm_b_re[l], b_im=ssm_b_im[l], c_re=ssm_c_re[l], c_im=ssm_c_im[l], d_skip=ssm_d[l],
        w_glu_a=w_glu_a[l].astype(BF16), w_glu_b=w_glu_b[l].astype(BF16),
        w_sb_out=w_sb_out[l].astype(BF16), w_out=w_out[l].astype(BF16),
        w_router_t=w_router[l].T, router_bias=router_bias[l].reshape(N_EXPERTS, 1),
        w_gate_e=w_gate_e[l], w_up_e=w_up_e[l],
        w_down_e=w_down_e[l].reshape(N_EXPERTS // EXPERTS_PER_STEP, EXPERTS_PER_STEP * EXPERT_HIDDEN, D_MODEL),
        w_gate_s=w_gate_s[l].astype(BF16), w_up_s=w_up_s[l].astype(BF16), w_down_s=w_down_s[l].astype(BF16),
    )
    outs = _layer(x_prompt, x_sample, cache_k[l], cache_v[l], state_ssm_re[l], state_ssm_im[l],
                  page_table, c_prompt, c_sample, p)
    y_p, y_s = outs[0], outs[1]
    return (y_p, y_s) + tuple(o[None] for o in outs[2:])
```
